```python
import jax, jax.numpy as jnp
from jax import lax
import numpy as np

D_MODEL = 2048
BATCH = 4
SEQ = 2048
DEPTH = 2
DEC_BATCH = 8
DEC_SEQ = 32
PAST_LEN = 1024

CHUNK = 64
SUB = 16
N_MIXERS = 2
N_HG_LAYERS = (DEPTH + 1) // 2
N_MLA_LAYERS = DEPTH // 2
HG_EXPAND = 128
HG_HEADS = D_MODEL // HG_EXPAND
HG_DK = HG_EXPAND
HG_DV = D_MODEL // HG_HEADS
HG_DIM = HG_HEADS * HG_DK
MLA_HEADS = 16
Q_LORA = 512
KV_LORA = 512
NOPE_DIM = 128
ROPE_DIM = 64
V_DIM = 128
ROPE_BASE = 10000.0
Q_BLOCK = 128
D_FF = 5632
CONV_K = 3
EPS = 1e-6

kernel_name = 'hybrid_hgrn2_mla_convffn_stream_step'

F32 = jnp.float32


def rms_norm(x, w):
    xf = x.astype(F32)
    y = xf * lax.rsqrt(jnp.mean(xf * xf, axis=-1, keepdims=True) + EPS)
    return (y * w.astype(F32)).astype(x.dtype)


def _gla_chunk(S, inp):
    q, k, v, logf = inp
    B, H, C, dk = q.shape
    ns = C // SUB
    b = jnp.cumsum(logf, axis=2)
    bs = b.reshape(B, H, ns, SUB, dk)
    qs = q.reshape(B, H, ns, SUB, dk)
    ks = k.reshape(B, H, ns, SUB, dk)
    vs = v.reshape(B, H, ns, SUB, -1)
    ref = jnp.concatenate([jnp.zeros_like(bs[:, :, :1, -1]), bs[:, :, :-1, -1]], axis=2)
    q_ref = qs * jnp.exp(bs - ref[:, :, :, None])
    blk = jnp.arange(ns)
    earlier = blk[None, :] < blk[:, None]
    expo = ref[:, :, :, None, None, :] - bs[:, :, None]
    k_ref = ks[:, :, None] * jnp.exp(jnp.where(earlier[:, :, None, None], expo, -jnp.inf))
    a_off = jnp.einsum('bhitk,bhijuk->bhitju', q_ref, k_ref)
    o = jnp.einsum('bhitju,bhjuv->bhitv', a_off, vs)
    pos = jnp.arange(SUB)
    causal = pos[None, :] <= pos[:, None]
    d_expo = bs[:, :, :, :, None, :] - bs[:, :, :, None, :, :]
    decay = jnp.exp(jnp.where(causal[:, :, None], d_expo, -jnp.inf))
    a_diag = jnp.einsum('bhitk,bhituk,bhiuk->bhitu', qs, decay, ks)
    o = o + jnp.einsum('bhitu,bhiuv->bhitv', a_diag, vs)
    o = o.reshape(B, H, C, -1) + jnp.einsum('bhck,bhkv->bhcv', q * jnp.exp(b), S)
    b_last = b[:, :, -1]
    S_new = jnp.exp(b_last)[..., None] * S + jnp.einsum(
        'bhck,bhcv->bhkv', k * jnp.exp(b_last[:, :, None] - b), v)
    return S_new, o


def hgrn2_recurrence(q, k, v, logf, S0):
    B, H, L, _ = q.shape
    n = -(-L // CHUNK)
    padw = ((0, 0), (0, 0), (0, n * CHUNK - L), (0, 0))
    q, k, v, logf = (jnp.pad(t, padw) for t in (q, k, v, logf))
    to_chunks = lambda t: jnp.moveaxis(t.reshape(B, H, n, CHUNK, t.shape[-1]), 2, 0)
    S, o = lax.scan(_gla_chunk, S0, (to_chunks(q), to_chunks(k), to_chunks(v), to_chunks(logf)))
    o = jnp.moveaxis(o, 0, 2).reshape(B, H, n * CHUNK, -1)[:, :, :L]
    return o, S


def hgrn2_mixer(h, S0, w_in, g_norm_w, w_out, lb):
    B, L, _ = h.shape
    q, f, i, g = jnp.split(h @ w_in, 4, axis=-1)
    heads = lambda t, d: t.reshape(B, L, HG_HEADS, d).transpose(0, 2, 1, 3).astype(F32)
    q = jax.nn.silu(heads(q, HG_DK))
    lb = lb.astype(F32).reshape(HG_HEADS, 1, HG_DK)
    fg = lb + (1.0 - lb) * jax.nn.sigmoid(heads(f, HG_DK))
    k = 1.0 - fg
    logf = jnp.log(fg)
    v = heads(i, HG_DV)
    o, S = hgrn2_recurrence(q, k, v, logf, S0.astype(F32))
    o = o.transpose(0, 2, 1, 3)
    o = rms_norm(o, g_norm_w) * jax.nn.silu(g.reshape(B, L, HG_HEADS, HG_DV).astype(F32))
    y = o.reshape(B, L, HG_HEADS * HG_DV).astype(h.dtype) @ w_out
    return y, S.astype(h.dtype)


def rope(x, pos):
    half = ROPE_DIM // 2
    inv = ROPE_BASE ** (-jnp.arange(half, dtype=F32) / half)
    ang = pos[:, None] * inv[None]
    shape = (ang.shape[0],) + (1,) * (x.ndim - 3) + (half,)
    cos = jnp.cos(ang).reshape(shape)
    sin = jnp.sin(ang).reshape(shape)
    xf = x.astype(F32)
    x1, x2 = xf[..., :half], xf[..., half:]
    return jnp.concatenate([x1 * cos - x2 * sin, x2 * cos + x1 * sin], axis=-1).astype(x.dtype)


def chunk_causal_attention(q_nope, q_pe, k_nope, k_pe, v, q_chunk, k_chunk):
    B, Lq, H, _ = q_nope.shape
    blk = Q_BLOCK if Lq % Q_BLOCK == 0 else Lq
    nb = Lq // blk
    scale = (NOPE_DIM + ROPE_DIM) ** -0.5

    def one_block(args):
        qn, qp, qc = args
        s = jnp.einsum('bqhd,bkhd->bhqk', qn, k_nope) + jnp.einsum('bqhr,bkr->bhqk', qp, k_pe)
        s = s.astype(F32) * scale
        s = jnp.where(k_chunk[None, :] <= qc[:, None], s, -jnp.inf)
        p = jax.nn.softmax(s, axis=-1).astype(v.dtype)
        return jnp.einsum('bhqk,bkhv->bqhv', p, v)

    split = lambda t: jnp.moveaxis(t.reshape((B, nb, blk) + t.shape[2:]), 1, 0)
    o = lax.map(one_block, (split(q_nope), split(q_pe), q_chunk.reshape(nb, blk)))
    return jnp.moveaxis(o, 0, 1).reshape(B, Lq, H, -1)


def mla_mixer(h, ckv_cache, kpe_cache, past, w_down, q_norm_w, kv_norm_w, w_uq, w_ukv, w_out):
    B, L, _ = h.shape
    qpos = past + jnp.arange(L)
    pos_f = qpos.astype(F32)
    down = h @ w_down
    c_q = rms_norm(down[..., :Q_LORA], q_norm_w)
    c_kv = rms_norm(down[..., Q_LORA:Q_LORA + KV_LORA], kv_norm_w)
    k_pe = rope(down[..., Q_LORA + KV_LORA:], pos_f)
    q = (c_q @ w_uq).reshape(B, L, MLA_HEADS, NOPE_DIM + ROPE_DIM)
    q_nope = q[..., :NOPE_DIM]
    q_pe = rope(q[..., NOPE_DIM:], pos_f)
    if ckv_cache is None:
        ckv_all, kpe_all = c_kv, k_pe
    else:
        ckv_all = jnp.concatenate([ckv_cache.astype(c_kv.dtype), c_kv], axis=1)
        kpe_all = jnp.concatenate([kpe_cache.astype(k_pe.dtype), k_pe], axis=1)
    Lk = ckv_all.shape[1]
    kv = (ckv_all @ w_ukv).reshape(B, Lk, MLA_HEADS, NOPE_DIM + V_DIM)
    k_nope, v = kv[..., :NOPE_DIM], kv[..., NOPE_DIM:]
    o = chunk_causal_attention(q_nope, q_pe, k_nope, kpe_all, v,
                               qpos // CHUNK, jnp.arange(Lk) // CHUNK)
    y = o.reshape(B, L, MLA_HEADS * V_DIM) @ w_out
    return y, c_kv, k_pe


def conv_ffn(h, buf, w_in, conv_w, conv_b, w_out):
    L = h.shape[1]
    u = h @ w_in
    ext = jnp.concatenate([buf.astype(u.dtype), u], axis=1)
    c = conv_b
    for j in range(CONV_K):
        c = c + conv_w[j] * ext[:, j:j + L]
    gate, up = jnp.split(c, 2, axis=-1)
    y = (jax.nn.silu(gate) * up) @ w_out
    return y, ext[:, L:]


def trunk(x, hg_states, ckv_caches, kpe_caches, conv_states, lower_bounds, norm_w,
          hgrn_w_in, hgrn_gnorm_w, hgrn_w_out, mla_w_down, mla_q_norm_w, mla_kv_norm_w,
          mla_w_uq, mla_w_ukv, mla_w_out, ffn_w_in, ffn_conv_w, ffn_conv_b, ffn_w_out):
    past = 0 if ckv_caches is None else ckv_caches.shape[2]
    new_hg, new_ckv, new_kpe, new_conv = [], [], [], []
    for layer in range(DEPTH):
        idx = layer // N_MIXERS
        h = rms_norm(x, norm_w[layer, 0])
        if layer % N_MIXERS == 0:
            m, s = hgrn2_mixer(h, hg_states[idx], hgrn_w_in[idx], hgrn_gnorm_w[idx],
                               hgrn_w_out[idx], lower_bounds[layer])
            new_hg.append(s)
        else:
            m, c, kp = mla_mixer(h,
                                 None if ckv_caches is None else ckv_caches[idx],
                                 None if kpe_caches is None else kpe_caches[idx],
                                 past, mla_w_down[idx], mla_q_norm_w[idx], mla_kv_norm_w[idx],
                                 mla_w_uq[idx], mla_w_ukv[idx], mla_w_out[idx])
            new_ckv.append(c)
            new_kpe.append(kp)
        x = x + rms_norm(m, norm_w[layer, 1])
        f, buf = conv_ffn(rms_norm(x, norm_w[layer, 2]), conv_states[layer], ffn_w_in[layer],
                          ffn_conv_w[layer], ffn_conv_b[layer], ffn_w_out[layer])
        new_conv.append(buf)
        x = x + rms_norm(f, norm_w[layer, 3])
    return x, jnp.stack(new_hg), jnp.stack(new_ckv), jnp.stack(new_kpe), jnp.stack(new_conv)


def setup_inputs(seed: int = 0) -> dict:
    key = jax.random.key(seed)
    ks = jax.random.split(key, 24)
    nrm = lambda k, shape, scale: jax.random.normal(k, shape, F32) * scale
    gain = lambda k, shape: 1.0 + 0.05 * jax.random.normal(k, shape, F32)
    return {
        'x_prompt': nrm(ks[0], (BATCH, SEQ, D_MODEL), 1.0),
        'x_sample': nrm(ks[1], (DEC_BATCH, DEC_SEQ, D_MODEL), 1.0),
        'state_hgrn': nrm(ks[2], (N_HG_LAYERS, DEC_BATCH, HG_HEADS, HG_DK, HG_DV), 0.5),
        'cache_ckv': nrm(ks[3], (N_MLA_LAYERS, DEC_BATCH, PAST_LEN, KV_LORA), 1.0),
        'cache_kpe': nrm(ks[4], (N_MLA_LAYERS, DEC_BATCH, PAST_LEN, ROPE_DIM), 1.0),
        'state_conv': nrm(ks[5], (DEPTH, DEC_BATCH, CONV_K - 1, 2 * D_FF), 1.0),
        'norm_w': gain(ks[6], (DEPTH, 4, D_MODEL)),
        'lb_logits': nrm(ks[7], (DEPTH + 1, HG_DIM), 0.1),
        'hgrn_w_in': nrm(ks[8], (N_HG_LAYERS, D_MODEL, 4 * HG_DIM), D_MODEL ** -0.5),
        'hgrn_gnorm_w': gain(ks[9], (N_HG_LAYERS, HG_DV)),
        'hgrn_w_out': nrm(ks[10], (N_HG_LAYERS, HG_HEADS * HG_DV, D_MODEL), (HG_HEADS * HG_DV) ** -0.5),
        'mla_w_down': nrm(ks[11], (N_MLA_LAYERS, D_MODEL, Q_LORA + KV_LORA + ROPE_DIM), D_MODEL ** -0.5),
        'mla_q_norm_w': gain(ks[12], (N_MLA_LAYERS, Q_LORA)),
        'mla_kv_norm_w': gain(ks[13], (N_MLA_LAYERS, KV_LORA)),
        'mla_w_uq': nrm(ks[14], (N_MLA_LAYERS, Q_LORA, MLA_HEADS * (NOPE_DIM + ROPE_DIM)), Q_LORA ** -0.5),
        'mla_w_ukv': nrm(ks[15], (N_MLA_LAYERS, KV_LORA, MLA_HEADS * (NOPE_DIM + V_DIM)), KV_LORA ** -0.5),
        'mla_w_out': nrm(ks[16], (N_MLA_LAYERS, MLA_HEADS * V_DIM, D_MODEL), (MLA_HEADS * V_DIM) ** -0.5),
        'ffn_w_in': nrm(ks[17], (DEPTH, D_MODEL, 2 * D_FF), D_MODEL ** -0.5),
        'ffn_conv_w': nrm(ks[18], (DEPTH, CONV_K, 2 * D_FF), CONV_K ** -0.5),
        'ffn_conv_b': nrm(ks[19], (DEPTH, 2 * D_FF), 0.02),
        'ffn_w_out': nrm(ks[20], (DEPTH, D_FF, D_MODEL), D_FF ** -0.5),
    }


def reference(x_prompt, x_sample, state_hgrn, cache_ckv, cache_kpe, state_conv, norm_w, lb_logits,
              hgrn_w_in, hgrn_gnorm_w, hgrn_w_out, mla_w_down, mla_q_norm_w, mla_kv_norm_w,
              mla_w_uq, mla_w_ukv, mla_w_out, ffn_w_in, ffn_conv_w, ffn_conv_b, ffn_w_out):
    lower_bounds = jnp.cumsum(jax.nn.softmax(lb_logits.astype(F32), axis=0), axis=0)
    bp = x_prompt.shape[0]
    hg0 = jnp.zeros((N_HG_LAYERS, bp, HG_HEADS, HG_DK, HG_DV), x_prompt.dtype)
    conv0 = jnp.zeros((DEPTH, bp, CONV_K - 1, 2 * D_FF), x_prompt.dtype)
    y_prompt, hg_p, ckv_p, kpe_p, conv_p = trunk(
        x_prompt, hg0, None, None, conv0, lower_bounds, norm_w,
        hgrn_w_in, hgrn_gnorm_w, hgrn_w_out, mla_w_down, mla_q_norm_w, mla_kv_norm_w,
        mla_w_uq, mla_w_ukv, mla_w_out, ffn_w_in, ffn_conv_w, ffn_conv_b, ffn_w_out)
    y_sample, hg_s, ckv_s, kpe_s, conv_s = trunk(
        x_sample, state_hgrn, cache_ckv, cache_kpe, state_conv, lower_bounds, norm_w,
        hgrn_w_in, hgrn_gnorm_w, hgrn_w_out, mla_w_down, mla_q_norm_w, mla_kv_norm_w,
        mla_w_uq, mla_w_ukv, mla_w_out, ffn_w_in, ffn_conv_w, ffn_conv_b, ffn_w_out)
    return (y_prompt, y_sample, hg_p, hg_s, ckv_p, ckv_s, kpe_p, kpe_s, conv_p, conv_s)
```

```python
import functools
import math

import jax
import jax.numpy as jnp
from jax import lax
from jax.experimental import pallas as pl
from jax.experimental.pallas import tpu as pltpu

F32 = jnp.float32
BF16 = jnp.bfloat16

EPS = 1e-6
LANES = 128
CHUNK = 64
HG_HEADS = 16
HG_DK = 128
HG_DV = 128
MLA_HEADS = 16
Q_LORA = 512
KV_LORA = 512
NOPE_DIM = 128
ROPE_DIM = 64
V_DIM = 128
ROPE_BASE = 10000.0
D_FF = 5632
FF_TILE = 512
QK_PAD = 256
VMEM_LIMIT = 56 * 1024 * 1024


def _cparams(sem):
    return pltpu.CompilerParams(dimension_semantics=sem, vmem_limit_bytes=VMEM_LIMIT)


def _rms(x, w):
    ms = jnp.mean(x * x, axis=-1, keepdims=True)
    return x * lax.rsqrt(ms + EPS) * w


def _sigmoid(x):
    return 1.0 / (1.0 + jnp.exp(-x))


def _dot(a, b):
    return jnp.dot(a, b, preferred_element_type=F32)


def _dot_nt(a, b):
    return lax.dot_general(a, b, (((1,), (1,)), ((), ())), preferred_element_type=F32)


def _dot_tn(a, b):
    return lax.dot_general(a, b, (((0,), (0,)), ((), ())), preferred_element_type=F32)


def _hgrn_in_kernel(x_ref, nw_ref, w_ref, lb_ref, o_ref, h_ref, *, lb_row, blk_per_sec):
    j = pl.program_id(1)

    @pl.when(j == 0)
    def _():
        h_ref[...] = _rms(x_ref[...], nw_ref[...]).astype(BF16)

    r = _dot(h_ref[...], w_ref[...])
    sec = j // blk_per_sec
    heads = o_ref.shape[0]

    def store(val):
        for hh in range(heads):
            o_ref[hh] = val[:, hh * LANES:(hh + 1) * LANES]

    @pl.when((sec == 0) | (sec == 3))
    def _():
        store(r * _sigmoid(r))

    @pl.when(sec == 1)
    def _():
        lg = lb_ref[...]
        e = jnp.exp(lg - jnp.max(lg, axis=0, keepdims=True))
        sm = e / jnp.sum(e, axis=0, keepdims=True)
        lb = jnp.sum(sm[0:lb_row + 1], axis=0, keepdims=True)
        store(lb + (1.0 - lb) * _sigmoid(r))

    @pl.when(sec == 2)
    def _():
        store(r)


def _hgrn_in(x, nw, w_bf, lb_logits, lb_row, tm):
    T, D = x.shape
    N = w_bf.shape[1]
    tn = 1024
    sec_w = N // 4
    blk_per_sec = sec_w // tn
    heads = tn // LANES
    return pl.pallas_call(
        functools.partial(_hgrn_in_kernel, lb_row=lb_row, blk_per_sec=blk_per_sec),
        grid=(T // tm, N // tn),
        in_specs=[
            pl.BlockSpec((tm, D), lambda i, j: (i, 0)),
            pl.BlockSpec((1, D), lambda i, j: (0, 0)),
            pl.BlockSpec((D, tn), lambda i, j: (0, j)),
            pl.BlockSpec((lb_logits.shape[0], tn), lambda i, j: (0, j % blk_per_sec)),
        ],
        out_specs=pl.BlockSpec((heads, tm, LANES), lambda i, j: (j, i, 0)),
        out_shape=jax.ShapeDtypeStruct((N // LANES, T, LANES), F32),
        scratch_shapes=[pltpu.VMEM((tm, D), BF16)],
        compiler_params=_cparams(("arbitrary", "arbitrary")),
        name="hgrn_in_proj",
    )(x, nw, w_bf, lb_logits)


def _bcast_mid(c, s):
    C = c.shape[0]
    if s >= 4:
        blk = 2 * s
        c3 = c.reshape(C // blk, blk, LANES)
        return jnp.broadcast_to(c3[:, s - 1:s, :], c3.shape).reshape(C, LANES)
    t = lax.broadcasted_iota(jnp.int32, c.shape, 0)
    if s == 2:
        ph = t & 3
        up1 = pltpu.roll(c, C - 1, 0)
        dn1 = pltpu.roll(c, 1, 0)
        dn2 = pltpu.roll(c, 2, 0)
        return jnp.where(ph == 0, up1, jnp.where(ph == 1, c, jnp.where(ph == 2, dn1, dn2)))
    dn1 = pltpu.roll(c, 1, 0)
    return jnp.where((t & 1) == 1, dn1, c)


def _gla_chunk(q, fg, v, st):
    C = q.shape[0]
    k = 1.0 - fg
    c = jnp.log(fg)
    t = lax.broadcasted_iota(jnp.int32, (C, LANES), 0)
    rr = lax.broadcasted_iota(jnp.int32, (C, C), 0)
    cc = lax.broadcasted_iota(jnp.int32, (C, C), 1)
    a = jnp.zeros((C, C), F32)
    s = 1
    while s < C:
        upper = (t & s) != 0
        bc = _bcast_mid(c, s)
        qd = jnp.where(upper, q * jnp.exp(c), 0.0)
        kd = jnp.where(upper, 0.0, k * jnp.exp(jnp.where(upper, 0.0, bc - c)))
        a_s = _dot_nt(qd.astype(BF16), kd.astype(BF16))
        if 2 * s < C:
            sh = int(math.log2(2 * s))
            a_s = jnp.where((rr >> sh) == (cc >> sh), a_s, 0.0)
        a = a + a_s
        c = c + jnp.where(upper, bc, 0.0)
        s *= 2
    b = c
    b_last = b[C - 1:C, :]
    diag = jnp.sum(q * k, axis=-1, keepdims=True)
    v_bf = v.astype(BF16)
    o = _dot(a.astype(BF16), v_bf) + diag * v
    o = o + _dot_nt((q * jnp.exp(b)).astype(BF16), st.astype(BF16))
    kdec = (k * jnp.exp(b_last - b)).astype(BF16)
    st_new = st * jnp.exp(b_last) + _dot_tn(v_bf, kdec)
    return o, st_new


def _hgrn_rec_kernel(*refs, C, n_chunks, has_state):
    if has_state:
        q_ref, f_ref, v_ref, g_ref, gw_ref, s0_ref, z_ref, so_ref, st_ref = refs
        st_ref[...] = s0_ref[0, 0].T
    else:
        q_ref, f_ref, v_ref, g_ref, gw_ref, z_ref, so_ref, st_ref = refs
        st_ref[...] = jnp.zeros_like(st_ref)
    gw = gw_ref[...]

    def body(ci, carry):
        r0 = pl.multiple_of(ci * C, C)
        rows = pl.ds(r0, C)
        o, st_new = _gla_chunk(q_ref[0, rows, :], f_ref[0, rows, :], v_ref[0, rows, :], st_ref[...])
        st_ref[...] = st_new
        z_ref[rows, :] = (_rms(o, gw) * g_ref[0, rows, :]).astype(BF16)
        return carry

    lax.fori_loop(0, n_chunks, body, 0)
    so_ref[0, 0] = st_ref[...].T


def _hgrn_rec(p, gnorm_w, s0, B, L):
    H = HG_HEADS
    T = B * L
    C = min(CHUNK, L)
    has_state = s0 is not None
    in_specs = [
        pl.BlockSpec((1, L, LANES), lambda b, h: (h, b, 0)),
        pl.BlockSpec((1, L, LANES), lambda b, h: (H + h, b, 0)),
        pl.BlockSpec((1, L, LANES), lambda b, h: (2 * H + h, b, 0)),
        pl.BlockSpec((1, L, LANES), lambda b, h: (3 * H + h, b, 0)),
        pl.BlockSpec((1, HG_DV), lambda b, h: (0, 0)),
    ]
    args = [p, p, p, p, gnorm_w]
    if has_state:
        in_specs.append(pl.BlockSpec((1, 1, HG_DK, HG_DV), lambda b, h: (b, h, 0, 0)))
        args.append(s0)
    return pl.pallas_call(
        functools.partial(_hgrn_rec_kernel, C=C, n_chunks=L // C, has_state=has_state),
        grid=(B, H),
        in_specs=in_specs,
        out_specs=[
            pl.BlockSpec((L, LANES), lambda b, h: (b, h)),
            pl.BlockSpec((1, 1, HG_DK, HG_DV), lambda b, h: (b, h, 0, 0)),
        ],
        out_shape=[
            jax.ShapeDtypeStruct((T, H * HG_DV), BF16),
            jax.ShapeDtypeStruct((B, H, HG_DK, HG_DV), F32),
        ],
        scratch_shapes=[pltpu.VMEM((HG_DV, HG_DK), F32)],
        compiler_params=_cparams(("arbitrary", "arbitrary")),
        name="hgrn_recurrence",
    )(*args)


def _out_proj_kernel(a_ref, w_ref, x_ref, nw_ref, o_ref):
    y = _dot(a_ref[...], w_ref[...])
    o_ref[...] = x_ref[...] + _rms(y, nw_ref[...])


def _out_proj(a_bf, w_bf, x, nw, tm):
    T, K = a_bf.shape
    N = w_bf.shape[1]
    return pl.pallas_call(
        _out_proj_kernel,
        grid=(T // tm,),
        in_specs=[
            pl.BlockSpec((tm, K), lambda i: (i, 0)),
            pl.BlockSpec((K, N), lambda i: (0, 0)),
            pl.BlockSpec((tm, N), lambda i: (i, 0)),
            pl.BlockSpec((1, N), lambda i: (0, 0)),
        ],
        out_specs=pl.BlockSpec((tm, N), lambda i: (i, 0)),
        out_shape=jax.ShapeDtypeStruct((T, N), F32),
        compiler_params=_cparams(("arbitrary",)),
        name="mixer_out_proj",
    )(a_bf, w_bf, x, nw)


def _ffn_kernel(*refs, ns, ls, tps, has_state):
    if has_state:
        (x_ref, nw_ref, wg_ref, wu_ref, cwg_ref, cwu_ref, cbg_ref, cbu_ref, wo_ref, nw2_ref,
         sg_ref, su_ref, o_ref, csg_ref, csu_ref, h_ref, acc_ref, work_ref, carry_ref) = refs
    else:
        (x_ref, nw_ref, wg_ref, wu_ref, cwg_ref, cwu_ref, cbg_ref, cbu_ref, wo_ref, nw2_ref,
         o_ref, csg_ref, csu_ref, h_ref, acc_ref, work_ref, carry_ref) = refs
        sg_ref = su_ref = None
    i = pl.program_id(0)
    j = pl.program_id(1)
    nj = pl.num_programs(1)
    tf = wg_ref.shape[1]

    @pl.when(j == 0)
    def _():
        h_ref[...] = _rms(x_ref[...], nw_ref[...]).astype(BF16)
        acc_ref[...] = jnp.zeros_like(acc_ref)

    h = h_ref[...]

    def conv(half, w_ref, cw_ref, cb_ref, s_ref, cs_ref):
        u3 = _dot(h, w_ref[...]).reshape(ns, ls, tf)
        work_ref[half, :, 8:8 + ls, :] = u3
        if tps == 1:
            if s_ref is None:
                work_ref[half, :, 6:8, :] = jnp.zeros((ns, 2, tf), F32)
            else:
                work_ref[half, :, 6:8, :] = s_ref[...]
        else:
            first = (i % tps) == 0

            @pl.when(first)
            def _():
                if s_ref is None:
                    work_ref[half, :, 6:8, :] = jnp.zeros((ns, 2, tf), F32)
                else:
                    work_ref[half, :, 6:8, :] = s_ref[...]

            @pl.when(jnp.logical_not(first))
            def _():
                work_ref[half, :, 6:8, :] = carry_ref[j, half]

        x1 = work_ref[half, :, 7:7 + ls, :]
        x2 = work_ref[half, :, 6:6 + ls, :]
        cw = cw_ref[...]
        c = cb_ref[...] + cw[0:1] * x2 + cw[1:2] * x1 + cw[2:3] * u3
        tail = work_ref[half, :, 6 + ls:8 + ls, :]
        cs_ref[...] = tail
        if tps > 1:
            carry_ref[j, half] = tail
        return c.reshape(ns * ls, tf)

    cg = conv(0, wg_ref, cwg_ref, cbg_ref, sg_ref, csg_ref)
    cu = conv(1, wu_ref, cwu_ref, cbu_ref, su_ref, csu_ref)
    act = (cg * _sigmoid(cg) * cu).astype(BF16)
    acc_ref[...] += _dot(act, wo_ref[...])

    @pl.when(j == nj - 1)
    def _():
        o_ref[...] = x_ref[...] + _rms(acc_ref[...], nw2_ref[...])


def _ffn(x, nw, w_in_bf, conv_w, conv_b, w_out_bf, nw2, state, n_streams, L, tm):
    T, D = x.shape
    tf = FF_TILE
    nj = D_FF // tf
    if tm >= L:
        assert tm % L == 0
        ns, ls, tps = tm // L, L, 1
    else:
        assert L % tm == 0
        ns, ls, tps = 1, tm, L // tm
    has_state = state is not None
    cb2 = conv_b.reshape(1, 2 * D_FF)

    def stream_blk(i):
        return (i * tm) // L // ns if ns > 1 else (i * tm) // L

    in_specs = [
        pl.BlockSpec((tm, D), lambda i, j: (i, 0)),
        pl.BlockSpec((1, D), lambda i, j: (0, 0)),
        pl.BlockSpec((D, tf), lambda i, j: (0, j)),
        pl.BlockSpec((D, tf), lambda i, j: (0, nj + j)),
        pl.BlockSpec((3, tf), lambda i, j: (0, j)),
        pl.BlockSpec((3, tf), lambda i, j: (0, nj + j)),
        pl.BlockSpec((1, tf), lambda i, j: (0, j)),
        pl.BlockSpec((1, tf), lambda i, j: (0, nj + j)),
        pl.BlockSpec((tf, D), lambda i, j: (j, 0)),
        pl.BlockSpec((1, D), lambda i, j: (0, 0)),
    ]
    args = [x, nw, w_in_bf, w_in_bf, conv_w, conv_w, cb2, cb2, w_out_bf, nw2]
    if has_state:
        in_specs += [
            pl.BlockSpec((ns, 2, tf), lambda i, j: (stream_blk(i), 0, j)),
            pl.BlockSpec((ns, 2, tf), lambda i, j: (stream_blk(i), 0, nj + j)),
        ]
        args += [state, state]
    out, csg, csu = pl.pallas_call(
        functools.partial(_ffn_kernel, ns=ns, ls=ls, tps=tps, has_state=has_state),
        grid=(T // tm, nj),
        in_specs=in_specs,
        out_specs=[
            pl.BlockSpec((tm, D), lambda i, j: (i, 0)),
            pl.BlockSpec((ns, 2, tf), lambda i, j: (stream_blk(i), 0, j)),
            pl.BlockSpec((ns, 2, tf), lambda i, j: (stream_blk(i), 0, j)),
        ],
        out_shape=[
            jax.ShapeDtypeStruct((T, D), F32),
            jax.ShapeDtypeStruct((n_streams, 2, D_FF), F32),
            jax.ShapeDtypeStruct((n_streams, 2, D_FF), F32),
        ],
        scratch_shapes=[
            pltpu.VMEM((tm, D), BF16),
            pltpu.VMEM((tm, D), F32),
            pltpu.VMEM((2, ns, 8 + ls, tf), F32),
            pltpu.VMEM((nj, 2, ns, 2, tf), F32),
        ],
        compiler_params=_cparams(("arbitrary", "arbitrary")),
        name="conv_ffn",
    )(*args)
    return out, jnp.concatenate([csg, csu], axis=-1)


def _rope_table_kernel(c_ref, s_ref):
    shape = c_ref.shape
    half = ROPE_DIM // 2
    pos = lax.broadcasted_iota(jnp.int32, shape, 0).astype(F32)
    lane = lax.broadcasted_iota(jnp.int32, shape, 1)
    fi = (lane & (half - 1)).astype(F32)
    inv = jnp.exp(fi * (-math.log(ROPE_BASE) / half))
    ang = pos * inv
    valid = lane < ROPE_DIM
    c_ref[...] = jnp.where(valid, jnp.cos(ang), 0.0)
    s_ref[...] = jnp.where(valid, jnp.where(lane < half, -jnp.sin(ang), jnp.sin(ang)), 0.0)


def _rope_tables(n_pos):
    return pl.pallas_call(
        _rope_table_kernel,
        out_shape=[jax.ShapeDtypeStruct((n_pos, LANES), F32)] * 2,
        name="rope_tables",
    )()


def _mla_in_kernel(x_ref, nw_ref, wd_ref, qnw_ref, kvnw_ref, wq_ref, c_ref, s_ref,
                   qcat_ref, ckv_ref, kpe_ref):
    h = _rms(x_ref[...], nw_ref[...]).astype(BF16)
    d = _dot(h, wd_ref[...])
    cq = _rms(d[:, :Q_LORA], qnw_ref[...]).astype(BF16)
    ckv_ref[...] = _rms(d[:, Q_LORA:Q_LORA + KV_LORA], kvnw_ref[...])
    cs = c_ref[...]
    sn = s_ref[...]
    o = Q_LORA + KV_LORA
    kpe_ref[...] = d[:, o:o + LANES] * cs + d[:, o + LANES:o + 2 * LANES] * sn
    hw = MLA_HEADS * LANES
    qn = _dot(cq, wq_ref[:, 0:hw])
    pr = _dot(cq, wq_ref[:, hw:2 * hw])
    ps = _dot(cq, wq_ref[:, 2 * hw:3 * hw])
    for hh in range(MLA_HEADS):
        sl = slice(hh * LANES, (hh + 1) * LANES)
        qcat_ref[hh, :, 0:LANES] = qn[:, sl].astype(BF16)
        qcat_ref[hh, :, LANES:2 * LANES] = (pr[:, sl] * cs + ps[:, sl] * sn).astype(BF16)


def _mla_in(x, nw, wd_bf, qnw, kvnw, wq_bf, cos_t, sin_t, tm, n_tab_blk):
    T, D = x.shape
    return pl.pallas_call(
        _mla_in_kernel,
        grid=(T // tm,),
        in_specs=[
            pl.BlockSpec((tm, D), lambda i: (i, 0)),
            pl.BlockSpec((1, D), lambda i: (0, 0)),
            pl.BlockSpec(wd_bf.shape, lambda i: (0, 0)),
            pl.BlockSpec((1, Q_LORA), lambda i: (0, 0)),
            pl.BlockSpec((1, KV_LORA), lambda i: (0, 0)),
            pl.BlockSpec(wq_bf.shape, lambda i: (0, 0)),
            pl.BlockSpec((tm, LANES), lambda i: (i % n_tab_blk, 0)),
            pl.BlockSpec((tm, LANES), lambda i: (i % n_tab_blk, 0)),
        ],
        out_specs=[
            pl.BlockSpec((MLA_HEADS, tm, QK_PAD), lambda i: (0, i, 0)),
            pl.BlockSpec((tm, KV_LORA), lambda i: (i, 0)),
            pl.BlockSpec((tm, LANES), lambda i: (i, 0)),
        ],
        out_shape=[
            jax.ShapeDtypeStruct((MLA_HEADS, T, QK_PAD), BF16),
            jax.ShapeDtypeStruct((T, KV_LORA), F32),
            jax.ShapeDtypeStruct((T, LANES), F32),
        ],
        compiler_params=_cparams(("arbitrary",)),
        name="mla_in_proj",
    )(x, nw, wd_bf, qnw, kvnw, wq_bf, cos_t, sin_t)


def _kv_up_kernel(ckv_ref, kpe_ref, wk_ref, wv_ref, kcat_ref, v_ref):
    c = ckv_ref[...].astype(BF16)
    kn = _dot(c, wk_ref[...])
    vv = _dot(c, wv_ref[...])
    kp = kpe_ref[...].astype(BF16)
    for hh in range(MLA_HEADS):
        sl = slice(hh * LANES, (hh + 1) * LANES)
        kcat_ref[hh, :, 0:LANES] = kn[:, sl].astype(BF16)
        kcat_ref[hh, :, LANES:2 * LANES] = kp
        v_ref[hh] = vv[:, sl].astype(BF16)


def _kv_up(ckv, kpe_pad, wk_bf, wv_bf, tm):
    R = ckv.shape[0]
    return pl.pallas_call(
        _kv_up_kernel,
        grid=(R // tm,),
        in_specs=[
            pl.BlockSpec((tm, KV_LORA), lambda i: (i, 0)),
            pl.BlockSpec((tm, LANES), lambda i: (i, 0)),
            pl.BlockSpec(wk_bf.shape, lambda i: (0, 0)),
            pl.BlockSpec(wv_bf.shape, lambda i: (0, 0)),
        ],
        out_specs=[
            pl.BlockSpec((MLA_HEADS, tm, QK_PAD), lambda i: (0, i, 0)),
            pl.BlockSpec((MLA_HEADS, tm, V_DIM), lambda i: (0, i, 0)),
        ],
        out_shape=[
            jax.ShapeDtypeStruct((MLA_HEADS, R, QK_PAD), BF16),
            jax.ShapeDtypeStruct((MLA_HEADS, R, V_DIM), BF16),
        ],
        compiler_params=_cparams(("arbitrary",)),
        name="mla_kv_up_proj",
    )(ckv, kpe_pad, wk_bf, wv_bf)


def _attn_prompt_kernel(q_ref, k_ref, v_ref, o_ref, *, L, tq, scale):
    row = lax.broadcasted_iota(jnp.int32, (tq, tq), 0)
    col = lax.broadcasted_iota(jnp.int32, (tq, tq), 1)
    sh = int(math.log2(CHUNK))
    dmask = (row >> sh) >= (col >> sh)

    def step(q, k0, m, l, acc, mask):
        k = k_ref[0, pl.ds(k0, tq), :]
        v = v_ref[0, pl.ds(k0, tq), :]
        s = _dot_nt(q, k) * scale
        if mask is not None:
            s = jnp.where(mask, s, -jnp.inf)
        m_new = jnp.maximum(m, jnp.max(s, axis=-1, keepdims=True))
        alpha = jnp.exp(m - m_new)
        p = jnp.exp(s - m_new)
        l = l * alpha + jnp.sum(p, axis=-1, keepdims=True)
        acc = acc * alpha + _dot(p.astype(BF16), v)
        return m_new, l, acc

    def q_body(qi, carry):
        q0 = pl.multiple_of(qi * tq, tq)
        q = q_ref[0, pl.ds(q0, tq), :]
        m0 = jnp.full((tq, 1), -jnp.inf, F32)
        l0 = jnp.zeros((tq, 1), F32)
        a0 = jnp.zeros((tq, V_DIM), F32)

        def kv_body(kb, c):
            return step(q, pl.multiple_of(kb * tq, tq), *c, None)

        m, l, acc = lax.fori_loop(0, qi, kv_body, (m0, l0, a0))
        m, l, acc = step(q, q0, m, l, acc, dmask)
        o_ref[pl.ds(q0, tq), :] = (acc / l).astype(BF16)
        return carry

    lax.fori_loop(0, L // tq, q_body, 0)


def _attn_prompt(qcat, kcat, v, B, L):
    H = MLA_HEADS
    scale = (NOPE_DIM + ROPE_DIM) ** -0.5
    return pl.pallas_call(
        functools.partial(_attn_prompt_kernel, L=L, tq=256, scale=scale),
        grid=(B, H),
        in_specs=[
            pl.BlockSpec((1, L, QK_PAD), lambda b, h: (h, b, 0)),
            pl.BlockSpec((1, L, QK_PAD), lambda b, h: (h, b, 0)),
            pl.BlockSpec((1, L, V_DIM), lambda b, h: (h, b, 0)),
        ],
        out_specs=pl.BlockSpec((L, V_DIM), lambda b, h: (b, h)),
        out_shape=jax.ShapeDtypeStruct((B * L, H * V_DIM), BF16),
        compiler_params=_cparams(("arbitrary", "arbitrary")),
        name="attn_prompt",
    )(qcat, kcat, v)


def _attn_sample_kernel(q_ref, kc_ref, vc_ref, kn_ref, vn_ref, o_ref, *, past, scale):
    q = q_ref[0]
    Lq = q.shape[0]
    s1 = _dot_nt(q, kc_ref[0]) * scale
    s2 = _dot_nt(q, kn_ref[0]) * scale
    sh = int(math.log2(CHUNK))
    qc = (lax.broadcasted_iota(jnp.int32, s1.shape, 0) + past) >> sh
    kc1 = lax.broadcasted_iota(jnp.int32, s1.shape, 1) >> sh
    s1 = jnp.where(kc1 <= qc, s1, -jnp.inf)
    qc2 = (lax.broadcasted_iota(jnp.int32, (Lq, Lq), 0) + past) >> sh
    kc2 = (lax.broadcasted_iota(jnp.int32, (Lq, Lq), 1) + past) >> sh
    s2 = jnp.where(kc2 <= qc2, s2, -jnp.inf)
    m = jnp.maximum(jnp.max(s1, axis=-1, keepdims=True), jnp.max(s2, axis=-1, keepdims=True))
    p1 = jnp.exp(s1 - m)
    p2 = jnp.exp(s2 - m)
    l = jnp.sum(p1, axis=-1, keepdims=True) + jnp.sum(p2, axis=-1, keepdims=True)
    acc = _dot(p1.astype(BF16), vc_ref[0]) + _dot(p2.astype(BF16), vn_ref[0])
    o_ref[...] = (acc / l).astype(BF16)


def _attn_sample(qcat, kcat_c, v_c, kcat_n, v_n, B, L, past):
    H = MLA_HEADS
    scale = (NOPE_DIM + ROPE_DIM) ** -0.5
    return pl.pallas_call(
        functools.partial(_attn_sample_kernel, past=past, scale=scale),
        grid=(B, H),
        in_specs=[
            pl.BlockSpec((1, L, QK_PAD), lambda b, h: (h, b, 0)),
            pl.BlockSpec((1, past, QK_PAD), lambda b, h: (h, b, 0)),
            pl.BlockSpec((1, past, V_DIM), lambda b, h: (h, b, 0)),
            pl.BlockSpec((1, L, QK_PAD), lambda b, h: (h, b, 0)),
            pl.BlockSpec((1, L, V_DIM), lambda b, h: (h, b, 0)),
        ],
        out_specs=pl.BlockSpec((L, V_DIM), lambda b, h: (b, h)),
        out_shape=jax.ShapeDtypeStruct((B * L, H * V_DIM), BF16),
        compiler_params=_cparams(("arbitrary", "arbitrary")),
        name="attn_sample",
    )(qcat, kcat_c, v_c, kcat_n, v_n)


def _swap_halves(w):
    half = w.shape[-1] // 2
    return jnp.concatenate([w[..., half:], w[..., :half]], axis=-1)


def _pad_lanes(w):
    return jnp.pad(w, [(0, 0)] * (w.ndim - 1) + [(0, LANES - w.shape[-1])])


def _prep_mla_weights(w_down, w_uq, w_ukv):
    o = Q_LORA + KV_LORA
    wpe = w_down[:, o:]
    wd = jnp.concatenate([w_down[:, :o], _pad_lanes(wpe), _pad_lanes(_swap_halves(wpe))], axis=-1)
    wq3 = w_uq.reshape(Q_LORA, MLA_HEADS, NOPE_DIM + ROPE_DIM)
    wn = wq3[..., :NOPE_DIM].reshape(Q_LORA, -1)
    wr = wq3[..., NOPE_DIM:]
    wrp = _pad_lanes(wr).reshape(Q_LORA, -1)
    wrs = _pad_lanes(_swap_halves(wr)).reshape(Q_LORA, -1)
    wq = jnp.concatenate([wn, wrp, wrs], axis=-1)
    wkv4 = w_ukv.reshape(KV_LORA, MLA_HEADS, 2, NOPE_DIM)
    wk = wkv4[:, :, 0].reshape(KV_LORA, -1)
    wv = wkv4[:, :, 1].reshape(KV_LORA, -1)
    return wd.astype(BF16), wq.astype(BF16), wk.astype(BF16), wv.astype(BF16)


def _trunk(x, B, L, tm, hg_state, ckv_cache, kpe_cache, conv_state, W, cos_t, sin_t, n_tab_blk):
    norm_w = W["norm_w"]
    nrow = lambda l, k: norm_w[l, k].reshape(1, -1)
    past = 0 if ckv_cache is None else ckv_cache.shape[1]

    p = _hgrn_in(x, nrow(0, 0), W["hgrn_w_in"], W["lb_logits"], 0, tm)
    z, hg_new = _hgrn_rec(p, W["hgrn_gnorm_w"].reshape(1, -1), hg_state, B, L)
    x = _out_proj(z, W["hgrn_w_out"], x, nrow(0, 1), tm)
    x, conv0 = _ffn(x, nrow(0, 2), W["ffn_w_in"][0], W["ffn_conv_w"][0], W["ffn_conv_b"][0],
                    W["ffn_w_out"][0], nrow(0, 3), None if conv_state is None else conv_state[0],
                    B, L, tm)

    qcat, ckv, kpe_pad = _mla_in(x, nrow(1, 0), W["mla_wd"], W["mla_q_norm_w"].reshape(1, -1),
                                 W["mla_kv_norm_w"].reshape(1, -1), W["mla_wq"], cos_t, sin_t,
                                 min(tm, 256), n_tab_blk)
    kcat_n, v_n = _kv_up(ckv, kpe_pad, W["mla_wk"], W["mla_wv"], min(tm, 256))
    if ckv_cache is None:
        o = _attn_prompt(qcat, kcat_n, v_n, B, L)
    else:
        cc = ckv_cache.reshape(B * past, KV_LORA)
        kc = _pad_lanes(kpe_cache.reshape(B * past, ROPE_DIM))
        kcat_c, v_c = _kv_up(cc, kc, W["mla_wk"], W["mla_wv"], 512)
        o = _attn_sample(qcat, kcat_c, v_c, kcat_n, v_n, B, L, past)
    x = _out_proj(o, W["mla_w_out"], x, nrow(1, 1), tm)
    x, conv1 = _ffn(x, nrow(1, 2), W["ffn_w_in"][1], W["ffn_conv_w"][1], W["ffn_conv_b"][1],
                    W["ffn_w_out"][1], nrow(1, 3), None if conv_state is None else conv_state[1],
                    B, L, tm)

    D = x.shape[-1]
    return (x.reshape(B, L, D), hg_new[None], ckv.reshape(1, B, L, KV_LORA),
            kpe_pad[:, :ROPE_DIM].reshape(1, B, L, ROPE_DIM), jnp.stack([conv0, conv1]))


def kernel(x_prompt, x_sample, state_hgrn, cache_ckv, cache_kpe, state_conv, norm_w, lb_logits,
           hgrn_w_in, hgrn_gnorm_w, hgrn_w_out, mla_w_down, mla_q_norm_w, mla_kv_norm_w,
           mla_w_uq, mla_w_ukv, mla_w_out, ffn_w_in, ffn_conv_w, ffn_conv_b, ffn_w_out):
    wd, wq, wk, wv = _prep_mla_weights(mla_w_down[0], mla_w_uq[0], mla_w_ukv[0])
    W = dict(
        norm_w=norm_w, lb_logits=lb_logits,
        hgrn_w_in=hgrn_w_in[0].astype(BF16), hgrn_gnorm_w=hgrn_gnorm_w[0],
        hgrn_w_out=hgrn_w_out[0].astype(BF16),
        mla_wd=wd, mla_wq=wq, mla_wk=wk, mla_wv=wv,
        mla_q_norm_w=mla_q_norm_w[0], mla_kv_norm_w=mla_kv_norm_w[0],
        mla_w_out=mla_w_out[0].astype(BF16),
        ffn_w_in=ffn_w_in.astype(BF16), ffn_conv_w=ffn_conv_w, ffn_conv_b=ffn_conv_b,
        ffn_w_out=ffn_w_out.astype(BF16),
    )
    Bp, Lp, D = x_prompt.shape
    Bs, Ls, _ = x_sample.shape
    past = cache_ckv.shape[2]
    cos_t, sin_t = _rope_tables(max(Lp, past + Ls))

    yp, hgp, ckvp, kpep, convp = _trunk(
        x_prompt.reshape(Bp * Lp, D), Bp, Lp, 512, None, None, None, None, W,
        cos_t, sin_t, Lp // 256)

    cos_s = jnp.tile(cos_t[past:past + Ls], (Bs, 1))
    sin_s = jnp.tile(sin_t[past:past + Ls], (Bs, 1))
    ys, hgs, ckvs, kpes, convs = _trunk(
        x_sample.reshape(Bs * Ls, D), Bs, Ls, Bs * Ls, state_hgrn[0], cache_ckv[0], cache_kpe[0],
        state_conv, W, cos_s, sin_s, 1)
    return (yp, ys, hgp, hgs, ckvp, ckvs, kpep, kpes, convp, convs)
```

```python
import functools
import math

import jax
import jax.numpy as jnp
from jax import lax
from jax.experimental import pallas as pl
from jax.experimental.pallas import tpu as pltpu

F32 = jnp.float32
BF16 = jnp.bfloat16

EPS = 1e-6
LANES = 128
CHUNK = 64
HG_HEADS = 16
HG_DK = 128
HG_DV = 128
MLA_HEADS = 16
Q_LORA = 512
KV_LORA = 512
NOPE_DIM = 128
ROPE_DIM = 64
V_DIM = 128
ROPE_BASE = 10000.0
D_FF = 5632
FF_TILE = 512
QK_PAD = 256
VMEM_LIMIT = 56 * 1024 * 1024


def _cparams(sem):
    return pltpu.CompilerParams(dimension_semantics=sem, vmem_limit_bytes=VMEM_LIMIT)


def _rms(x, w):
    ms = jnp.mean(x * x, axis=-1, keepdims=True)
    return x * lax.rsqrt(ms + EPS) * w


def _sigmoid(x):
    return 1.0 / (1.0 + jnp.exp(-x))


def _dot(a, b):
    return jnp.dot(a, b, preferred_element_type=F32)


def _dot_nt(a, b):
    return lax.dot_general(a, b, (((1,), (1,)), ((), ())), preferred_element_type=F32)


def _dot_tn(a, b):
    return lax.dot_general(a, b, (((0,), (0,)), ((), ())), preferred_element_type=F32)


def _hgrn_in_kernel(x_ref, nw_ref, w_ref, lb_ref, o_ref, h_ref, *, lb_row, blk_per_sec):
    j = pl.program_id(1)

    @pl.when(j == 0)
    def _():
        h_ref[...] = _rms(x_ref[...], nw_ref[...]).astype(BF16)

    r = _dot(h_ref[...], w_ref[...])
    sec = j // blk_per_sec
    heads = o_ref.shape[0]

    def store(val):
        for hh in range(heads):
            o_ref[hh] = val[:, hh * LANES:(hh + 1) * LANES]

    @pl.when((sec == 0) | (sec == 3))
    def _():
        store(r * _sigmoid(r))

    @pl.when(sec == 1)
    def _():
        lg = lb_ref[...]
        e = jnp.exp(lg - jnp.max(lg, axis=0, keepdims=True))
        sm = e / jnp.sum(e, axis=0, keepdims=True)
        lb = jnp.sum(sm[0:lb_row + 1], axis=0, keepdims=True)
        store(lb + (1.0 - lb) * _sigmoid(r))

    @pl.when(sec == 2)
    def _():
        store(r)


def _hgrn_in(x, nw, w_bf, lb_logits, lb_row, tm):
    T, D = x.shape
    N = w_bf.shape[1]
    tn = 1024
    sec_w = N // 4
    blk_per_sec = sec_w // tn
    heads = tn // LANES
    return pl.pallas_call(
        functools.partial(_hgrn_in_kernel, lb_row=lb_row, blk_per_sec=blk_per_sec),
        grid=(T // tm, N // tn),
        in_specs=[
            pl.BlockSpec((tm, D), lambda i, j: (i, 0)),
            pl.BlockSpec((1, D), lambda i, j: (0, 0)),
            pl.BlockSpec((D, tn), lambda i, j: (0, j)),
            pl.BlockSpec((lb_logits.shape[0], tn), lambda i, j: (0, j % blk_per_sec)),
        ],
        out_specs=pl.BlockSpec((heads, tm, LANES), lambda i, j: (j, i, 0)),
        out_shape=jax.ShapeDtypeStruct((N // LANES, T, LANES), F32),
        scratch_shapes=[pltpu.VMEM((tm, D), BF16)],
        compiler_params=_cparams(("arbitrary", "arbitrary")),
        name="hgrn_in_proj",
    )(x, nw, w_bf, lb_logits)


def _bcast_mid(c, s):
    C = c.shape[0]
    if s >= 4:
        blk = 2 * s
        c3 = c.reshape(C // blk, blk, LANES)
        return jnp.broadcast_to(c3[:, s - 1:s, :], c3.shape).reshape(C, LANES)
    t = lax.broadcasted_iota(jnp.int32, c.shape, 0)
    if s == 2:
        ph = t & 3
        up1 = pltpu.roll(c, C - 1, 0)
        dn1 = pltpu.roll(c, 1, 0)
        dn2 = pltpu.roll(c, 2, 0)
        return jnp.where(ph == 0, up1, jnp.where(ph == 1, c, jnp.where(ph == 2, dn1, dn2)))
    dn1 = pltpu.roll(c, 1, 0)
    return jnp.where((t & 1) == 1, dn1, c)


NEG_BIG = -1e30


def _gla_masks(C):
    t = lax.broadcasted_iota(jnp.int32, (C, LANES), 0)
    rr = lax.broadcasted_iota(jnp.int32, (C, C), 0)
    cc = lax.broadcasted_iota(jnp.int32, (C, C), 1)
    masks = []
    s = 1
    while s < C:
        sh = int(math.log2(2 * s))
        same = ((rr >> sh) == (cc >> sh)) if 2 * s < C else None
        masks.append(((t & s) != 0, same))
        s *= 2
    return masks


def _gla_chunk(q, fg, v, st, masks):
    C = q.shape[0]
    k = 1.0 - fg
    c = jnp.log2(fg)
    a = None
    s = 1
    for upper, same in masks:
        bc = _bcast_mid(c, s)
        qd = jnp.where(upper, q * jnp.exp2(c), 0.0)
        kd = k * jnp.exp2(jnp.where(upper, NEG_BIG, bc - c))
        a_s = _dot_nt(qd.astype(BF16), kd.astype(BF16))
        if same is not None:
            a_s = jnp.where(same, a_s, 0.0)
        a = a_s if a is None else a + a_s
        c = c + jnp.where(upper, bc, 0.0)
        s *= 2
    b = c
    b_last = b[C - 1:C, :]
    diag = jnp.sum(q * k, axis=-1, keepdims=True)
    v_bf = v.astype(BF16)
    o = _dot(a.astype(BF16), v_bf) + diag * v
    o = o + _dot_nt((q * jnp.exp2(b)).astype(BF16), st.astype(BF16))
    kdec = (k * jnp.exp2(b_last - b)).astype(BF16)
    st_new = st * jnp.exp2(b_last) + _dot_tn(v_bf, kdec)
    return o, st_new


def _hgrn_rec_kernel(*refs, C, n_chunks, has_state):
    if has_state:
        q_ref, f_ref, v_ref, g_ref, gw_ref, s0_ref, z_ref, so_ref, st_ref = refs
    else:
        q_ref, f_ref, v_ref, g_ref, gw_ref, z_ref, so_ref, st_ref = refs
    li = pl.program_id(2)
    hb = st_ref.shape[0]

    @pl.when(li == 0)
    def _():
        for hh in range(hb):
            st_ref[hh] = s0_ref[0, hh].T if has_state else jnp.zeros((HG_DV, HG_DK), F32)

    gw = gw_ref[...]
    masks = _gla_masks(C)

    def body(ci, carry):
        rows = pl.ds(pl.multiple_of(ci * C, C), C)
        for hh in range(hb):
            o, st_new = _gla_chunk(q_ref[hh, rows, :], f_ref[hh, rows, :], v_ref[hh, rows, :],
                                   st_ref[hh], masks)
            st_ref[hh] = st_new
            z_ref[rows, hh * LANES:(hh + 1) * LANES] = (
                _rms(o, gw) * g_ref[hh, rows, :]).astype(BF16)
        return carry

    lax.fori_loop(0, n_chunks, body, 0)

    @pl.when(li == pl.num_programs(2) - 1)
    def _():
        for hh in range(hb):
            so_ref[0, hh] = st_ref[hh].T


def _hgrn_rec(p, gnorm_w, s0, B, L):
    H = HG_HEADS
    T = B * L
    C = min(CHUNK, L)
    hb = 4
    lb = min(L, 512)
    nl = L // lb
    ng = H // hb
    has_state = s0 is not None

    def sec_spec(sec):
        return pl.BlockSpec((hb, lb, LANES), lambda b, g, l: (sec * ng + g, b * nl + l, 0))

    in_specs = [sec_spec(0), sec_spec(1), sec_spec(2), sec_spec(3),
                pl.BlockSpec((1, HG_DV), lambda b, g, l: (0, 0))]
    args = [p, p, p, p, gnorm_w]
    if has_state:
        in_specs.append(pl.BlockSpec((1, hb, HG_DK, HG_DV), lambda b, g, l: (b, g, 0, 0)))
        args.append(s0)
    return pl.pallas_call(
        functools.partial(_hgrn_rec_kernel, C=C, n_chunks=lb // C, has_state=has_state),
        grid=(B, ng, nl),
        in_specs=in_specs,
        out_specs=[
            pl.BlockSpec((lb, hb * LANES), lambda b, g, l: (b * nl + l, g)),
            pl.BlockSpec((1, hb, HG_DK, HG_DV), lambda b, g, l: (b, g, 0, 0)),
        ],
        out_shape=[
            jax.ShapeDtypeStruct((T, H * HG_DV), BF16),
            jax.ShapeDtypeStruct((B, H, HG_DK, HG_DV), F32),
        ],
        scratch_shapes=[pltpu.VMEM((hb, HG_DV, HG_DK), F32)],
        compiler_params=_cparams(("arbitrary", "arbitrary", "arbitrary")),
        name="hgrn_recurrence",
    )(*args)


def _out_proj_kernel(a_ref, w_ref, x_ref, nw_ref, o_ref):
    y = _dot(a_ref[...], w_ref[...])
    o_ref[...] = x_ref[...] + _rms(y, nw_ref[...])


def _out_proj(a_bf, w_bf, x, nw, tm):
    T, K = a_bf.shape
    N = w_bf.shape[1]
    return pl.pallas_call(
        _out_proj_kernel,
        grid=(T // tm,),
        in_specs=[
            pl.BlockSpec((tm, K), lambda i: (i, 0)),
            pl.BlockSpec((K, N), lambda i: (0, 0)),
            pl.BlockSpec((tm, N), lambda i: (i, 0)),
            pl.BlockSpec((1, N), lambda i: (0, 0)),
        ],
        out_specs=pl.BlockSpec((tm, N), lambda i: (i, 0)),
        out_shape=jax.ShapeDtypeStruct((T, N), F32),
        compiler_params=_cparams(("arbitrary",)),
        name="mixer_out_proj",
    )(a_bf, w_bf, x, nw)


def _ffn_kernel(*refs, ns, ls, tps, has_state):
    if has_state:
        (x_ref, nw_ref, wg_ref, wu_ref, cwg_ref, cwu_ref, cbg_ref, cbu_ref, wo_ref, nw2_ref,
         sg_ref, su_ref, o_ref, csg_ref, csu_ref, h_ref, acc_ref, work_ref, carry_ref) = refs
    else:
        (x_ref, nw_ref, wg_ref, wu_ref, cwg_ref, cwu_ref, cbg_ref, cbu_ref, wo_ref, nw2_ref,
         o_ref, csg_ref, csu_ref, h_ref, acc_ref, work_ref, carry_ref) = refs
        sg_ref = su_ref = None
    i = pl.program_id(0)
    j = pl.program_id(1)
    nj = pl.num_programs(1)
    tf = wg_ref.shape[1]

    @pl.when(j == 0)
    def _():
        h_ref[...] = _rms(x_ref[...], nw_ref[...]).astype(BF16)
        acc_ref[...] = jnp.zeros_like(acc_ref)

    h = h_ref[...]

    def conv(half, w_ref, cw_ref, cb_ref, s_ref, cs_ref):
        u3 = _dot(h, w_ref[...]).reshape(ns, ls, tf)
        work_ref[half, :, 8:8 + ls, :] = u3
        if tps == 1:
            if s_ref is None:
                work_ref[half, :, 6:8, :] = jnp.zeros((ns, 2, tf), F32)
            else:
                work_ref[half, :, 6:8, :] = s_ref[...]
        else:
            first = (i % tps) == 0

            @pl.when(first)
            def _():
                if s_ref is None:
                    work_ref[half, :, 6:8, :] = jnp.zeros((ns, 2, tf), F32)
                else:
                    work_ref[half, :, 6:8, :] = s_ref[...]

            @pl.when(jnp.logical_not(first))
            def _():
                work_ref[half, :, 6:8, :] = carry_ref[j, half]

        x1 = work_ref[half, :, 7:7 + ls, :]
        x2 = work_ref[half, :, 6:6 + ls, :]
        cw = cw_ref[...]
        c = cb_ref[...] + cw[0:1] * x2 + cw[1:2] * x1 + cw[2:3] * u3
        tail = work_ref[half, :, 6 + ls:8 + ls, :]
        cs_ref[...] = tail
        if tps > 1:
            carry_ref[j, half] = tail
        return c.reshape(ns * ls, tf)

    cg = conv(0, wg_ref, cwg_ref, cbg_ref, sg_ref, csg_ref)
    cu = conv(1, wu_ref, cwu_ref, cbu_ref, su_ref, csu_ref)
    act = (cg * _sigmoid(cg) * cu).astype(BF16)
    acc_ref[...] += _dot(act, wo_ref[...])

    @pl.when(j == nj - 1)
    def _():
        o_ref[...] = x_ref[...] + _rms(acc_ref[...], nw2_ref[...])


def _ffn(x, nw, w_in_bf, conv_w, conv_b, w_out_bf, nw2, state, n_streams, L, tm):
    T, D = x.shape
    tf = FF_TILE
    nj = D_FF // tf
    if tm >= L:
        assert tm % L == 0
        ns, ls, tps = tm // L, L, 1
    else:
        assert L % tm == 0
        ns, ls, tps = 1, tm, L // tm
    has_state = state is not None
    cb2 = conv_b.reshape(1, 2 * D_FF)

    def stream_blk(i):
        return (i * tm) // L // ns if ns > 1 else (i * tm) // L

    in_specs = [
        pl.BlockSpec((tm, D), lambda i, j: (i, 0)),
        pl.BlockSpec((1, D), lambda i, j: (0, 0)),
        pl.BlockSpec((D, tf), lambda i, j: (0, j)),
        pl.BlockSpec((D, tf), lambda i, j: (0, nj + j)),
        pl.BlockSpec((3, tf), lambda i, j: (0, j)),
        pl.BlockSpec((3, tf), lambda i, j: (0, nj + j)),
        pl.BlockSpec((1, tf), lambda i, j: (0, j)),
        pl.BlockSpec((1, tf), lambda i, j: (0, nj + j)),
        pl.BlockSpec((tf, D), lambda i, j: (j, 0)),
        pl.BlockSpec((1, D), lambda i, j: (0, 0)),
    ]
    args = [x, nw, w_in_bf, w_in_bf, conv_w, conv_w, cb2, cb2, w_out_bf, nw2]
    if has_state:
        in_specs += [
            pl.BlockSpec((ns, 2, tf), lambda i, j: (stream_blk(i), 0, j)),
            pl.BlockSpec((ns, 2, tf), lambda i, j: (stream_blk(i), 0, nj + j)),
        ]
        args += [state, state]
    out, csg, csu = pl.pallas_call(
        functools.partial(_ffn_kernel, ns=ns, ls=ls, tps=tps, has_state=has_state),
        grid=(T // tm, nj),
        in_specs=in_specs,
        out_specs=[
            pl.BlockSpec((tm, D), lambda i, j: (i, 0)),
            pl.BlockSpec((ns, 2, tf), lambda i, j: (i, 0, j)),
            pl.BlockSpec((ns, 2, tf), lambda i, j: (i, 0, j)),
        ],
        out_shape=[
            jax.ShapeDtypeStruct((T, D), F32),
            jax.ShapeDtypeStruct((T // ls, 2, D_FF), F32),
            jax.ShapeDtypeStruct((T // ls, 2, D_FF), F32),
        ],
        scratch_shapes=[
            pltpu.VMEM((tm, D), BF16),
            pltpu.VMEM((tm, D), F32),
            pltpu.VMEM((2, ns, 8 + ls, tf), F32),
            pltpu.VMEM((nj, 2, ns, 2, tf), F32),
        ],
        compiler_params=_cparams(("arbitrary", "arbitrary")),
        name="conv_ffn",
    )(*args)
    tails = jnp.concatenate([csg, csu], axis=-1).reshape(n_streams, tps, 2, 2 * D_FF)
    return out, tails[:, tps - 1]


def _rope_table_kernel(c_ref, s_ref):
    shape = c_ref.shape
    half = ROPE_DIM // 2
    pos = lax.broadcasted_iota(jnp.int32, shape, 0).astype(F32)
    lane = lax.broadcasted_iota(jnp.int32, shape, 1)
    fi = (lane & (half - 1)).astype(F32)
    inv = jnp.exp(fi * (-math.log(ROPE_BASE) / half))
    ang = pos * inv
    valid = lane < ROPE_DIM
    c_ref[...] = jnp.where(valid, jnp.cos(ang), 0.0)
    s_ref[...] = jnp.where(valid, jnp.where(lane < half, -jnp.sin(ang), jnp.sin(ang)), 0.0)


def _rope_tables(n_pos):
    return pl.pallas_call(
        _rope_table_kernel,
        out_shape=[jax.ShapeDtypeStruct((n_pos, LANES), F32)] * 2,
        name="rope_tables",
    )()


def _mla_in_kernel(x_ref, nw_ref, wd_ref, qnw_ref, kvnw_ref, wq_ref, c_ref, s_ref,
                   qcat_ref, ckv_ref, kpe_ref):
    h = _rms(x_ref[...], nw_ref[...]).astype(BF16)
    d = _dot(h, wd_ref[...])
    cq = _rms(d[:, :Q_LORA], qnw_ref[...]).astype(BF16)
    ckv_ref[...] = _rms(d[:, Q_LORA:Q_LORA + KV_LORA], kvnw_ref[...])
    cs = c_ref[...]
    sn = s_ref[...]
    o = Q_LORA + KV_LORA
    kpe_ref[...] = d[:, o:o + LANES] * cs + d[:, o + LANES:o + 2 * LANES] * sn
    hw = MLA_HEADS * LANES
    qn = _dot(cq, wq_ref[:, 0:hw])
    pr = _dot(cq, wq_ref[:, hw:2 * hw])
    ps = _dot(cq, wq_ref[:, 2 * hw:3 * hw])
    for hh in range(MLA_HEADS):
        sl = slice(hh * LANES, (hh + 1) * LANES)
        qcat_ref[hh, :, 0:LANES] = qn[:, sl].astype(BF16)
        qcat_ref[hh, :, LANES:2 * LANES] = (pr[:, sl] * cs + ps[:, sl] * sn).astype(BF16)


def _mla_in(x, nw, wd_bf, qnw, kvnw, wq_bf, cos_t, sin_t, tm, n_tab_blk):
    T, D = x.shape
    return pl.pallas_call(
        _mla_in_kernel,
        grid=(T // tm,),
        in_specs=[
            pl.BlockSpec((tm, D), lambda i: (i, 0)),
            pl.BlockSpec((1, D), lambda i: (0, 0)),
            pl.BlockSpec(wd_bf.shape, lambda i: (0, 0)),
            pl.BlockSpec((1, Q_LORA), lambda i: (0, 0)),
            pl.BlockSpec((1, KV_LORA), lambda i: (0, 0)),
            pl.BlockSpec(wq_bf.shape, lambda i: (0, 0)),
            pl.BlockSpec((tm, LANES), lambda i: (i % n_tab_blk, 0)),
            pl.BlockSpec((tm, LANES), lambda i: (i % n_tab_blk, 0)),
        ],
        out_specs=[
            pl.BlockSpec((MLA_HEADS, tm, QK_PAD), lambda i: (0, i, 0)),
            pl.BlockSpec((tm, KV_LORA), lambda i: (i, 0)),
            pl.BlockSpec((tm, LANES), lambda i: (i, 0)),
        ],
        out_shape=[
            jax.ShapeDtypeStruct((MLA_HEADS, T, QK_PAD), BF16),
            jax.ShapeDtypeStruct((T, KV_LORA), F32),
            jax.ShapeDtypeStruct((T, LANES), F32),
        ],
        compiler_params=_cparams(("arbitrary",)),
        name="mla_in_proj",
    )(x, nw, wd_bf, qnw, kvnw, wq_bf, cos_t, sin_t)


def _kv_up_kernel(ckv_ref, kpe_ref, wk_ref, wv_ref, kcat_ref, v_ref):
    c = ckv_ref[...].astype(BF16)
    kn = _dot(c, wk_ref[...])
    vv = _dot(c, wv_ref[...])
    kp = kpe_ref[...].astype(BF16)
    for hh in range(MLA_HEADS):
        sl = slice(hh * LANES, (hh + 1) * LANES)
        kcat_ref[hh, :, 0:LANES] = kn[:, sl].astype(BF16)
        kcat_ref[hh, :, LANES:2 * LANES] = kp
        v_ref[hh] = vv[:, sl].astype(BF16)


def _kv_up(ckv, kpe_pad, wk_bf, wv_bf, tm):
    R = ckv.shape[0]
    return pl.pallas_call(
        _kv_up_kernel,
        grid=(R // tm,),
        in_specs=[
            pl.BlockSpec((tm, KV_LORA), lambda i: (i, 0)),
            pl.BlockSpec((tm, LANES), lambda i: (i, 0)),
            pl.BlockSpec(wk_bf.shape, lambda i: (0, 0)),
            pl.BlockSpec(wv_bf.shape, lambda i: (0, 0)),
        ],
        out_specs=[
            pl.BlockSpec((MLA_HEADS, tm, QK_PAD), lambda i: (0, i, 0)),
            pl.BlockSpec((MLA_HEADS, tm, V_DIM), lambda i: (0, i, 0)),
        ],
        out_shape=[
            jax.ShapeDtypeStruct((MLA_HEADS, R, QK_PAD), BF16),
            jax.ShapeDtypeStruct((MLA_HEADS, R, V_DIM), BF16),
        ],
        compiler_params=_cparams(("arbitrary",)),
        name="mla_kv_up_proj",
    )(ckv, kpe_pad, wk_bf, wv_bf)


def _attn_prompt_kernel(q_ref, k_ref, v_ref, o_ref, *, L, tq, scale):
    row = lax.broadcasted_iota(jnp.int32, (tq, tq), 0)
    col = lax.broadcasted_iota(jnp.int32, (tq, tq), 1)
    sh = int(math.log2(CHUNK))
    dmask = (row >> sh) >= (col >> sh)

    c2 = scale * math.log2(math.e)
    for qi in range(L // tq):
        n = qi * tq
        q = q_ref[0, n:n + tq, :]
        s_d = jnp.where(dmask, _dot_nt(q, k_ref[0, n:n + tq, :]), -jnp.inf)
        m = jnp.max(s_d, axis=-1, keepdims=True)
        if qi > 0:
            s_p = _dot_nt(q, k_ref[0, 0:n, :])
            m = jnp.maximum(m, jnp.max(s_p, axis=-1, keepdims=True))
        p_d = jnp.exp2((s_d - m) * c2)
        l = jnp.sum(p_d, axis=-1, keepdims=True)
        acc = _dot(p_d.astype(BF16), v_ref[0, n:n + tq, :])
        if qi > 0:
            p_p = jnp.exp2((s_p - m) * c2)
            l = l + jnp.sum(p_p, axis=-1, keepdims=True)
            acc = acc + _dot(p_p.astype(BF16), v_ref[0, 0:n, :])
        o_ref[n:n + tq, :] = (acc / l).astype(BF16)


def _attn_prompt(qcat, kcat, v, B, L):
    H = MLA_HEADS
    scale = (NOPE_DIM + ROPE_DIM) ** -0.5
    return pl.pallas_call(
        functools.partial(_attn_prompt_kernel, L=L, tq=256, scale=scale),
        grid=(B, H),
        in_specs=[
            pl.BlockSpec((1, L, QK_PAD), lambda b, h: (h, b, 0)),
            pl.BlockSpec((1, L, QK_PAD), lambda b, h: (h, b, 0)),
            pl.BlockSpec((1, L, V_DIM), lambda b, h: (h, b, 0)),
        ],
        out_specs=pl.BlockSpec((L, V_DIM), lambda b, h: (b, h)),
        out_shape=jax.ShapeDtypeStruct((B * L, H * V_DIM), BF16),
        compiler_params=_cparams(("arbitrary", "arbitrary")),
        name="attn_prompt",
    )(qcat, kcat, v)


def _attn_sample_kernel(q_ref, kc_ref, vc_ref, kn_ref, vn_ref, o_ref, *, past, scale):
    q = q_ref[0]
    Lq = q.shape[0]
    s1 = _dot_nt(q, kc_ref[0]) * scale
    s2 = _dot_nt(q, kn_ref[0]) * scale
    sh = int(math.log2(CHUNK))
    qc = (lax.broadcasted_iota(jnp.int32, s1.shape, 0) + past) >> sh
    kc1 = lax.broadcasted_iota(jnp.int32, s1.shape, 1) >> sh
    s1 = jnp.where(kc1 <= qc, s1, -jnp.inf)
    qc2 = (lax.broadcasted_iota(jnp.int32, (Lq, Lq), 0) + past) >> sh
    kc2 = (lax.broadcasted_iota(jnp.int32, (Lq, Lq), 1) + past) >> sh
    s2 = jnp.where(kc2 <= qc2, s2, -jnp.inf)
    m = jnp.maximum(jnp.max(s1, axis=-1, keepdims=True), jnp.max(s2, axis=-1, keepdims=True))
    p1 = jnp.exp(s1 - m)
    p2 = jnp.exp(s2 - m)
    l = jnp.sum(p1, axis=-1, keepdims=True) + jnp.sum(p2, axis=-1, keepdims=True)
    acc = _dot(p1.astype(BF16), vc_ref[0]) + _dot(p2.astype(BF16), vn_ref[0])
    o_ref[...] = (acc / l).astype(BF16)


def _attn_sample(qcat, kcat_c, v_c, kcat_n, v_n, B, L, past):
    H = MLA_HEADS
    scale = (NOPE_DIM + ROPE_DIM) ** -0.5
    return pl.pallas_call(
        functools.partial(_attn_sample_kernel, past=past, scale=scale),
        grid=(B, H),
        in_specs=[
            pl.BlockSpec((1, L, QK_PAD), lambda b, h: (h, b, 0)),
            pl.BlockSpec((1, past, QK_PAD), lambda b, h: (h, b, 0)),
            pl.BlockSpec((1, past, V_DIM), lambda b, h: (h, b, 0)),
            pl.BlockSpec((1, L, QK_PAD), lambda b, h: (h, b, 0)),
            pl.BlockSpec((1, L, V_DIM), lambda b, h: (h, b, 0)),
        ],
        out_specs=pl.BlockSpec((L, V_DIM), lambda b, h: (b, h)),
        out_shape=jax.ShapeDtypeStruct((B * L, H * V_DIM), BF16),
        compiler_params=_cparams(("arbitrary", "arbitrary")),
        name="attn_sample",
    )(qcat, kcat_c, v_c, kcat_n, v_n)


def _swap_halves(w):
    half = w.shape[-1] // 2
    return jnp.concatenate([w[..., half:], w[..., :half]], axis=-1)


def _pad_lanes(w):
    return jnp.pad(w, [(0, 0)] * (w.ndim - 1) + [(0, LANES - w.shape[-1])])


def _prep_mla_weights(w_down, w_uq, w_ukv):
    o = Q_LORA + KV_LORA
    wpe = w_down[:, o:]
    wd = jnp.concatenate([w_down[:, :o], _pad_lanes(wpe), _pad_lanes(_swap_halves(wpe))], axis=-1)
    wq3 = w_uq.reshape(Q_LORA, MLA_HEADS, NOPE_DIM + ROPE_DIM)
    wn = wq3[..., :NOPE_DIM].reshape(Q_LORA, -1)
    wr = wq3[..., NOPE_DIM:]
    wrp = _pad_lanes(wr).reshape(Q_LORA, -1)
    wrs = _pad_lanes(_swap_halves(wr)).reshape(Q_LORA, -1)
    wq = jnp.concatenate([wn, wrp, wrs], axis=-1)
    wkv4 = w_ukv.reshape(KV_LORA, MLA_HEADS, 2, NOPE_DIM)
    wk = wkv4[:, :, 0].reshape(KV_LORA, -1)
    wv = wkv4[:, :, 1].reshape(KV_LORA, -1)
    return wd.astype(BF16), wq.astype(BF16), wk.astype(BF16), wv.astype(BF16)


def _trunk(x, B, L, tm, hg_state, ckv_cache, kpe_cache, conv_state, W, cos_t, sin_t, n_tab_blk):
    norm_w = W["norm_w"]
    nrow = lambda l, k: norm_w[l, k].reshape(1, -1)
    past = 0 if ckv_cache is None else ckv_cache.shape[1]

    p = _hgrn_in(x, nrow(0, 0), W["hgrn_w_in"], W["lb_logits"], 0, tm)
    z, hg_new = _hgrn_rec(p, W["hgrn_gnorm_w"].reshape(1, -1), hg_state, B, L)
    x = _out_proj(z, W["hgrn_w_out"], x, nrow(0, 1), tm)
    x, conv0 = _ffn(x, nrow(0, 2), W["ffn_w_in"][0], W["ffn_conv_w"][0], W["ffn_conv_b"][0],
                    W["ffn_w_out"][0], nrow(0, 3), None if conv_state is None else conv_state[0],
                    B, L, tm)

    qcat, ckv, kpe_pad = _mla_in(x, nrow(1, 0), W["mla_wd"], W["mla_q_norm_w"].reshape(1, -1),
                                 W["mla_kv_norm_w"].reshape(1, -1), W["mla_wq"], cos_t, sin_t,
                                 min(tm, 256), n_tab_blk)
    kcat_n, v_n = _kv_up(ckv, kpe_pad, W["mla_wk"], W["mla_wv"], min(tm, 256))
    if ckv_cache is None:
        o = _attn_prompt(qcat, kcat_n, v_n, B, L)
    else:
        cc = ckv_cache.reshape(B * past, KV_LORA)
        kc = _pad_lanes(kpe_cache.reshape(B * past, ROPE_DIM))
        kcat_c, v_c = _kv_up(cc, kc, W["mla_wk"], W["mla_wv"], 512)
        o = _attn_sample(qcat, kcat_c, v_c, kcat_n, v_n, B, L, past)
    x = _out_proj(o, W["mla_w_out"], x, nrow(1, 1), tm)
    x, conv1 = _ffn(x, nrow(1, 2), W["ffn_w_in"][1], W["ffn_conv_w"][1], W["ffn_conv_b"][1],
                    W["ffn_w_out"][1], nrow(1, 3), None if conv_state is None else conv_state[1],
                    B, L, tm)

    D = x.shape[-1]
    return (x.reshape(B, L, D), hg_new[None], ckv.reshape(1, B, L, KV_LORA),
            kpe_pad[:, :ROPE_DIM].reshape(1, B, L, ROPE_DIM), jnp.stack([conv0, conv1]))


def kernel(x_prompt, x_sample, state_hgrn, cache_ckv, cache_kpe, state_conv, norm_w, lb_logits,
           hgrn_w_in, hgrn_gnorm_w, hgrn_w_out, mla_w_down, mla_q_norm_w, mla_kv_norm_w,
           mla_w_uq, mla_w_ukv, mla_w_out, ffn_w_in, ffn_conv_w, ffn_conv_b, ffn_w_out):
    wd, wq, wk, wv = _prep_mla_weights(mla_w_down[0], mla_w_uq[0], mla_w_ukv[0])
    W = dict(
        norm_w=norm_w, lb_logits=lb_logits,
        hgrn_w_in=hgrn_w_in[0].astype(BF16), hgrn_gnorm_w=hgrn_gnorm_w[0],
        hgrn_w_out=hgrn_w_out[0].astype(BF16),
        mla_wd=wd, mla_wq=wq, mla_wk=wk, mla_wv=wv,
        mla_q_norm_w=mla_q_norm_w[0], mla_kv_norm_w=mla_kv_norm_w[0],
        mla_w_out=mla_w_out[0].astype(BF16),
        ffn_w_in=ffn_w_in.astype(BF16), ffn_conv_w=ffn_conv_w, ffn_conv_b=ffn_conv_b,
        ffn_w_out=ffn_w_out.astype(BF16),
    )
    Bp, Lp, D = x_prompt.shape
    Bs, Ls, _ = x_sample.shape
    past = cache_ckv.shape[2]
    cos_t, sin_t = _rope_tables(max(Lp, past + Ls))

    yp, hgp, ckvp, kpep, convp = _trunk(
        x_prompt.reshape(Bp * Lp, D), Bp, Lp, 512, None, None, None, None, W,
        cos_t, sin_t, Lp // 256)

    cos_s = jnp.tile(cos_t[past:past + Ls], (Bs, 1))
    sin_s = jnp.tile(sin_t[past:past + Ls], (Bs, 1))
    ys, hgs, ckvs, kpes, convs = _trunk(
        x_sample.reshape(Bs * Ls, D), Bs, Ls, Bs * Ls, state_hgrn[0], cache_ckv[0], cache_kpe[0],
        state_conv, W, cos_s, sin_s, 1)
    return (yp, ys, hgp, hgs, ckvp, ckvs, kpep, kpes, convp, convs)
```

```python
import functools
import math

import jax
import jax.numpy as jnp
from jax import lax
from jax.experimental import pallas as pl
from jax.experimental.pallas import tpu as pltpu

F32 = jnp.float32
BF16 = jnp.bfloat16

EPS = 1e-6
LANES = 128
CHUNK = 64
HG_HEADS = 16
HG_DK = 128
HG_DV = 128
MLA_HEADS = 16
Q_LORA = 512
KV_LORA = 512
NOPE_DIM = 128
ROPE_DIM = 64
V_DIM = 128
ROPE_BASE = 10000.0
D_FF = 5632
FF_TILE = 512
QK_PAD = 256
VMEM_LIMIT = 56 * 1024 * 1024


def _cparams(sem):
    return pltpu.CompilerParams(dimension_semantics=sem, vmem_limit_bytes=VMEM_LIMIT)


def _rms(x, w):
    ms = jnp.mean(x * x, axis=-1, keepdims=True)
    return x * lax.rsqrt(ms + EPS) * w


def _sigmoid(x):
    return 1.0 / (1.0 + jnp.exp(-x))


def _dot(a, b):
    return jnp.dot(a, b, preferred_element_type=F32)


def _dot_nt(a, b):
    return lax.dot_general(a, b, (((1,), (1,)), ((), ())), preferred_element_type=F32)


def _dot_tn(a, b):
    return lax.dot_general(a, b, (((0,), (0,)), ((), ())), preferred_element_type=F32)


def _hgrn_in_kernel(x_ref, nw_ref, w_ref, lb_ref, o_ref, h_ref, *, lb_row, blk_per_sec):
    j = pl.program_id(1)

    @pl.when(j == 0)
    def _():
        h_ref[...] = _rms(x_ref[...], nw_ref[...]).astype(BF16)

    r = _dot(h_ref[...], w_ref[...])
    sec = j // blk_per_sec
    heads = o_ref.shape[0]

    def store(val):
        for hh in range(heads):
            o_ref[hh] = val[:, hh * LANES:(hh + 1) * LANES]

    @pl.when((sec == 0) | (sec == 3))
    def _():
        store(r * _sigmoid(r))

    @pl.when(sec == 1)
    def _():
        lg = lb_ref[...]
        e = jnp.exp(lg - jnp.max(lg, axis=0, keepdims=True))
        sm = e / jnp.sum(e, axis=0, keepdims=True)
        lb = jnp.sum(sm[0:lb_row + 1], axis=0, keepdims=True)
        store(lb + (1.0 - lb) * _sigmoid(r))

    @pl.when(sec == 2)
    def _():
        store(r)


def _hgrn_in(x, nw, w_bf, lb_logits, lb_row, tm):
    T, D = x.shape
    N = w_bf.shape[1]
    tn = 1024
    sec_w = N // 4
    blk_per_sec = sec_w // tn
    heads = tn // LANES
    return pl.pallas_call(
        functools.partial(_hgrn_in_kernel, lb_row=lb_row, blk_per_sec=blk_per_sec),
        grid=(T // tm, N // tn),
        in_specs=[
            pl.BlockSpec((tm, D), lambda i, j: (i, 0)),
            pl.BlockSpec((1, D), lambda i, j: (0, 0)),
            pl.BlockSpec((D, tn), lambda i, j: (0, j)),
            pl.BlockSpec((lb_logits.shape[0], tn), lambda i, j: (0, j % blk_per_sec)),
        ],
        out_specs=pl.BlockSpec((heads, tm, LANES), lambda i, j: (j, i, 0)),
        out_shape=jax.ShapeDtypeStruct((N // LANES, T, LANES), F32),
        scratch_shapes=[pltpu.VMEM((tm, D), BF16)],
        compiler_params=_cparams(("arbitrary", "arbitrary")),
        name="hgrn_in_proj",
    )(x, nw, w_bf, lb_logits)


def _bcast_mid(c, s):
    C = c.shape[0]
    if s >= 4:
        blk = 2 * s
        c3 = c.reshape(C // blk, blk, LANES)
        return jnp.broadcast_to(c3[:, s - 1:s, :], c3.shape).reshape(C, LANES)
    t = lax.broadcasted_iota(jnp.int32, c.shape, 0)
    if s == 2:
        ph = t & 3
        up1 = pltpu.roll(c, C - 1, 0)
        dn1 = pltpu.roll(c, 1, 0)
        dn2 = pltpu.roll(c, 2, 0)
        return jnp.where(ph == 0, up1, jnp.where(ph == 1, c, jnp.where(ph == 2, dn1, dn2)))
    dn1 = pltpu.roll(c, 1, 0)
    return jnp.where((t & 1) == 1, dn1, c)


def _gla_masks(C):
    t = lax.broadcasted_iota(jnp.int32, (C, LANES), 0)
    rr = lax.broadcasted_iota(jnp.int32, (C, C), 0)
    cc = lax.broadcasted_iota(jnp.int32, (C, C), 1)
    masks = []
    s = 1
    while s < C:
        sh = int(math.log2(2 * s))
        pair = ((rr >> sh) == (cc >> sh)) & ((rr & s) != 0) & ((cc & s) == 0)
        masks.append(((t & s) != 0, pair))
        s *= 2
    return masks


def _gla_chunk(q, fg, v, st, masks):
    C = q.shape[0]
    k = 1.0 - fg
    c = jnp.log2(fg)
    a = jnp.zeros((C, C), F32)
    s = 1
    for upper, pair in masks:
        bc = _bcast_mid(c, s)
        qd = q * jnp.exp2(c)
        kd = k * jnp.exp2(jnp.minimum(bc - c, 0.0))
        a = jnp.where(pair, _dot_nt(qd.astype(BF16), kd.astype(BF16)), a)
        c = c + jnp.where(upper, bc, 0.0)
        s *= 2
    b = c
    b_last = b[C - 1:C, :]
    diag = jnp.sum(q * k, axis=-1, keepdims=True)
    v_bf = v.astype(BF16)
    o = _dot(a.astype(BF16), v_bf) + diag * v
    o = o + _dot_nt((q * jnp.exp2(b)).astype(BF16), st.astype(BF16))
    kdec = (k * jnp.exp2(b_last - b)).astype(BF16)
    st_new = st * jnp.exp2(b_last) + _dot_tn(v_bf, kdec)
    return o, st_new


def _hgrn_rec_kernel(*refs, C, n_chunks, has_state):
    if has_state:
        q_ref, f_ref, v_ref, g_ref, gw_ref, s0_ref, z_ref, so_ref, st_ref = refs
    else:
        q_ref, f_ref, v_ref, g_ref, gw_ref, z_ref, so_ref, st_ref = refs
    li = pl.program_id(2)
    hb = st_ref.shape[0]

    @pl.when(li == 0)
    def _():
        for hh in range(hb):
            st_ref[hh] = s0_ref[0, hh].T if has_state else jnp.zeros((HG_DV, HG_DK), F32)

    gw = gw_ref[...]
    masks = _gla_masks(C)

    def body(ci, carry):
        rows = pl.ds(pl.multiple_of(ci * C, C), C)
        for hh in range(hb):
            o, st_new = _gla_chunk(q_ref[hh, rows, :], f_ref[hh, rows, :], v_ref[hh, rows, :],
                                   st_ref[hh], masks)
            st_ref[hh] = st_new
            z_ref[rows, hh * LANES:(hh + 1) * LANES] = (
                _rms(o, gw) * g_ref[hh, rows, :]).astype(BF16)
        return carry

    lax.fori_loop(0, n_chunks, body, 0)

    @pl.when(li == pl.num_programs(2) - 1)
    def _():
        for hh in range(hb):
            so_ref[0, hh] = st_ref[hh].T


def _hgrn_rec(p, gnorm_w, s0, B, L):
    H = HG_HEADS
    T = B * L
    C = min(CHUNK, L)
    hb = 4
    lb = min(L, 512)
    nl = L // lb
    ng = H // hb
    has_state = s0 is not None

    def sec_spec(sec):
        return pl.BlockSpec((hb, lb, LANES), lambda b, g, l: (sec * ng + g, b * nl + l, 0))

    in_specs = [sec_spec(0), sec_spec(1), sec_spec(2), sec_spec(3),
                pl.BlockSpec((1, HG_DV), lambda b, g, l: (0, 0))]
    args = [p, p, p, p, gnorm_w]
    if has_state:
        in_specs.append(pl.BlockSpec((1, hb, HG_DK, HG_DV), lambda b, g, l: (b, g, 0, 0)))
        args.append(s0)
    return pl.pallas_call(
        functools.partial(_hgrn_rec_kernel, C=C, n_chunks=lb // C, has_state=has_state),
        grid=(B, ng, nl),
        in_specs=in_specs,
        out_specs=[
            pl.BlockSpec((lb, hb * LANES), lambda b, g, l: (b * nl + l, g)),
            pl.BlockSpec((1, hb, HG_DK, HG_DV), lambda b, g, l: (b, g, 0, 0)),
        ],
        out_shape=[
            jax.ShapeDtypeStruct((T, H * HG_DV), BF16),
            jax.ShapeDtypeStruct((B, H, HG_DK, HG_DV), F32),
        ],
        scratch_shapes=[pltpu.VMEM((hb, HG_DV, HG_DK), F32)],
        compiler_params=_cparams(("arbitrary", "arbitrary", "arbitrary")),
        name="hgrn_recurrence",
    )(*args)


def _out_proj_kernel(a_ref, w_ref, x_ref, nw_ref, o_ref):
    y = _dot(a_ref[...], w_ref[...])
    o_ref[...] = x_ref[...] + _rms(y, nw_ref[...])


def _out_proj(a_bf, w_bf, x, nw, tm):
    T, K = a_bf.shape
    N = w_bf.shape[1]
    return pl.pallas_call(
        _out_proj_kernel,
        grid=(T // tm,),
        in_specs=[
            pl.BlockSpec((tm, K), lambda i: (i, 0)),
            pl.BlockSpec((K, N), lambda i: (0, 0)),
            pl.BlockSpec((tm, N), lambda i: (i, 0)),
            pl.BlockSpec((1, N), lambda i: (0, 0)),
        ],
        out_specs=pl.BlockSpec((tm, N), lambda i: (i, 0)),
        out_shape=jax.ShapeDtypeStruct((T, N), F32),
        compiler_params=_cparams(("arbitrary",)),
        name="mixer_out_proj",
    )(a_bf, w_bf, x, nw)


def _ffn_kernel(*refs, ns, ls, tps, nj, has_state):
    if has_state:
        (x_ref, nw_ref, wg_ref, wu_ref, cwg_ref, cwu_ref, cbg_ref, cbu_ref, wo_ref, nw2_ref,
         sg_ref, su_ref, o_ref, csg_ref, csu_ref, h_ref, acc_ref, work_ref, carry_ref,
         act_ref) = refs
    else:
        (x_ref, nw_ref, wg_ref, wu_ref, cwg_ref, cwu_ref, cbg_ref, cbu_ref, wo_ref, nw2_ref,
         o_ref, csg_ref, csu_ref, h_ref, acc_ref, work_ref, carry_ref, act_ref) = refs
        sg_ref = su_ref = None
    i = pl.program_id(0)
    j = pl.program_id(1)
    tf = wg_ref.shape[1]

    def conv(half, w_ref, cw_ref, cb_ref, s_ref, cs_ref):
        u3 = _dot(h_ref[...], w_ref[...]).reshape(ns, ls, tf)
        work_ref[half, :, 8:8 + ls, :] = u3
        prev = jnp.zeros((ns, 2, tf), F32) if s_ref is None else s_ref[...]
        if tps > 1:
            prev = jnp.where((i % tps) == 0, prev, carry_ref[j, half])
        work_ref[half, :, 6:8, :] = prev
        x1 = work_ref[half, :, 7:7 + ls, :]
        x2 = work_ref[half, :, 6:6 + ls, :]
        cw = cw_ref[...]
        c = cb_ref[...] + cw[0:1] * x2 + cw[1:2] * x1 + cw[2:3] * u3
        tail = u3[:, ls - 2:ls, :]
        cs_ref[...] = tail
        if tps > 1:
            carry_ref[j, half] = tail
        return c.reshape(ns * ls, tf)

    def up_conv_act():
        cg = conv(0, wg_ref, cwg_ref, cbg_ref, sg_ref, csg_ref)
        cu = conv(1, wu_ref, cwu_ref, cbu_ref, su_ref, csu_ref)
        return (cg * _sigmoid(cg) * cu).astype(BF16)

    @pl.when(j == 0)
    def _():
        h_ref[...] = _rms(x_ref[...], nw_ref[...]).astype(BF16)
        acc_ref[...] = jnp.zeros_like(acc_ref)
        act_ref[0] = up_conv_act()

    for par in range(2):
        @pl.when((j > 0) & (j < nj) & (j % 2 == par))
        def _():
            acc_ref[...] += _dot(act_ref[1 - par], wo_ref[...])
            act_ref[par] = up_conv_act()

    @pl.when(j == nj)
    def _():
        y = acc_ref[...] + _dot(act_ref[(nj - 1) % 2], wo_ref[...])
        o_ref[...] = x_ref[...] + _rms(y, nw2_ref[...])


def _ffn(x, nw, w_in_bf, conv_w, conv_b, w_out_bf, nw2, state, n_streams, L, tm):
    T, D = x.shape
    tf = FF_TILE
    nj = D_FF // tf
    if tm >= L:
        assert tm % L == 0
        ns, ls, tps = tm // L, L, 1
    else:
        assert L % tm == 0
        ns, ls, tps = 1, tm, L // tm
    has_state = state is not None
    cb2 = conv_b.reshape(1, 2 * D_FF)

    def stream_blk(i):
        return (i * tm) // L // ns if ns > 1 else (i * tm) // L

    up = lambda j: jnp.minimum(j, nj - 1)
    dn = lambda j: jnp.maximum(j - 1, 0)
    in_specs = [
        pl.BlockSpec((tm, D), lambda i, j: (i, 0)),
        pl.BlockSpec((1, D), lambda i, j: (0, 0)),
        pl.BlockSpec((D, tf), lambda i, j: (0, up(j))),
        pl.BlockSpec((D, tf), lambda i, j: (0, nj + up(j))),
        pl.BlockSpec((3, tf), lambda i, j: (0, up(j))),
        pl.BlockSpec((3, tf), lambda i, j: (0, nj + up(j))),
        pl.BlockSpec((1, tf), lambda i, j: (0, up(j))),
        pl.BlockSpec((1, tf), lambda i, j: (0, nj + up(j))),
        pl.BlockSpec((tf, D), lambda i, j: (dn(j), 0)),
        pl.BlockSpec((1, D), lambda i, j: (0, 0)),
    ]
    args = [x, nw, w_in_bf, w_in_bf, conv_w, conv_w, cb2, cb2, w_out_bf, nw2]
    if has_state:
        in_specs += [
            pl.BlockSpec((ns, 2, tf), lambda i, j: (stream_blk(i), 0, up(j))),
            pl.BlockSpec((ns, 2, tf), lambda i, j: (stream_blk(i), 0, nj + up(j))),
        ]
        args += [state, state]
    out, csg, csu = pl.pallas_call(
        functools.partial(_ffn_kernel, ns=ns, ls=ls, tps=tps, nj=nj, has_state=has_state),
        grid=(T // tm, nj + 1),
        in_specs=in_specs,
        out_specs=[
            pl.BlockSpec((tm, D), lambda i, j: (i, 0)),
            pl.BlockSpec((ns, 2, tf), lambda i, j: (i, 0, up(j))),
            pl.BlockSpec((ns, 2, tf), lambda i, j: (i, 0, up(j))),
        ],
        out_shape=[
            jax.ShapeDtypeStruct((T, D), F32),
            jax.ShapeDtypeStruct((T // ls, 2, D_FF), F32),
            jax.ShapeDtypeStruct((T // ls, 2, D_FF), F32),
        ],
        scratch_shapes=[
            pltpu.VMEM((tm, D), BF16),
            pltpu.VMEM((tm, D), F32),
            pltpu.VMEM((2, ns, 8 + ls, tf), F32),
            pltpu.VMEM((nj, 2, ns, 2, tf), F32),
            pltpu.VMEM((2, tm, tf), BF16),
        ],
        compiler_params=_cparams(("arbitrary", "arbitrary")),
        name="conv_ffn",
    )(*args)
    tails = jnp.concatenate([csg, csu], axis=-1).reshape(n_streams, tps, 2, 2 * D_FF)
    return out, tails[:, tps - 1]


def _rope_table_kernel(c_ref, s_ref):
    shape = c_ref.shape
    half = ROPE_DIM // 2
    pos = lax.broadcasted_iota(jnp.int32, shape, 0).astype(F32)
    lane = lax.broadcasted_iota(jnp.int32, shape, 1)
    fi = (lane & (half - 1)).astype(F32)
    inv = jnp.exp(fi * (-math.log(ROPE_BASE) / half))
    ang = pos * inv
    valid = lane < ROPE_DIM
    c_ref[...] = jnp.where(valid, jnp.cos(ang), 0.0)
    s_ref[...] = jnp.where(valid, jnp.where(lane < half, -jnp.sin(ang), jnp.sin(ang)), 0.0)


def _rope_tables(n_pos):
    return pl.pallas_call(
        _rope_table_kernel,
        out_shape=[jax.ShapeDtypeStruct((n_pos, LANES), F32)] * 2,
        name="rope_tables",
    )()


def _mla_in_kernel(x_ref, nw_ref, wd_ref, qnw_ref, kvnw_ref, wq_ref, c_ref, s_ref,
                   qcat_ref, ckv_ref, kpe_ref):
    h = _rms(x_ref[...], nw_ref[...]).astype(BF16)
    d = _dot(h, wd_ref[...])
    cq = _rms(d[:, :Q_LORA], qnw_ref[...]).astype(BF16)
    ckv_ref[...] = _rms(d[:, Q_LORA:Q_LORA + KV_LORA], kvnw_ref[...])
    cs = c_ref[...]
    sn = s_ref[...]
    o = Q_LORA + KV_LORA
    kpe_ref[...] = d[:, o:o + LANES] * cs + d[:, o + LANES:o + 2 * LANES] * sn
    hw = MLA_HEADS * LANES
    qn = _dot(cq, wq_ref[:, 0:hw])
    pr = _dot(cq, wq_ref[:, hw:2 * hw])
    ps = _dot(cq, wq_ref[:, 2 * hw:3 * hw])
    for hh in range(MLA_HEADS):
        sl = slice(hh * LANES, (hh + 1) * LANES)
        qcat_ref[hh, :, 0:LANES] = qn[:, sl].astype(BF16)
        qcat_ref[hh, :, LANES:2 * LANES] = (pr[:, sl] * cs + ps[:, sl] * sn).astype(BF16)


def _mla_in(x, nw, wd_bf, qnw, kvnw, wq_bf, cos_t, sin_t, tm, n_tab_blk):
    T, D = x.shape
    return pl.pallas_call(
        _mla_in_kernel,
        grid=(T // tm,),
        in_specs=[
            pl.BlockSpec((tm, D), lambda i: (i, 0)),
            pl.BlockSpec((1, D), lambda i: (0, 0)),
            pl.BlockSpec(wd_bf.shape, lambda i: (0, 0)),
            pl.BlockSpec((1, Q_LORA), lambda i: (0, 0)),
            pl.BlockSpec((1, KV_LORA), lambda i: (0, 0)),
            pl.BlockSpec(wq_bf.shape, lambda i: (0, 0)),
            pl.BlockSpec((tm, LANES), lambda i: (i % n_tab_blk, 0)),
            pl.BlockSpec((tm, LANES), lambda i: (i % n_tab_blk, 0)),
        ],
        out_specs=[
            pl.BlockSpec((MLA_HEADS, tm, QK_PAD), lambda i: (0, i, 0)),
            pl.BlockSpec((tm, KV_LORA), lambda i: (i, 0)),
            pl.BlockSpec((tm, LANES), lambda i: (i, 0)),
        ],
        out_shape=[
            jax.ShapeDtypeStruct((MLA_HEADS, T, QK_PAD), BF16),
            jax.ShapeDtypeStruct((T, KV_LORA), F32),
            jax.ShapeDtypeStruct((T, LANES), F32),
        ],
        compiler_params=_cparams(("arbitrary",)),
        name="mla_in_proj",
    )(x, nw, wd_bf, qnw, kvnw, wq_bf, cos_t, sin_t)


def _kv_up_kernel(ckv_ref, kpe_ref, wk_ref, wv_ref, kcat_ref, v_ref):
    c = ckv_ref[...].astype(BF16)
    kn = _dot(c, wk_ref[...])
    vv = _dot(c, wv_ref[...])
    kp = kpe_ref[...].astype(BF16)
    for hh in range(MLA_HEADS):
        sl = slice(hh * LANES, (hh + 1) * LANES)
        kcat_ref[hh, :, 0:LANES] = kn[:, sl].astype(BF16)
        kcat_ref[hh, :, LANES:2 * LANES] = kp
        v_ref[hh] = vv[:, sl].astype(BF16)


def _kv_up(ckv, kpe_pad, wk_bf, wv_bf, tm):
    R = ckv.shape[0]
    return pl.pallas_call(
        _kv_up_kernel,
        grid=(R // tm,),
        in_specs=[
            pl.BlockSpec((tm, KV_LORA), lambda i: (i, 0)),
            pl.BlockSpec((tm, LANES), lambda i: (i, 0)),
            pl.BlockSpec(wk_bf.shape, lambda i: (0, 0)),
            pl.BlockSpec(wv_bf.shape, lambda i: (0, 0)),
        ],
        out_specs=[
            pl.BlockSpec((MLA_HEADS, tm, QK_PAD), lambda i: (0, i, 0)),
            pl.BlockSpec((MLA_HEADS, tm, V_DIM), lambda i: (0, i, 0)),
        ],
        out_shape=[
            jax.ShapeDtypeStruct((MLA_HEADS, R, QK_PAD), BF16),
            jax.ShapeDtypeStruct((MLA_HEADS, R, V_DIM), BF16),
        ],
        compiler_params=_cparams(("arbitrary",)),
        name="mla_kv_up_proj",
    )(ckv, kpe_pad, wk_bf, wv_bf)


def _attn_prompt_kernel(q_ref, k_ref, v_ref, o_ref, *, L, tq, scale):
    row = lax.broadcasted_iota(jnp.int32, (tq, tq), 0)
    col = lax.broadcasted_iota(jnp.int32, (tq, tq), 1)
    sh = int(math.log2(CHUNK))
    dmask = (row >> sh) >= (col >> sh)

    c2 = scale * math.log2(math.e)
    for qi in range(L // tq):
        n = qi * tq
        q = q_ref[0, n:n + tq, :]
        s_d = jnp.where(dmask, _dot_nt(q, k_ref[0, n:n + tq, :]), -jnp.inf)
        m = jnp.max(s_d, axis=-1, keepdims=True)
        if qi > 0:
            s_p = _dot_nt(q, k_ref[0, 0:n, :])
            m = jnp.maximum(m, jnp.max(s_p, axis=-1, keepdims=True))
        p_d = jnp.exp2((s_d - m) * c2)
        l = jnp.sum(p_d, axis=-1, keepdims=True)
        acc = _dot(p_d.astype(BF16), v_ref[0, n:n + tq, :])
        if qi > 0:
            p_p = jnp.exp2((s_p - m) * c2)
            l = l + jnp.sum(p_p, axis=-1, keepdims=True)
            acc = acc + _dot(p_p.astype(BF16), v_ref[0, 0:n, :])
        o_ref[n:n + tq, :] = (acc / l).astype(BF16)


def _attn_prompt(qcat, kcat, v, B, L):
    H = MLA_HEADS
    scale = (NOPE_DIM + ROPE_DIM) ** -0.5
    return pl.pallas_call(
        functools.partial(_attn_prompt_kernel, L=L, tq=256, scale=scale),
        grid=(B, H),
        in_specs=[
            pl.BlockSpec((1, L, QK_PAD), lambda b, h: (h, b, 0)),
            pl.BlockSpec((1, L, QK_PAD), lambda b, h: (h, b, 0)),
            pl.BlockSpec((1, L, V_DIM), lambda b, h: (h, b, 0)),
        ],
        out_specs=pl.BlockSpec((L, V_DIM), lambda b, h: (b, h)),
        out_shape=jax.ShapeDtypeStruct((B * L, H * V_DIM), BF16),
        compiler_params=_cparams(("arbitrary", "arbitrary")),
        name="attn_prompt",
    )(qcat, kcat, v)


def _attn_sample_kernel(q_ref, kc_ref, vc_ref, kn_ref, vn_ref, o_ref, *, past, scale):
    q = q_ref[0]
    Lq = q.shape[0]
    s1 = _dot_nt(q, kc_ref[0]) * scale
    s2 = _dot_nt(q, kn_ref[0]) * scale
    sh = int(math.log2(CHUNK))
    qc = (lax.broadcasted_iota(jnp.int32, s1.shape, 0) + past) >> sh
    kc1 = lax.broadcasted_iota(jnp.int32, s1.shape, 1) >> sh
    s1 = jnp.where(kc1 <= qc, s1, -jnp.inf)
    qc2 = (lax.broadcasted_iota(jnp.int32, (Lq, Lq), 0) + past) >> sh
    kc2 = (lax.broadcasted_iota(jnp.int32, (Lq, Lq), 1) + past) >> sh
    s2 = jnp.where(kc2 <= qc2, s2, -jnp.inf)
    m = jnp.maximum(jnp.max(s1, axis=-1, keepdims=True), jnp.max(s2, axis=-1, keepdims=True))
    p1 = jnp.exp(s1 - m)
    p2 = jnp.exp(s2 - m)
    l = jnp.sum(p1, axis=-1, keepdims=True) + jnp.sum(p2, axis=-1, keepdims=True)
    acc = _dot(p1.astype(BF16), vc_ref[0]) + _dot(p2.astype(BF16), vn_ref[0])
    o_ref[...] = (acc / l).astype(BF16)


def _attn_sample(qcat, kcat_c, v_c, kcat_n, v_n, B, L, past):
    H = MLA_HEADS
    scale = (NOPE_DIM + ROPE_DIM) ** -0.5
    return pl.pallas_call(
        functools.partial(_attn_sample_kernel, past=past, scale=scale),
        grid=(B, H),
        in_specs=[
            pl.BlockSpec((1, L, QK_PAD), lambda b, h: (h, b, 0)),
            pl.BlockSpec((1, past, QK_PAD), lambda b, h: (h, b, 0)),
            pl.BlockSpec((1, past, V_DIM), lambda b, h: (h, b, 0)),
            pl.BlockSpec((1, L, QK_PAD), lambda b, h: (h, b, 0)),
            pl.BlockSpec((1, L, V_DIM), lambda b, h: (h, b, 0)),
        ],
        out_specs=pl.BlockSpec((L, V_DIM), lambda b, h: (b, h)),
        out_shape=jax.ShapeDtypeStruct((B * L, H * V_DIM), BF16),
        compiler_params=_cparams(("arbitrary", "arbitrary")),
        name="attn_sample",
    )(qcat, kcat_c, v_c, kcat_n, v_n)


def _swap_halves(w):
    half = w.shape[-1] // 2
    return jnp.concatenate([w[..., half:], w[..., :half]], axis=-1)


def _pad_lanes(w):
    return jnp.pad(w, [(0, 0)] * (w.ndim - 1) + [(0, LANES - w.shape[-1])])


def _prep_mla_weights(w_down, w_uq, w_ukv):
    o = Q_LORA + KV_LORA
    wpe = w_down[:, o:]
    wd = jnp.concatenate([w_down[:, :o], _pad_lanes(wpe), _pad_lanes(_swap_halves(wpe))], axis=-1)
    wq3 = w_uq.reshape(Q_LORA, MLA_HEADS, NOPE_DIM + ROPE_DIM)
    wn = wq3[..., :NOPE_DIM].reshape(Q_LORA, -1)
    wr = wq3[..., NOPE_DIM:]
    wrp = _pad_lanes(wr).reshape(Q_LORA, -1)
    wrs = _pad_lanes(_swap_halves(wr)).reshape(Q_LORA, -1)
    wq = jnp.concatenate([wn, wrp, wrs], axis=-1)
    wkv4 = w_ukv.reshape(KV_LORA, MLA_HEADS, 2, NOPE_DIM)
    wk = wkv4[:, :, 0].reshape(KV_LORA, -1)
    wv = wkv4[:, :, 1].reshape(KV_LORA, -1)
    return wd.astype(BF16), wq.astype(BF16), wk.astype(BF16), wv.astype(BF16)


def _trunk(x, B, L, tm, hg_state, ckv_cache, kpe_cache, conv_state, W, cos_t, sin_t, n_tab_blk):
    norm_w = W["norm_w"]
    nrow = lambda l, k: norm_w[l, k].reshape(1, -1)
    past = 0 if ckv_cache is None else ckv_cache.shape[1]

    p = _hgrn_in(x, nrow(0, 0), W["hgrn_w_in"], W["lb_logits"], 0, tm)
    z, hg_new = _hgrn_rec(p, W["hgrn_gnorm_w"].reshape(1, -1), hg_state, B, L)
    x = _out_proj(z, W["hgrn_w_out"], x, nrow(0, 1), tm)
    x, conv0 = _ffn(x, nrow(0, 2), W["ffn_w_in"][0], W["ffn_conv_w"][0], W["ffn_conv_b"][0],
                    W["ffn_w_out"][0], nrow(0, 3), None if conv_state is None else conv_state[0],
                    B, L, tm)

    qcat, ckv, kpe_pad = _mla_in(x, nrow(1, 0), W["mla_wd"], W["mla_q_norm_w"].reshape(1, -1),
                                 W["mla_kv_norm_w"].reshape(1, -1), W["mla_wq"], cos_t, sin_t,
                                 min(tm, 256), n_tab_blk)
    kcat_n, v_n = _kv_up(ckv, kpe_pad, W["mla_wk"], W["mla_wv"], min(tm, 256))
    if ckv_cache is None:
        o = _attn_prompt(qcat, kcat_n, v_n, B, L)
    else:
        cc = ckv_cache.reshape(B * past, KV_LORA)
        kc = _pad_lanes(kpe_cache.reshape(B * past, ROPE_DIM))
        kcat_c, v_c = _kv_up(cc, kc, W["mla_wk"], W["mla_wv"], 512)
        o = _attn_sample(qcat, kcat_c, v_c, kcat_n, v_n, B, L, past)
    x = _out_proj(o, W["mla_w_out"], x, nrow(1, 1), tm)
    x, conv1 = _ffn(x, nrow(1, 2), W["ffn_w_in"][1], W["ffn_conv_w"][1], W["ffn_conv_b"][1],
                    W["ffn_w_out"][1], nrow(1, 3), None if conv_state is None else conv_state[1],
                    B, L, tm)

    D = x.shape[-1]
    return (x.reshape(B, L, D), hg_new[None], ckv.reshape(1, B, L, KV_LORA),
            kpe_pad[:, :ROPE_DIM].reshape(1, B, L, ROPE_DIM), jnp.stack([conv0, conv1]))


def kernel(x_prompt, x_sample, state_hgrn, cache_ckv, cache_kpe, state_conv, norm_w, lb_logits,
           hgrn_w_in, hgrn_gnorm_w, hgrn_w_out, mla_w_down, mla_q_norm_w, mla_kv_norm_w,
           mla_w_uq, mla_w_ukv, mla_w_out, ffn_w_in, ffn_conv_w, ffn_conv_b, ffn_w_out):
    wd, wq, wk, wv = _prep_mla_weights(mla_w_down[0], mla_w_uq[0], mla_w_ukv[0])
    W = dict(
        norm_w=norm_w, lb_logits=lb_logits,
        hgrn_w_in=hgrn_w_in[0].astype(BF16), hgrn_gnorm_w=hgrn_gnorm_w[0],
        hgrn_w_out=hgrn_w_out[0].astype(BF16),
        mla_wd=wd, mla_wq=wq, mla_wk=wk, mla_wv=wv,
        mla_q_norm_w=mla_q_norm_w[0], mla_kv_norm_w=mla_kv_norm_w[0],
        mla_w_out=mla_w_out[0].astype(BF16),
        ffn_w_in=[w.astype(BF16) for w in ffn_w_in], ffn_conv_w=ffn_conv_w,
        ffn_conv_b=ffn_conv_b, ffn_w_out=[w.astype(BF16) for w in ffn_w_out],
    )
    Bp, Lp, D = x_prompt.shape
    Bs, Ls, _ = x_sample.shape
    past = cache_ckv.shape[2]
    cos_t, sin_t = _rope_tables(max(Lp, past + Ls))

    yp, hgp, ckvp, kpep, convp = _trunk(
        x_prompt.reshape(Bp * Lp, D), Bp, Lp, 512, None, None, None, None, W,
        cos_t, sin_t, Lp // 256)

    cos_s = jnp.tile(cos_t[past:past + Ls], (Bs, 1))
    sin_s = jnp.tile(sin_t[past:past + Ls], (Bs, 1))
    ys, hgs, ckvs, kpes, convs = _trunk(
        x_sample.reshape(Bs * Ls, D), Bs, Ls, Bs * Ls, state_hgrn[0], cache_ckv[0], cache_kpe[0],
        state_conv, W, cos_s, sin_s, 1)
    return (yp, ys, hgp, hgs, ckvp, ckvs, kpep, kpes, convp, convs)
```

```python
import functools
import math

import jax
import jax.numpy as jnp
from jax import lax
from jax.experimental import pallas as pl
from jax.experimental.pallas import tpu as pltpu

F32 = jnp.float32
BF16 = jnp.bfloat16

EPS = 1e-6
LANES = 128
CHUNK = 64
HG_HEADS = 16
HG_DK = 128
HG_DV = 128
MLA_HEADS = 16
Q_LORA = 512
KV_LORA = 512
NOPE_DIM = 128
ROPE_DIM = 64
V_DIM = 128
ROPE_BASE = 10000.0
D_FF = 5632
FF_TILE = 512
QK_PAD = 256
VMEM_LIMIT = 56 * 1024 * 1024


def _cparams(sem):
    return pltpu.CompilerParams(dimension_semantics=sem, vmem_limit_bytes=VMEM_LIMIT)


def _rms(x, w):
    ms = jnp.mean(x * x, axis=-1, keepdims=True)
    return x * lax.rsqrt(ms + EPS) * w


def _sigmoid(x):
    return 1.0 / (1.0 + jnp.exp(-x))


def _dot(a, b):
    return jnp.dot(a, b, preferred_element_type=F32)


def _dot_nt(a, b):
    return lax.dot_general(a, b, (((1,), (1,)), ((), ())), preferred_element_type=F32)


def _dot_tn(a, b):
    return lax.dot_general(a, b, (((0,), (0,)), ((), ())), preferred_element_type=F32)


def _hgrn_in_kernel(x_ref, nw_ref, w_ref, lb_ref, o_ref, h_ref, *, lb_row, blk_per_sec):
    j = pl.program_id(1)

    @pl.when(j == 0)
    def _():
        h_ref[...] = _rms(x_ref[...], nw_ref[...]).astype(BF16)

    r = _dot(h_ref[...], w_ref[...])
    sec = j // blk_per_sec
    heads = o_ref.shape[0]

    def store(val):
        for hh in range(heads):
            o_ref[hh] = val[:, hh * LANES:(hh + 1) * LANES]

    @pl.when((sec == 0) | (sec == 3))
    def _():
        store(r * _sigmoid(r))

    @pl.when(sec == 1)
    def _():
        lg = lb_ref[...]
        e = jnp.exp(lg - jnp.max(lg, axis=0, keepdims=True))
        sm = e / jnp.sum(e, axis=0, keepdims=True)
        lb = jnp.sum(sm[0:lb_row + 1], axis=0, keepdims=True)
        store(lb + (1.0 - lb) * _sigmoid(r))

    @pl.when(sec == 2)
    def _():
        store(r)


def _hgrn_in(x, nw, w_bf, lb_logits, lb_row, tm):
    T, D = x.shape
    nblk, _, tn = w_bf.shape
    N = nblk * tn
    sec_w = N // 4
    blk_per_sec = sec_w // tn
    heads = tn // LANES
    return pl.pallas_call(
        functools.partial(_hgrn_in_kernel, lb_row=lb_row, blk_per_sec=blk_per_sec),
        grid=(T // tm, N // tn),
        in_specs=[
            pl.BlockSpec((tm, D), lambda i, j: (i, 0)),
            pl.BlockSpec((1, D), lambda i, j: (0, 0)),
            pl.BlockSpec((None, D, tn), lambda i, j: (j, 0, 0)),
            pl.BlockSpec((lb_logits.shape[0], tn), lambda i, j: (0, j % blk_per_sec)),
        ],
        out_specs=pl.BlockSpec((heads, tm, LANES), lambda i, j: (j, i, 0)),
        out_shape=jax.ShapeDtypeStruct((N // LANES, T, LANES), F32),
        scratch_shapes=[pltpu.VMEM((tm, D), BF16)],
        compiler_params=_cparams(("arbitrary", "arbitrary")),
        name="hgrn_in_proj",
    )(x, nw, w_bf, lb_logits)


def _bcast_mid(c, s):
    C = c.shape[0]
    if s >= 4:
        blk = 2 * s
        c3 = c.reshape(C // blk, blk, LANES)
        return jnp.broadcast_to(c3[:, s - 1:s, :], c3.shape).reshape(C, LANES)
    t = lax.broadcasted_iota(jnp.int32, c.shape, 0)
    if s == 2:
        ph = t & 3
        up1 = pltpu.roll(c, C - 1, 0)
        dn1 = pltpu.roll(c, 1, 0)
        dn2 = pltpu.roll(c, 2, 0)
        return jnp.where(ph == 0, up1, jnp.where(ph == 1, c, jnp.where(ph == 2, dn1, dn2)))
    dn1 = pltpu.roll(c, 1, 0)
    return jnp.where((t & 1) == 1, dn1, c)


def _gla_masks(C):
    t = lax.broadcasted_iota(jnp.int32, (C, LANES), 0)
    rr = lax.broadcasted_iota(jnp.int32, (C, C), 0)
    cc = lax.broadcasted_iota(jnp.int32, (C, C), 1)
    masks = []
    s = 1
    while s < C:
        sh = int(math.log2(2 * s))
        pair = ((rr >> sh) == (cc >> sh)) & ((rr & s) != 0) & ((cc & s) == 0)
        masks.append(((t & s) != 0, pair))
        s *= 2
    return masks


def _gla_head(q, fg, v, masks):
    C = q.shape[0]
    k = 1.0 - fg
    c = jnp.log2(fg)
    a = jnp.zeros((C, C), F32)
    s = 1
    for upper, pair in masks:
        bc = _bcast_mid(c, s)
        qd = q * jnp.exp2(c)
        kd = k * jnp.exp2(jnp.minimum(bc - c, 0.0))
        a = jnp.where(pair, _dot_nt(qd.astype(BF16), kd.astype(BF16)), a)
        c = c + jnp.where(upper, bc, 0.0)
        s *= 2
    b = c
    b_last = b[C - 1:C, :]
    diag = jnp.sum(q * k, axis=-1, keepdims=True)
    qb = (q * jnp.exp2(b)).astype(BF16)
    kdec = (k * jnp.exp2(b_last - b)).astype(BF16)
    return a.astype(BF16), qb, kdec, diag * v, jnp.exp2(b_last)


def _gla_tail(a, qb, kdec, dv, e_last, v_bf, st):
    o = _dot(a, v_bf) + dv + _dot_nt(qb, st.astype(BF16))
    st_new = st * e_last + _dot_tn(v_bf, kdec)
    return o, st_new


def _hgrn_rec_kernel(*refs, C, n_chunks, has_state):
    if has_state:
        (q_ref, f_ref, v_ref, g_ref, gw_ref, s0_ref, z_ref, so_ref,
         st_ref, a_buf, qb_buf, kd_buf, dv_buf, el_buf) = refs
    else:
        (q_ref, f_ref, v_ref, g_ref, gw_ref, z_ref, so_ref,
         st_ref, a_buf, qb_buf, kd_buf, dv_buf, el_buf) = refs
    li = pl.program_id(2)
    hb = st_ref.shape[0]

    @pl.when(li == 0)
    def _():
        for hh in range(hb):
            st_ref[hh] = s0_ref[0, hh].T if has_state else jnp.zeros((HG_DV, HG_DK), F32)

    gw = gw_ref[...]
    masks = _gla_masks(C)

    def rows_of(ci):
        return pl.ds(pl.multiple_of(ci * C, C), C)

    def head(ci, slot):
        rows = rows_of(ci)
        for hh in range(hb):
            a, qb, kdec, dv, e_last = _gla_head(q_ref[hh, rows, :], f_ref[hh, rows, :],
                                                v_ref[hh, rows, :], masks)
            a_buf[slot, hh] = a
            qb_buf[slot, hh] = qb
            kd_buf[slot, hh] = kdec
            dv_buf[slot, hh] = dv
            el_buf[slot, hh] = e_last

    def tail(ci, slot):
        rows = rows_of(ci)
        for hh in range(hb):
            o, st_new = _gla_tail(a_buf[slot, hh], qb_buf[slot, hh], kd_buf[slot, hh],
                                  dv_buf[slot, hh], el_buf[slot, hh],
                                  v_ref[hh, rows, :].astype(BF16), st_ref[hh])
            st_ref[hh] = st_new
            z_ref[rows, hh * LANES:(hh + 1) * LANES] = (
                _rms(o, gw) * g_ref[hh, rows, :]).astype(BF16)

    head(0, 0)
    if n_chunks > 1:
        assert n_chunks % 2 == 0

        def body(kk, carry):
            c0 = 2 * kk
            tail(c0, 0)
            head(c0 + 1, 1)
            tail(c0 + 1, 1)
            head(c0 + 2, 0)
            return carry

        lax.fori_loop(0, n_chunks // 2 - 1, body, 0)
        tail(n_chunks - 2, 0)
        head(n_chunks - 1, 1)
        tail(n_chunks - 1, 1)
    else:
        tail(0, 0)

    @pl.when(li == pl.num_programs(2) - 1)
    def _():
        for hh in range(hb):
            so_ref[0, hh] = st_ref[hh].T


def _hgrn_rec(p, gnorm_w, s0, B, L):
    H = HG_HEADS
    T = B * L
    C = min(CHUNK, L)
    hb = 4
    lb = min(L, 1024)
    nl = L // lb
    ng = H // hb
    has_state = s0 is not None

    def sec_spec(sec):
        return pl.BlockSpec((hb, lb, LANES), lambda b, g, l: (sec * ng + g, b * nl + l, 0))

    in_specs = [sec_spec(0), sec_spec(1), sec_spec(2), sec_spec(3),
                pl.BlockSpec((1, HG_DV), lambda b, g, l: (0, 0))]
    args = [p, p, p, p, gnorm_w]
    if has_state:
        in_specs.append(pl.BlockSpec((1, hb, HG_DK, HG_DV), lambda b, g, l: (b, g, 0, 0)))
        args.append(s0)
    return pl.pallas_call(
        functools.partial(_hgrn_rec_kernel, C=C, n_chunks=lb // C, has_state=has_state),
        grid=(B, ng, nl),
        in_specs=in_specs,
        out_specs=[
            pl.BlockSpec((lb, hb * LANES), lambda b, g, l: (b * nl + l, g)),
            pl.BlockSpec((1, hb, HG_DK, HG_DV), lambda b, g, l: (b, g, 0, 0)),
        ],
        out_shape=[
            jax.ShapeDtypeStruct((T, H * HG_DV), BF16),
            jax.ShapeDtypeStruct((B, H, HG_DK, HG_DV), F32),
        ],
        scratch_shapes=[
            pltpu.VMEM((hb, HG_DV, HG_DK), F32),
            pltpu.VMEM((2, hb, C, C), BF16),
            pltpu.VMEM((2, hb, C, HG_DK), BF16),
            pltpu.VMEM((2, hb, C, HG_DK), BF16),
            pltpu.VMEM((2, hb, C, HG_DV), F32),
            pltpu.VMEM((2, hb, 1, HG_DK), F32),
        ],
        compiler_params=_cparams(("arbitrary", "arbitrary", "arbitrary")),
        name="hgrn_recurrence",
    )(*args)


def _out_proj_kernel(a_ref, w_ref, x_ref, nw_ref, o_ref):
    y = _dot(a_ref[...], w_ref[...])
    o_ref[...] = x_ref[...] + _rms(y, nw_ref[...])


def _out_proj(a_bf, w_bf, x, nw, tm):
    T, K = a_bf.shape
    N = w_bf.shape[1]
    return pl.pallas_call(
        _out_proj_kernel,
        grid=(T // tm,),
        in_specs=[
            pl.BlockSpec((tm, K), lambda i: (i, 0)),
            pl.BlockSpec((K, N), lambda i: (0, 0)),
            pl.BlockSpec((tm, N), lambda i: (i, 0)),
            pl.BlockSpec((1, N), lambda i: (0, 0)),
        ],
        out_specs=pl.BlockSpec((tm, N), lambda i: (i, 0)),
        out_shape=jax.ShapeDtypeStruct((T, N), F32),
        compiler_params=_cparams(("arbitrary",)),
        name="mixer_out_proj",
    )(a_bf, w_bf, x, nw)


def _ffn_kernel(*refs, ns, ls, tps, nj, has_state):
    if has_state:
        (x_ref, nw_ref, wg_ref, wu_ref, cwg_ref, cwu_ref, cbg_ref, cbu_ref, wo_ref, nw2_ref,
         sg_ref, su_ref, o_ref, csg_ref, csu_ref, h_ref, acc_ref, work_ref, carry_ref,
         act_ref) = refs
    else:
        (x_ref, nw_ref, wg_ref, wu_ref, cwg_ref, cwu_ref, cbg_ref, cbu_ref, wo_ref, nw2_ref,
         o_ref, csg_ref, csu_ref, h_ref, acc_ref, work_ref, carry_ref, act_ref) = refs
        sg_ref = su_ref = None
    i = pl.program_id(0)
    j = pl.program_id(1)
    tf = wg_ref.shape[1]

    def conv(half, w_ref, cw_ref, cb_ref, s_ref, cs_ref):
        u3 = _dot(h_ref[...], w_ref[...]).reshape(ns, ls, tf)
        work_ref[half, :, 8:8 + ls, :] = u3
        prev = jnp.zeros((ns, 2, tf), F32) if s_ref is None else s_ref[...]
        if tps > 1:
            prev = jnp.where((i % tps) == 0, prev, carry_ref[j, half])
        work_ref[half, :, 6:8, :] = prev
        x1 = work_ref[half, :, 7:7 + ls, :]
        x2 = work_ref[half, :, 6:6 + ls, :]
        cw = cw_ref[...]
        c = cb_ref[...] + cw[0:1] * x2 + cw[1:2] * x1 + cw[2:3] * u3
        tail = u3[:, ls - 2:ls, :]
        cs_ref[...] = tail
        if tps > 1:
            carry_ref[j, half] = tail
        return c.reshape(ns * ls, tf)

    def up_conv_act():
        cg = conv(0, wg_ref, cwg_ref, cbg_ref, sg_ref, csg_ref)
        cu = conv(1, wu_ref, cwu_ref, cbu_ref, su_ref, csu_ref)
        return (cg * _sigmoid(cg) * cu).astype(BF16)

    @pl.when(j == 0)
    def _():
        h_ref[...] = _rms(x_ref[...], nw_ref[...]).astype(BF16)
        acc_ref[...] = jnp.zeros_like(acc_ref)
        act_ref[0] = up_conv_act()

    for par in range(2):
        @pl.when((j > 0) & (j < nj) & (j % 2 == par))
        def _():
            acc_ref[...] += _dot(act_ref[1 - par], wo_ref[...])
            act_ref[par] = up_conv_act()

    @pl.when(j == nj)
    def _():
        y = acc_ref[...] + _dot(act_ref[(nj - 1) % 2], wo_ref[...])
        o_ref[...] = x_ref[...] + _rms(y, nw2_ref[...])


def _ffn(x, nw, w_in_bf, conv_w, conv_b, w_out_bf, nw2, state, n_streams, L, tm, layer):
    T, D = x.shape
    tf = w_in_bf.shape[-1]
    nj = D_FF // tf
    if tm >= L:
        assert tm % L == 0
        ns, ls, tps = tm // L, L, 1
    else:
        assert L % tm == 0
        ns, ls, tps = 1, tm, L // tm
    has_state = state is not None
    cb2 = conv_b.reshape(1, 2 * D_FF)

    def stream_blk(i):
        return (i * tm) // L // ns if ns > 1 else (i * tm) // L

    up = lambda j: jnp.minimum(j, nj - 1)
    dn = lambda j: jnp.maximum(j - 1, 0)
    in_specs = [
        pl.BlockSpec((tm, D), lambda i, j: (i, 0)),
        pl.BlockSpec((1, D), lambda i, j: (0, 0)),
        pl.BlockSpec((None, None, D, tf), lambda i, j: (layer, up(j), 0, 0)),
        pl.BlockSpec((None, None, D, tf), lambda i, j: (layer, nj + up(j), 0, 0)),
        pl.BlockSpec((3, tf), lambda i, j: (0, up(j))),
        pl.BlockSpec((3, tf), lambda i, j: (0, nj + up(j))),
        pl.BlockSpec((1, tf), lambda i, j: (0, up(j))),
        pl.BlockSpec((1, tf), lambda i, j: (0, nj + up(j))),
        pl.BlockSpec((None, tf, D), lambda i, j: (layer, dn(j), 0)),
        pl.BlockSpec((1, D), lambda i, j: (0, 0)),
    ]
    args = [x, nw, w_in_bf, w_in_bf, conv_w, conv_w, cb2, cb2, w_out_bf, nw2]
    if has_state:
        in_specs += [
            pl.BlockSpec((ns, 2, tf), lambda i, j: (stream_blk(i), 0, up(j))),
            pl.BlockSpec((ns, 2, tf), lambda i, j: (stream_blk(i), 0, nj + up(j))),
        ]
        args += [state, state]
    out, csg, csu = pl.pallas_call(
        functools.partial(_ffn_kernel, ns=ns, ls=ls, tps=tps, nj=nj, has_state=has_state),
        grid=(T // tm, nj + 1),
        in_specs=in_specs,
        out_specs=[
            pl.BlockSpec((tm, D), lambda i, j: (i, 0)),
            pl.BlockSpec((ns, 2, tf), lambda i, j: (i, 0, up(j))),
            pl.BlockSpec((ns, 2, tf), lambda i, j: (i, 0, up(j))),
        ],
        out_shape=[
            jax.ShapeDtypeStruct((T, D), F32),
            jax.ShapeDtypeStruct((T // ls, 2, D_FF), F32),
            jax.ShapeDtypeStruct((T // ls, 2, D_FF), F32),
        ],
        scratch_shapes=[
            pltpu.VMEM((tm, D), BF16),
            pltpu.VMEM((tm, D), F32),
            pltpu.VMEM((2, ns, 8 + ls, tf), F32),
            pltpu.VMEM((nj, 2, ns, 2, tf), F32),
            pltpu.VMEM((2, tm, tf), BF16),
        ],
        compiler_params=_cparams(("arbitrary", "arbitrary")),
        name="conv_ffn",
    )(*args)
    tails = jnp.concatenate([csg, csu], axis=-1).reshape(n_streams, tps, 2, 2 * D_FF)
    return out, tails[:, tps - 1]


def _rope_table_kernel(c_ref, s_ref):
    shape = c_ref.shape
    half = ROPE_DIM // 2
    pos = lax.broadcasted_iota(jnp.int32, shape, 0).astype(F32)
    lane = lax.broadcasted_iota(jnp.int32, shape, 1)
    fi = (lane & (half - 1)).astype(F32)
    inv = jnp.exp(fi * (-math.log(ROPE_BASE) / half))
    ang = pos * inv
    valid = lane < ROPE_DIM
    c_ref[...] = jnp.where(valid, jnp.cos(ang), 0.0)
    s_ref[...] = jnp.where(valid, jnp.where(lane < half, -jnp.sin(ang), jnp.sin(ang)), 0.0)


def _rope_tables(n_pos):
    return pl.pallas_call(
        _rope_table_kernel,
        out_shape=[jax.ShapeDtypeStruct((n_pos, LANES), F32)] * 2,
        name="rope_tables",
    )()


def _mla_in_kernel(x_ref, nw_ref, wd_ref, qnw_ref, kvnw_ref, wq_ref, c_ref, s_ref,
                   qcat_ref, ckv_ref, kpe_ref):
    h = _rms(x_ref[...], nw_ref[...]).astype(BF16)
    d = _dot(h, wd_ref[...])
    cq = _rms(d[:, :Q_LORA], qnw_ref[...]).astype(BF16)
    ckv_ref[...] = _rms(d[:, Q_LORA:Q_LORA + KV_LORA], kvnw_ref[...])
    cs = c_ref[...]
    sn = s_ref[...]
    o = Q_LORA + KV_LORA
    kpe_ref[...] = d[:, o:o + LANES] * cs + d[:, o + LANES:o + 2 * LANES] * sn
    hw = MLA_HEADS * LANES
    qn = _dot(cq, wq_ref[:, 0:hw])
    pr = _dot(cq, wq_ref[:, hw:2 * hw])
    ps = _dot(cq, wq_ref[:, 2 * hw:3 * hw])
    for hh in range(MLA_HEADS):
        sl = slice(hh * LANES, (hh + 1) * LANES)
        qcat_ref[hh, :, 0:LANES] = qn[:, sl].astype(BF16)
        qcat_ref[hh, :, LANES:2 * LANES] = (pr[:, sl] * cs + ps[:, sl] * sn).astype(BF16)


def _mla_in(x, nw, wd_bf, qnw, kvnw, wq_bf, cos_t, sin_t, tm, n_tab_blk):
    T, D = x.shape
    return pl.pallas_call(
        _mla_in_kernel,
        grid=(T // tm,),
        in_specs=[
            pl.BlockSpec((tm, D), lambda i: (i, 0)),
            pl.BlockSpec((1, D), lambda i: (0, 0)),
            pl.BlockSpec(wd_bf.shape, lambda i: (0, 0)),
            pl.BlockSpec((1, Q_LORA), lambda i: (0, 0)),
            pl.BlockSpec((1, KV_LORA), lambda i: (0, 0)),
            pl.BlockSpec(wq_bf.shape, lambda i: (0, 0)),
            pl.BlockSpec((tm, LANES), lambda i: (i % n_tab_blk, 0)),
            pl.BlockSpec((tm, LANES), lambda i: (i % n_tab_blk, 0)),
        ],
        out_specs=[
            pl.BlockSpec((MLA_HEADS, tm, QK_PAD), lambda i: (0, i, 0)),
            pl.BlockSpec((tm, KV_LORA), lambda i: (i, 0)),
            pl.BlockSpec((tm, LANES), lambda i: (i, 0)),
        ],
        out_shape=[
            jax.ShapeDtypeStruct((MLA_HEADS, T, QK_PAD), BF16),
            jax.ShapeDtypeStruct((T, KV_LORA), F32),
            jax.ShapeDtypeStruct((T, LANES), F32),
        ],
        compiler_params=_cparams(("arbitrary",)),
        name="mla_in_proj",
    )(x, nw, wd_bf, qnw, kvnw, wq_bf, cos_t, sin_t)


def _kv_up_kernel(ckv_ref, kpe_ref, wk_ref, wv_ref, kcat_ref, v_ref):
    c = ckv_ref[...].astype(BF16)
    kn = _dot(c, wk_ref[...])
    vv = _dot(c, wv_ref[...])
    kp = kpe_ref[...].astype(BF16)
    for hh in range(MLA_HEADS):
        sl = slice(hh * LANES, (hh + 1) * LANES)
        kcat_ref[hh, :, 0:LANES] = kn[:, sl].astype(BF16)
        kcat_ref[hh, :, LANES:2 * LANES] = kp
        v_ref[hh] = vv[:, sl].astype(BF16)


def _kv_up(ckv, kpe_pad, wk_bf, wv_bf, tm):
    R = ckv.shape[0]
    return pl.pallas_call(
        _kv_up_kernel,
        grid=(R // tm,),
        in_specs=[
            pl.BlockSpec((tm, KV_LORA), lambda i: (i, 0)),
            pl.BlockSpec((tm, LANES), lambda i: (i, 0)),
            pl.BlockSpec(wk_bf.shape, lambda i: (0, 0)),
            pl.BlockSpec(wv_bf.shape, lambda i: (0, 0)),
        ],
        out_specs=[
            pl.BlockSpec((MLA_HEADS, tm, QK_PAD), lambda i: (0, i, 0)),
            pl.BlockSpec((MLA_HEADS, tm, V_DIM), lambda i: (0, i, 0)),
        ],
        out_shape=[
            jax.ShapeDtypeStruct((MLA_HEADS, R, QK_PAD), BF16),
            jax.ShapeDtypeStruct((MLA_HEADS, R, V_DIM), BF16),
        ],
        compiler_params=_cparams(("arbitrary",)),
        name="mla_kv_up_proj",
    )(ckv, kpe_pad, wk_bf, wv_bf)


def _attn_prompt_kernel(q_ref, k_ref, v_ref, o_ref, *, L, tq, scale):
    row = lax.broadcasted_iota(jnp.int32, (tq, tq), 0)
    col = lax.broadcasted_iota(jnp.int32, (tq, tq), 1)
    sh = int(math.log2(CHUNK))
    dmask = (row >> sh) >= (col >> sh)

    c2 = scale * math.log2(math.e)
    for qi in range(L // tq):
        n = qi * tq
        q = q_ref[0, n:n + tq, :]
        s_d = jnp.where(dmask, _dot_nt(q, k_ref[0, n:n + tq, :]), -jnp.inf)
        m = jnp.max(s_d, axis=-1, keepdims=True)
        if qi > 0:
            s_p = _dot_nt(q, k_ref[0, 0:n, :])
            m = jnp.maximum(m, jnp.max(s_p, axis=-1, keepdims=True))
        p_d = jnp.exp2((s_d - m) * c2)
        l = jnp.sum(p_d, axis=-1, keepdims=True)
        acc = _dot(p_d.astype(BF16), v_ref[0, n:n + tq, :])
        if qi > 0:
            p_p = jnp.exp2((s_p - m) * c2)
            l = l + jnp.sum(p_p, axis=-1, keepdims=True)
            acc = acc + _dot(p_p.astype(BF16), v_ref[0, 0:n, :])
        o_ref[n:n + tq, :] = (acc / l).astype(BF16)


def _attn_prompt(qcat, kcat, v, B, L):
    H = MLA_HEADS
    scale = (NOPE_DIM + ROPE_DIM) ** -0.5
    return pl.pallas_call(
        functools.partial(_attn_prompt_kernel, L=L, tq=256, scale=scale),
        grid=(B, H),
        in_specs=[
            pl.BlockSpec((1, L, QK_PAD), lambda b, h: (h, b, 0)),
            pl.BlockSpec((1, L, QK_PAD), lambda b, h: (h, b, 0)),
            pl.BlockSpec((1, L, V_DIM), lambda b, h: (h, b, 0)),
        ],
        out_specs=pl.BlockSpec((L, V_DIM), lambda b, h: (b, h)),
        out_shape=jax.ShapeDtypeStruct((B * L, H * V_DIM), BF16),
        compiler_params=_cparams(("arbitrary", "arbitrary")),
        name="attn_prompt",
    )(qcat, kcat, v)


def _attn_sample_kernel(q_ref, kc_ref, vc_ref, kn_ref, vn_ref, o_ref, *, past, scale):
    q = q_ref[0]
    Lq = q.shape[0]
    s1 = _dot_nt(q, kc_ref[0]) * scale
    s2 = _dot_nt(q, kn_ref[0]) * scale
    sh = int(math.log2(CHUNK))
    qc = (lax.broadcasted_iota(jnp.int32, s1.shape, 0) + past) >> sh
    kc1 = lax.broadcasted_iota(jnp.int32, s1.shape, 1) >> sh
    s1 = jnp.where(kc1 <= qc, s1, -jnp.inf)
    qc2 = (lax.broadcasted_iota(jnp.int32, (Lq, Lq), 0) + past) >> sh
    kc2 = (lax.broadcasted_iota(jnp.int32, (Lq, Lq), 1) + past) >> sh
    s2 = jnp.where(kc2 <= qc2, s2, -jnp.inf)
    m = jnp.maximum(jnp.max(s1, axis=-1, keepdims=True), jnp.max(s2, axis=-1, keepdims=True))
    p1 = jnp.exp(s1 - m)
    p2 = jnp.exp(s2 - m)
    l = jnp.sum(p1, axis=-1, keepdims=True) + jnp.sum(p2, axis=-1, keepdims=True)
    acc = _dot(p1.astype(BF16), vc_ref[0]) + _dot(p2.astype(BF16), vn_ref[0])
    o_ref[...] = (acc / l).astype(BF16)


def _attn_sample(qcat, kcat_c, v_c, kcat_n, v_n, B, L, past):
    H = MLA_HEADS
    scale = (NOPE_DIM + ROPE_DIM) ** -0.5
    return pl.pallas_call(
        functools.partial(_attn_sample_kernel, past=past, scale=scale),
        grid=(B, H),
        in_specs=[
            pl.BlockSpec((1, L, QK_PAD), lambda b, h: (h, b, 0)),
            pl.BlockSpec((1, past, QK_PAD), lambda b, h: (h, b, 0)),
            pl.BlockSpec((1, past, V_DIM), lambda b, h: (h, b, 0)),
            pl.BlockSpec((1, L, QK_PAD), lambda b, h: (h, b, 0)),
            pl.BlockSpec((1, L, V_DIM), lambda b, h: (h, b, 0)),
        ],
        out_specs=pl.BlockSpec((L, V_DIM), lambda b, h: (b, h)),
        out_shape=jax.ShapeDtypeStruct((B * L, H * V_DIM), BF16),
        compiler_params=_cparams(("arbitrary", "arbitrary")),
        name="attn_sample",
    )(qcat, kcat_c, v_c, kcat_n, v_n)


def _col_blocks(w, tn):
    *lead, K, N = w.shape
    w = w.reshape(*lead, K, N // tn, tn)
    return jnp.swapaxes(w, -3, -2)


def _swap_halves(w):
    half = w.shape[-1] // 2
    return jnp.concatenate([w[..., half:], w[..., :half]], axis=-1)


def _pad_lanes(w):
    return jnp.pad(w, [(0, 0)] * (w.ndim - 1) + [(0, LANES - w.shape[-1])])


def _prep_mla_weights(w_down, w_uq, w_ukv):
    o = Q_LORA + KV_LORA
    wpe = w_down[:, o:]
    wd = jnp.concatenate([w_down[:, :o], _pad_lanes(wpe), _pad_lanes(_swap_halves(wpe))], axis=-1)
    wq3 = w_uq.reshape(Q_LORA, MLA_HEADS, NOPE_DIM + ROPE_DIM)
    wn = wq3[..., :NOPE_DIM].reshape(Q_LORA, -1)
    wr = wq3[..., NOPE_DIM:]
    wrp = _pad_lanes(wr).reshape(Q_LORA, -1)
    wrs = _pad_lanes(_swap_halves(wr)).reshape(Q_LORA, -1)
    wq = jnp.concatenate([wn, wrp, wrs], axis=-1)
    wkv4 = w_ukv.reshape(KV_LORA, MLA_HEADS, 2, NOPE_DIM)
    wk = wkv4[:, :, 0].reshape(KV_LORA, -1)
    wv = wkv4[:, :, 1].reshape(KV_LORA, -1)
    return wd.astype(BF16), wq.astype(BF16), wk.astype(BF16), wv.astype(BF16)


def _trunk(x, B, L, tm, hg_state, ckv_cache, kpe_cache, conv_state, W, cos_t, sin_t, n_tab_blk):
    norm_w = W["norm_w"]
    nrow = lambda l, k: norm_w[l, k].reshape(1, -1)
    past = 0 if ckv_cache is None else ckv_cache.shape[1]

    p = _hgrn_in(x, nrow(0, 0), W["hgrn_w_in"], W["lb_logits"], 0, tm)
    z, hg_new = _hgrn_rec(p, W["hgrn_gnorm_w"].reshape(1, -1), hg_state, B, L)
    x = _out_proj(z, W["hgrn_w_out"], x, nrow(0, 1), tm)
    x, conv0 = _ffn(x, nrow(0, 2), W["ffn_w_in"], W["ffn_conv_w"][0], W["ffn_conv_b"][0],
                    W["ffn_w_out"], nrow(0, 3), None if conv_state is None else conv_state[0],
                    B, L, tm, 0)

    qcat, ckv, kpe_pad = _mla_in(x, nrow(1, 0), W["mla_wd"], W["mla_q_norm_w"].reshape(1, -1),
                                 W["mla_kv_norm_w"].reshape(1, -1), W["mla_wq"], cos_t, sin_t,
                                 min(tm, 256), n_tab_blk)
    kcat_n, v_n = _kv_up(ckv, kpe_pad, W["mla_wk"], W["mla_wv"], min(tm, 256))
    if ckv_cache is None:
        o = _attn_prompt(qcat, kcat_n, v_n, B, L)
    else:
        cc = ckv_cache.reshape(B * past, KV_LORA)
        kc = _pad_lanes(kpe_cache.reshape(B * past, ROPE_DIM))
        kcat_c, v_c = _kv_up(cc, kc, W["mla_wk"], W["mla_wv"], 512)
        o = _attn_sample(qcat, kcat_c, v_c, kcat_n, v_n, B, L, past)
    x = _out_proj(o, W["mla_w_out"], x, nrow(1, 1), tm)
    x, conv1 = _ffn(x, nrow(1, 2), W["ffn_w_in"], W["ffn_conv_w"][1], W["ffn_conv_b"][1],
                    W["ffn_w_out"], nrow(1, 3), None if conv_state is None else conv_state[1],
                    B, L, tm, 1)

    D = x.shape[-1]
    return (x.reshape(B, L, D), hg_new[None], ckv.reshape(1, B, L, KV_LORA),
            kpe_pad[:, :ROPE_DIM].reshape(1, B, L, ROPE_DIM), jnp.stack([conv0, conv1]))


def kernel(x_prompt, x_sample, state_hgrn, cache_ckv, cache_kpe, state_conv, norm_w, lb_logits,
           hgrn_w_in, hgrn_gnorm_w, hgrn_w_out, mla_w_down, mla_q_norm_w, mla_kv_norm_w,
           mla_w_uq, mla_w_ukv, mla_w_out, ffn_w_in, ffn_conv_w, ffn_conv_b, ffn_w_out):
    wd, wq, wk, wv = _prep_mla_weights(mla_w_down[0], mla_w_uq[0], mla_w_ukv[0])
    W = dict(
        norm_w=norm_w, lb_logits=lb_logits,
        hgrn_w_in=_col_blocks(hgrn_w_in[0].astype(BF16), 1024), hgrn_gnorm_w=hgrn_gnorm_w[0],
        hgrn_w_out=hgrn_w_out[0].astype(BF16),
        mla_wd=wd, mla_wq=wq, mla_wk=wk, mla_wv=wv,
        mla_q_norm_w=mla_q_norm_w[0], mla_kv_norm_w=mla_kv_norm_w[0],
        mla_w_out=mla_w_out[0].astype(BF16),
        ffn_w_in=_col_blocks(ffn_w_in.astype(BF16), FF_TILE), ffn_conv_w=ffn_conv_w,
        ffn_conv_b=ffn_conv_b, ffn_w_out=ffn_w_out.astype(BF16),
    )
    Bp, Lp, D = x_prompt.shape
    Bs, Ls, _ = x_sample.shape
    past = cache_ckv.shape[2]
    cos_t, sin_t = _rope_tables(max(Lp, past + Ls))

    yp, hgp, ckvp, kpep, convp = _trunk(
        x_prompt.reshape(Bp * Lp, D), Bp, Lp, 512, None, None, None, None, W,
        cos_t, sin_t, Lp // 256)

    cos_s = jnp.tile(cos_t[past:past + Ls], (Bs, 1))
    sin_s = jnp.tile(sin_t[past:past + Ls], (Bs, 1))
    ys, hgs, ckvs, kpes, convs = _trunk(
        x_sample.reshape(Bs * Ls, D), Bs, Ls, Bs * Ls, state_hgrn[0], cache_ckv[0], cache_kpe[0],
        state_conv, W, cos_s, sin_s, 1)
    return (yp, ys, hgp, hgs, ckvp, ckvs, kpep, kpes, convp, convs)
```

```python
import functools
import math

import jax
import jax.numpy as jnp
from jax import lax
from jax.experimental import pallas as pl
from jax.experimental.pallas import tpu as pltpu

F32 = jnp.float32
BF16 = jnp.bfloat16

EPS = 1e-6
LANES = 128
CHUNK = 64
HG_HEADS = 16
HG_DK = 128
HG_DV = 128
MLA_HEADS = 16
Q_LORA = 512
KV_LORA = 512
NOPE_DIM = 128
ROPE_DIM = 64
V_DIM = 128
ROPE_BASE = 10000.0
D_FF = 5632
FF_TILE = 512
QK_PAD = 256
VMEM_LIMIT = 56 * 1024 * 1024


def _cparams(sem):
    return pltpu.CompilerParams(dimension_semantics=sem, vmem_limit_bytes=VMEM_LIMIT)


def _rms(x, w):
    ms = jnp.mean(x * x, axis=-1, keepdims=True)
    return x * lax.rsqrt(ms + EPS) * w


def _sigmoid(x):
    return 1.0 / (1.0 + jnp.exp(-x))


def _dot(a, b):
    return jnp.dot(a, b, preferred_element_type=F32)


def _dot_nt(a, b):
    return lax.dot_general(a, b, (((1,), (1,)), ((), ())), preferred_element_type=F32)


def _dot_tn(a, b):
    return lax.dot_general(a, b, (((0,), (0,)), ((), ())), preferred_element_type=F32)


def _hgrn_in_kernel(x_ref, nw_ref, w_ref, lb_ref, o_ref, h_ref, *, lb_row, blk_per_sec):
    j = pl.program_id(1)

    @pl.when(j == 0)
    def _():
        h_ref[...] = _rms(x_ref[...], nw_ref[...]).astype(BF16)

    r = _dot(h_ref[...], w_ref[...])
    sec = j // blk_per_sec
    heads = o_ref.shape[0]

    def store(val):
        for hh in range(heads):
            o_ref[hh] = val[:, hh * LANES:(hh + 1) * LANES]

    @pl.when((sec == 0) | (sec == 3))
    def _():
        store(r * _sigmoid(r))

    @pl.when(sec == 1)
    def _():
        lg = lb_ref[...]
        e = jnp.exp(lg - jnp.max(lg, axis=0, keepdims=True))
        sm = e / jnp.sum(e, axis=0, keepdims=True)
        lb = jnp.sum(sm[0:lb_row + 1], axis=0, keepdims=True)
        store(lb + (1.0 - lb) * _sigmoid(r))

    @pl.when(sec == 2)
    def _():
        store(r)


def _hgrn_in(x, nw, w_bf, lb_logits, lb_row, tm):
    T, D = x.shape
    N = w_bf.shape[1]
    tn = 1024
    sec_w = N // 4
    blk_per_sec = sec_w // tn
    heads = tn // LANES
    return pl.pallas_call(
        functools.partial(_hgrn_in_kernel, lb_row=lb_row, blk_per_sec=blk_per_sec),
        grid=(T // tm, N // tn),
        in_specs=[
            pl.BlockSpec((tm, D), lambda i, j: (i, 0)),
            pl.BlockSpec((1, D), lambda i, j: (0, 0)),
            pl.BlockSpec((D, tn), lambda i, j: (0, j)),
            pl.BlockSpec((lb_logits.shape[0], tn), lambda i, j: (0, j % blk_per_sec)),
        ],
        out_specs=pl.BlockSpec((heads, tm, LANES), lambda i, j: (j, i, 0)),
        out_shape=jax.ShapeDtypeStruct((N // LANES, T, LANES), F32),
        scratch_shapes=[pltpu.VMEM((tm, D), BF16)],
        compiler_params=_cparams(("arbitrary", "arbitrary")),
        name="hgrn_in_proj",
    )(x, nw, w_bf, lb_logits)


def _bcast_mid(c, s):
    C = c.shape[0]
    if s >= 4:
        blk = 2 * s
        c3 = c.reshape(C // blk, blk, LANES)
        return jnp.broadcast_to(c3[:, s - 1:s, :], c3.shape).reshape(C, LANES)
    t = lax.broadcasted_iota(jnp.int32, c.shape, 0)
    if s == 2:
        ph = t & 3
        up1 = pltpu.roll(c, C - 1, 0)
        dn1 = pltpu.roll(c, 1, 0)
        dn2 = pltpu.roll(c, 2, 0)
        return jnp.where(ph == 0, up1, jnp.where(ph == 1, c, jnp.where(ph == 2, dn1, dn2)))
    dn1 = pltpu.roll(c, 1, 0)
    return jnp.where((t & 1) == 1, dn1, c)


def _gla_masks(C):
    t = lax.broadcasted_iota(jnp.int32, (C, LANES), 0)
    rr = lax.broadcasted_iota(jnp.int32, (C, C), 0)
    cc = lax.broadcasted_iota(jnp.int32, (C, C), 1)
    masks = []
    s = 1
    while s < C:
        sh = int(math.log2(2 * s))
        pair = ((rr >> sh) == (cc >> sh)) & ((rr & s) != 0) & ((cc & s) == 0)
        masks.append(((t & s) != 0, pair))
        s *= 2
    return masks


def _gla_head(q, fg, v, masks):
    C = q.shape[0]
    k = 1.0 - fg
    c = jnp.log2(fg)
    a = jnp.zeros((C, C), F32)
    s = 1
    for upper, pair in masks:
        bc = _bcast_mid(c, s)
        qd = q * jnp.exp2(c)
        kd = k * jnp.exp2(jnp.minimum(bc - c, 0.0))
        a = jnp.where(pair, _dot_nt(qd.astype(BF16), kd.astype(BF16)), a)
        c = c + jnp.where(upper, bc, 0.0)
        s *= 2
    b = c
    b_last = b[C - 1:C, :]
    diag = jnp.sum(q * k, axis=-1, keepdims=True)
    qb = (q * jnp.exp2(b)).astype(BF16)
    kdec = (k * jnp.exp2(b_last - b)).astype(BF16)
    return a.astype(BF16), qb, kdec, diag * v, jnp.exp2(b_last)


def _gla_tail(a, qb, kdec, dv, e_last, v_bf, st):
    o = _dot(a, v_bf) + dv + _dot_nt(qb, st.astype(BF16))
    st_new = st * e_last + _dot_tn(v_bf, kdec)
    return o, st_new


def _hgrn_rec_kernel(*refs, C, n_chunks, has_state):
    if has_state:
        (q_ref, f_ref, v_ref, g_ref, gw_ref, s0_ref, z_ref, so_ref,
         st_ref, a_buf, qb_buf, kd_buf, dv_buf, el_buf) = refs
    else:
        (q_ref, f_ref, v_ref, g_ref, gw_ref, z_ref, so_ref,
         st_ref, a_buf, qb_buf, kd_buf, dv_buf, el_buf) = refs
    li = pl.program_id(2)
    hb = st_ref.shape[0]

    @pl.when(li == 0)
    def _():
        for hh in range(hb):
            st_ref[hh] = s0_ref[0, hh].T if has_state else jnp.zeros((HG_DV, HG_DK), F32)

    gw = gw_ref[...]
    masks = _gla_masks(C)

    def rows_of(ci):
        return pl.ds(pl.multiple_of(ci * C, C), C)

    def head(ci, slot):
        rows = rows_of(ci)
        for hh in range(hb):
            a, qb, kdec, dv, e_last = _gla_head(q_ref[hh, rows, :], f_ref[hh, rows, :],
                                                v_ref[hh, rows, :], masks)
            a_buf[slot, hh] = a
            qb_buf[slot, hh] = qb
            kd_buf[slot, hh] = kdec
            dv_buf[slot, hh] = dv
            el_buf[slot, hh] = e_last

    def tail(ci, slot):
        rows = rows_of(ci)
        for hh in range(hb):
            o, st_new = _gla_tail(a_buf[slot, hh], qb_buf[slot, hh], kd_buf[slot, hh],
                                  dv_buf[slot, hh], el_buf[slot, hh],
                                  v_ref[hh, rows, :].astype(BF16), st_ref[hh])
            st_ref[hh] = st_new
            z_ref[rows, hh * LANES:(hh + 1) * LANES] = (
                _rms(o, gw) * g_ref[hh, rows, :]).astype(BF16)

    head(0, 0)
    if n_chunks > 1:
        assert n_chunks % 2 == 0

        def body(kk, carry):
            c0 = 2 * kk
            tail(c0, 0)
            head(c0 + 1, 1)
            tail(c0 + 1, 1)
            head(c0 + 2, 0)
            return carry

        lax.fori_loop(0, n_chunks // 2 - 1, body, 0)
        tail(n_chunks - 2, 0)
        head(n_chunks - 1, 1)
        tail(n_chunks - 1, 1)
    else:
        tail(0, 0)

    @pl.when(li == pl.num_programs(2) - 1)
    def _():
        for hh in range(hb):
            so_ref[0, hh] = st_ref[hh].T


def _hgrn_rec(p, gnorm_w, s0, B, L):
    H = HG_HEADS
    T = B * L
    C = min(CHUNK, L)
    hb = 4
    lb = min(L, 1024)
    nl = L // lb
    ng = H // hb
    has_state = s0 is not None

    def sec_spec(sec):
        return pl.BlockSpec((hb, lb, LANES), lambda b, g, l: (sec * ng + g, b * nl + l, 0))

    in_specs = [sec_spec(0), sec_spec(1), sec_spec(2), sec_spec(3),
                pl.BlockSpec((1, HG_DV), lambda b, g, l: (0, 0))]
    args = [p, p, p, p, gnorm_w]
    if has_state:
        in_specs.append(pl.BlockSpec((1, hb, HG_DK, HG_DV), lambda b, g, l: (b, g, 0, 0)))
        args.append(s0)
    return pl.pallas_call(
        functools.partial(_hgrn_rec_kernel, C=C, n_chunks=lb // C, has_state=has_state),
        grid=(B, ng, nl),
        in_specs=in_specs,
        out_specs=[
            pl.BlockSpec((lb, hb * LANES), lambda b, g, l: (b * nl + l, g)),
            pl.BlockSpec((1, hb, HG_DK, HG_DV), lambda b, g, l: (b, g, 0, 0)),
        ],
        out_shape=[
            jax.ShapeDtypeStruct((T, H * HG_DV), BF16),
            jax.ShapeDtypeStruct((B, H, HG_DK, HG_DV), F32),
        ],
        scratch_shapes=[
            pltpu.VMEM((hb, HG_DV, HG_DK), F32),
            pltpu.VMEM((2, hb, C, C), BF16),
            pltpu.VMEM((2, hb, C, HG_DK), BF16),
            pltpu.VMEM((2, hb, C, HG_DK), BF16),
            pltpu.VMEM((2, hb, C, HG_DV), F32),
            pltpu.VMEM((2, hb, 1, HG_DK), F32),
        ],
        compiler_params=_cparams(("arbitrary", "arbitrary", "arbitrary")),
        name="hgrn_recurrence",
    )(*args)


def _out_proj_kernel(a_ref, w_ref, x_ref, nw_ref, o_ref):
    y = _dot(a_ref[...], w_ref[...])
    o_ref[...] = x_ref[...] + _rms(y, nw_ref[...])


def _out_proj(a_bf, w_bf, x, nw, tm):
    T, K = a_bf.shape
    N = w_bf.shape[1]
    return pl.pallas_call(
        _out_proj_kernel,
        grid=(T // tm,),
        in_specs=[
            pl.BlockSpec((tm, K), lambda i: (i, 0)),
            pl.BlockSpec((K, N), lambda i: (0, 0)),
            pl.BlockSpec((tm, N), lambda i: (i, 0)),
            pl.BlockSpec((1, N), lambda i: (0, 0)),
        ],
        out_specs=pl.BlockSpec((tm, N), lambda i: (i, 0)),
        out_shape=jax.ShapeDtypeStruct((T, N), F32),
        compiler_params=_cparams(("arbitrary",)),
        name="mixer_out_proj",
    )(a_bf, w_bf, x, nw)


def _ffn_kernel(*refs, ns, ls, tps, nj, has_state):
    if has_state:
        (x_ref, nw_ref, wg_ref, wu_ref, cwg_ref, cwu_ref, cbg_ref, cbu_ref, wo_ref, nw2_ref,
         sg_ref, su_ref, o_ref, csg_ref, csu_ref, h_ref, work_ref, carry_ref, act_ref) = refs
    else:
        (x_ref, nw_ref, wg_ref, wu_ref, cwg_ref, cwu_ref, cbg_ref, cbu_ref, wo_ref, nw2_ref,
         o_ref, csg_ref, csu_ref, h_ref, work_ref, carry_ref, act_ref) = refs
        sg_ref = su_ref = None
    i = pl.program_id(0)
    j = pl.program_id(1)
    tf = wg_ref.shape[1]

    def conv(half, w_ref, cw_ref, cb_ref, s_ref, cs_ref):
        u3 = _dot(h_ref[...], w_ref[...]).reshape(ns, ls, tf)
        work_ref[half, :, 8:8 + ls, :] = u3
        prev = jnp.zeros((ns, 2, tf), F32) if s_ref is None else s_ref[...]
        if tps > 1:
            prev = jnp.where((i % tps) == 0, prev, carry_ref[j, half])
        work_ref[half, :, 6:8, :] = prev
        x1 = work_ref[half, :, 7:7 + ls, :]
        x2 = work_ref[half, :, 6:6 + ls, :]
        cw = cw_ref[...]
        c = cb_ref[...] + cw[0:1] * x2 + cw[1:2] * x1 + cw[2:3] * u3
        tail = u3[:, ls - 2:ls, :]
        cs_ref[...] = tail
        if tps > 1:
            carry_ref[j, half] = tail
        return c.reshape(ns * ls, tf)

    def up_conv_act():
        cg = conv(0, wg_ref, cwg_ref, cbg_ref, sg_ref, csg_ref)
        cu = conv(1, wu_ref, cwu_ref, cbu_ref, su_ref, csu_ref)
        return (cg * _sigmoid(cg) * cu).astype(BF16)

    @pl.when(j == 0)
    def _():
        h_ref[...] = _rms(x_ref[...], nw_ref[...]).astype(BF16)
        o_ref[...] = jnp.zeros_like(o_ref)
        act_ref[0] = up_conv_act()

    for par in range(2):
        @pl.when((j > 0) & (j < nj) & (j % 2 == par))
        def _():
            o_ref[...] += _dot(act_ref[1 - par], wo_ref[...])
            act_ref[par] = up_conv_act()

    @pl.when(j == nj)
    def _():
        y = o_ref[...] + _dot(act_ref[(nj - 1) % 2], wo_ref[...])
        o_ref[...] = x_ref[...] + _rms(y, nw2_ref[...])


def _ffn(x, nw, w_in_bf, conv_w, conv_b, w_out_bf, nw2, state, n_streams, L, tm, layer):
    T, D = x.shape
    tf = FF_TILE
    nj = D_FF // tf
    if tm >= L:
        assert tm % L == 0
        ns, ls, tps = tm // L, L, 1
    else:
        assert L % tm == 0
        ns, ls, tps = 1, tm, L // tm
    has_state = state is not None
    cb2 = conv_b.reshape(1, 2 * D_FF)

    def stream_blk(i):
        return (i * tm) // L // ns if ns > 1 else (i * tm) // L

    up = lambda j: jnp.minimum(j, nj - 1)
    dn = lambda j: jnp.maximum(j - 1, 0)
    in_specs = [
        pl.BlockSpec((tm, D), lambda i, j: (i, 0), pipeline_mode=pl.Buffered(1)),
        pl.BlockSpec((1, D), lambda i, j: (0, 0)),
        pl.BlockSpec((None, D, tf), lambda i, j: (layer, 0, up(j))),
        pl.BlockSpec((None, D, tf), lambda i, j: (layer, 0, nj + up(j))),
        pl.BlockSpec((3, tf), lambda i, j: (0, up(j))),
        pl.BlockSpec((3, tf), lambda i, j: (0, nj + up(j))),
        pl.BlockSpec((1, tf), lambda i, j: (0, up(j))),
        pl.BlockSpec((1, tf), lambda i, j: (0, nj + up(j))),
        pl.BlockSpec((None, tf, D), lambda i, j: (layer, dn(j), 0)),
        pl.BlockSpec((1, D), lambda i, j: (0, 0)),
    ]
    args = [x, nw, w_in_bf, w_in_bf, conv_w, conv_w, cb2, cb2, w_out_bf, nw2]
    if has_state:
        in_specs += [
            pl.BlockSpec((ns, 2, tf), lambda i, j: (stream_blk(i), 0, up(j))),
            pl.BlockSpec((ns, 2, tf), lambda i, j: (stream_blk(i), 0, nj + up(j))),
        ]
        args += [state, state]
    out, csg, csu = pl.pallas_call(
        functools.partial(_ffn_kernel, ns=ns, ls=ls, tps=tps, nj=nj, has_state=has_state),
        grid=(T // tm, nj + 1),
        in_specs=in_specs,
        out_specs=[
            pl.BlockSpec((tm, D), lambda i, j: (i, 0)),
            pl.BlockSpec((ns, 2, tf), lambda i, j: (i, 0, up(j))),
            pl.BlockSpec((ns, 2, tf), lambda i, j: (i, 0, up(j))),
        ],
        out_shape=[
            jax.ShapeDtypeStruct((T, D), F32),
            jax.ShapeDtypeStruct((T // ls, 2, D_FF), F32),
            jax.ShapeDtypeStruct((T // ls, 2, D_FF), F32),
        ],
        scratch_shapes=[
            pltpu.VMEM((tm, D), BF16),
            pltpu.VMEM((2, ns, 8 + ls, tf), F32),
            pltpu.VMEM((nj, 2, ns, 2, tf), F32),
            pltpu.VMEM((2, tm, tf), BF16),
        ],
        compiler_params=_cparams(("arbitrary", "arbitrary")),
        name="conv_ffn",
    )(*args)
    tails = jnp.concatenate([csg, csu], axis=-1).reshape(n_streams, tps, 2, 2 * D_FF)
    return out, tails[:, tps - 1]


def _rope_table_kernel(c_ref, s_ref):
    shape = c_ref.shape
    half = ROPE_DIM // 2
    pos = lax.broadcasted_iota(jnp.int32, shape, 0).astype(F32)
    lane = lax.broadcasted_iota(jnp.int32, shape, 1)
    fi = (lane & (half - 1)).astype(F32)
    inv = jnp.exp(fi * (-math.log(ROPE_BASE) / half))
    ang = pos * inv
    valid = lane < ROPE_DIM
    c_ref[...] = jnp.where(valid, jnp.cos(ang), 0.0)
    s_ref[...] = jnp.where(valid, jnp.where(lane < half, -jnp.sin(ang), jnp.sin(ang)), 0.0)


def _rope_tables(n_pos):
    return pl.pallas_call(
        _rope_table_kernel,
        out_shape=[jax.ShapeDtypeStruct((n_pos, LANES), F32)] * 2,
        name="rope_tables",
    )()


def _mla_in_kernel(x_ref, nw_ref, wd_ref, qnw_ref, kvnw_ref, wq_ref, c_ref, s_ref,
                   qcat_ref, ckv_ref, kpe_ref):
    h = _rms(x_ref[...], nw_ref[...]).astype(BF16)
    d = _dot(h, wd_ref[...])
    cq = _rms(d[:, :Q_LORA], qnw_ref[...]).astype(BF16)
    ckv_ref[...] = _rms(d[:, Q_LORA:Q_LORA + KV_LORA], kvnw_ref[...])
    cs = c_ref[...]
    sn = s_ref[...]
    o = Q_LORA + KV_LORA
    kpe_ref[...] = d[:, o:o + LANES] * cs + d[:, o + LANES:o + 2 * LANES] * sn
    hw = MLA_HEADS * LANES
    qn = _dot(cq, wq_ref[:, 0:hw])
    pr = _dot(cq, wq_ref[:, hw:2 * hw])
    ps = _dot(cq, wq_ref[:, 2 * hw:3 * hw])
    for hh in range(MLA_HEADS):
        sl = slice(hh * LANES, (hh + 1) * LANES)
        qcat_ref[hh, :, 0:LANES] = qn[:, sl].astype(BF16)
        qcat_ref[hh, :, LANES:2 * LANES] = (pr[:, sl] * cs + ps[:, sl] * sn).astype(BF16)


def _mla_in(x, nw, wd_bf, qnw, kvnw, wq_bf, cos_t, sin_t, tm, n_tab_blk):
    T, D = x.shape
    return pl.pallas_call(
        _mla_in_kernel,
        grid=(T // tm,),
        in_specs=[
            pl.BlockSpec((tm, D), lambda i: (i, 0)),
            pl.BlockSpec((1, D), lambda i: (0, 0)),
            pl.BlockSpec(wd_bf.shape, lambda i: (0, 0)),
            pl.BlockSpec((1, Q_LORA), lambda i: (0, 0)),
            pl.BlockSpec((1, KV_LORA), lambda i: (0, 0)),
            pl.BlockSpec(wq_bf.shape, lambda i: (0, 0)),
            pl.BlockSpec((tm, LANES), lambda i: (i % n_tab_blk, 0)),
            pl.BlockSpec((tm, LANES), lambda i: (i % n_tab_blk, 0)),
        ],
        out_specs=[
            pl.BlockSpec((MLA_HEADS, tm, QK_PAD), lambda i: (0, i, 0)),
            pl.BlockSpec((tm, KV_LORA), lambda i: (i, 0)),
            pl.BlockSpec((tm, LANES), lambda i: (i, 0)),
        ],
        out_shape=[
            jax.ShapeDtypeStruct((MLA_HEADS, T, QK_PAD), BF16),
            jax.ShapeDtypeStruct((T, KV_LORA), F32),
            jax.ShapeDtypeStruct((T, LANES), F32),
        ],
        compiler_params=_cparams(("arbitrary",)),
        name="mla_in_proj",
    )(x, nw, wd_bf, qnw, kvnw, wq_bf, cos_t, sin_t)


def _kv_up_kernel(ckv_ref, kpe_ref, wk_ref, wv_ref, kcat_ref, v_ref):
    c = ckv_ref[...].astype(BF16)
    kn = _dot(c, wk_ref[...])
    vv = _dot(c, wv_ref[...])
    kp = kpe_ref[...].astype(BF16)
    for hh in range(MLA_HEADS):
        sl = slice(hh * LANES, (hh + 1) * LANES)
        kcat_ref[hh, :, 0:LANES] = kn[:, sl].astype(BF16)
        kcat_ref[hh, :, LANES:2 * LANES] = kp
        v_ref[hh] = vv[:, sl].astype(BF16)


def _kv_up(ckv, kpe_pad, wk_bf, wv_bf, tm):
    R = ckv.shape[0]
    return pl.pallas_call(
        _kv_up_kernel,
        grid=(R // tm,),
        in_specs=[
            pl.BlockSpec((tm, KV_LORA), lambda i: (i, 0)),
            pl.BlockSpec((tm, LANES), lambda i: (i, 0)),
            pl.BlockSpec(wk_bf.shape, lambda i: (0, 0)),
            pl.BlockSpec(wv_bf.shape, lambda i: (0, 0)),
        ],
        out_specs=[
            pl.BlockSpec((MLA_HEADS, tm, QK_PAD), lambda i: (0, i, 0)),
            pl.BlockSpec((MLA_HEADS, tm, V_DIM), lambda i: (0, i, 0)),
        ],
        out_shape=[
            jax.ShapeDtypeStruct((MLA_HEADS, R, QK_PAD), BF16),
            jax.ShapeDtypeStruct((MLA_HEADS, R, V_DIM), BF16),
        ],
        compiler_params=_cparams(("arbitrary",)),
        name="mla_kv_up_proj",
    )(ckv, kpe_pad, wk_bf, wv_bf)


def _attn_prompt_kernel(q_ref, k_ref, v_ref, o_ref, *, L, tq, scale):
    row = lax.broadcasted_iota(jnp.int32, (tq, tq), 0)
    col = lax.broadcasted_iota(jnp.int32, (tq, tq), 1)
    sh = int(math.log2(CHUNK))
    dmask = (row >> sh) >= (col >> sh)

    c2 = scale * math.log2(math.e)
    for qi in range(L // tq):
        n = qi * tq
        q = q_ref[0, n:n + tq, :]
        s_d = jnp.where(dmask, _dot_nt(q, k_ref[0, n:n + tq, :]), -jnp.inf)
        m = jnp.max(s_d, axis=-1, keepdims=True)
        if qi > 0:
            s_p = _dot_nt(q, k_ref[0, 0:n, :])
            m = jnp.maximum(m, jnp.max(s_p, axis=-1, keepdims=True))
        p_d = jnp.exp2((s_d - m) * c2)
        l = jnp.sum(p_d, axis=-1, keepdims=True)
        acc = _dot(p_d.astype(BF16), v_ref[0, n:n + tq, :])
        if qi > 0:
            p_p = jnp.exp2((s_p - m) * c2)
            l = l + jnp.sum(p_p, axis=-1, keepdims=True)
            acc = acc + _dot(p_p.astype(BF16), v_ref[0, 0:n, :])
        o_ref[n:n + tq, :] = (acc / l).astype(BF16)


def _attn_prompt(qcat, kcat, v, B, L):
    H = MLA_HEADS
    scale = (NOPE_DIM + ROPE_DIM) ** -0.5
    return pl.pallas_call(
        functools.partial(_attn_prompt_kernel, L=L, tq=256, scale=scale),
        grid=(B, H),
        in_specs=[
            pl.BlockSpec((1, L, QK_PAD), lambda b, h: (h, b, 0)),
            pl.BlockSpec((1, L, QK_PAD), lambda b, h: (h, b, 0)),
            pl.BlockSpec((1, L, V_DIM), lambda b, h: (h, b, 0)),
        ],
        out_specs=pl.BlockSpec((L, V_DIM), lambda b, h: (b, h)),
        out_shape=jax.ShapeDtypeStruct((B * L, H * V_DIM), BF16),
        compiler_params=_cparams(("arbitrary", "arbitrary")),
        name="attn_prompt",
    )(qcat, kcat, v)


def _attn_sample_kernel(q_ref, cc_ref, pc_ref, cn_ref, pn_ref, wk_ref, wv_ref, o_ref, *,
                        past, scale):
    H, Lq, _ = q_ref.shape
    qlat, qrope = [], []
    for hh in range(H):
        sl = slice(hh * LANES, (hh + 1) * LANES)
        qlat.append(_dot_nt(q_ref[hh, :, 0:NOPE_DIM], wk_ref[:, sl]).astype(BF16))
        qrope.append(q_ref[hh, :, NOPE_DIM:NOPE_DIM + ROPE_DIM])
    qlat = jnp.concatenate(qlat, axis=0)
    qrope = jnp.concatenate(qrope, axis=0)
    cc = cc_ref[0].astype(BF16)
    cn = cn_ref[...].astype(BF16)
    s1 = _dot_nt(qlat, cc) + _dot_nt(qrope, pc_ref[0].astype(BF16))
    s2 = _dot_nt(qlat, cn) + _dot_nt(qrope, pn_ref[:, 0:ROPE_DIM].astype(BF16))
    sh = int(math.log2(CHUNK))

    def q_chunk(n):
        t = lax.broadcasted_iota(jnp.int32, (H, Lq, n), 1).reshape(H * Lq, n)
        return (t + past) >> sh

    kc1 = lax.broadcasted_iota(jnp.int32, s1.shape, 1) >> sh
    kc2 = (lax.broadcasted_iota(jnp.int32, s2.shape, 1) + past) >> sh
    s1 = jnp.where(kc1 <= q_chunk(s1.shape[1]), s1 * scale, -jnp.inf)
    s2 = jnp.where(kc2 <= q_chunk(s2.shape[1]), s2 * scale, -jnp.inf)
    m = jnp.maximum(jnp.max(s1, axis=-1, keepdims=True), jnp.max(s2, axis=-1, keepdims=True))
    p1 = jnp.exp(s1 - m)
    p2 = jnp.exp(s2 - m)
    l = jnp.sum(p1, axis=-1, keepdims=True) + jnp.sum(p2, axis=-1, keepdims=True)
    olat = ((_dot(p1.astype(BF16), cc) + _dot(p2.astype(BF16), cn)) / l).astype(BF16)
    for hh in range(H):
        sl = slice(hh * LANES, (hh + 1) * LANES)
        o_ref[:, sl] = _dot(olat[hh * Lq:(hh + 1) * Lq], wv_ref[:, sl]).astype(BF16)


def _attn_sample(qcat, ckv_cache, kpe_cache, ckv_new, kpe_new, wk_bf, wv_bf, B, L, past):
    H = MLA_HEADS
    scale = (NOPE_DIM + ROPE_DIM) ** -0.5
    return pl.pallas_call(
        functools.partial(_attn_sample_kernel, past=past, scale=scale),
        grid=(B,),
        in_specs=[
            pl.BlockSpec((H, L, QK_PAD), lambda b: (0, b, 0)),
            pl.BlockSpec((1, past, KV_LORA), lambda b: (b, 0, 0)),
            pl.BlockSpec((1, past, ROPE_DIM), lambda b: (b, 0, 0)),
            pl.BlockSpec((L, KV_LORA), lambda b: (b, 0)),
            pl.BlockSpec((L, LANES), lambda b: (b, 0)),
            pl.BlockSpec(wk_bf.shape, lambda b: (0, 0)),
            pl.BlockSpec(wv_bf.shape, lambda b: (0, 0)),
        ],
        out_specs=pl.BlockSpec((L, H * V_DIM), lambda b: (b, 0)),
        out_shape=jax.ShapeDtypeStruct((B * L, H * V_DIM), BF16),
        compiler_params=_cparams(("arbitrary",)),
        name="attn_sample",
    )(qcat, ckv_cache, kpe_cache, ckv_new, kpe_new, wk_bf, wv_bf)


def _swap_halves(w):
    half = w.shape[-1] // 2
    return jnp.concatenate([w[..., half:], w[..., :half]], axis=-1)


def _pad_lanes(w):
    return jnp.pad(w, [(0, 0)] * (w.ndim - 1) + [(0, LANES - w.shape[-1])])


def _prep_mla_weights(w_down, w_uq, w_ukv):
    o = Q_LORA + KV_LORA
    wpe = w_down[:, o:]
    wd = jnp.concatenate([w_down[:, :o], _pad_lanes(wpe), _pad_lanes(_swap_halves(wpe))], axis=-1)
    wq3 = w_uq.reshape(Q_LORA, MLA_HEADS, NOPE_DIM + ROPE_DIM)
    wn = wq3[..., :NOPE_DIM].reshape(Q_LORA, -1)
    wr = wq3[..., NOPE_DIM:]
    wrp = _pad_lanes(wr).reshape(Q_LORA, -1)
    wrs = _pad_lanes(_swap_halves(wr)).reshape(Q_LORA, -1)
    wq = jnp.concatenate([wn, wrp, wrs], axis=-1)
    wkv4 = w_ukv.reshape(KV_LORA, MLA_HEADS, 2, NOPE_DIM)
    wk = wkv4[:, :, 0].reshape(KV_LORA, -1)
    wv = wkv4[:, :, 1].reshape(KV_LORA, -1)
    return wd.astype(BF16), wq.astype(BF16), wk.astype(BF16), wv.astype(BF16)


def _trunk(x, B, L, tm, tm_ffn, hg_state, ckv_cache, kpe_cache, conv_state, W, cos_t, sin_t, n_tab_blk):
    norm_w = W["norm_w"]
    nrow = lambda l, k: norm_w[l, k].reshape(1, -1)
    past = 0 if ckv_cache is None else ckv_cache.shape[1]

    p = _hgrn_in(x, nrow(0, 0), W["hgrn_w_in"], W["lb_logits"], 0, tm)
    z, hg_new = _hgrn_rec(p, W["hgrn_gnorm_w"].reshape(1, -1), hg_state, B, L)
    x = _out_proj(z, W["hgrn_w_out"], x, nrow(0, 1), tm)
    x, conv0 = _ffn(x, nrow(0, 2), W["ffn_w_in"], W["ffn_conv_w"][0], W["ffn_conv_b"][0],
                    W["ffn_w_out"], nrow(0, 3), None if conv_state is None else conv_state[0],
                    B, L, tm_ffn, 0)

    qcat, ckv, kpe_pad = _mla_in(x, nrow(1, 0), W["mla_wd"], W["mla_q_norm_w"].reshape(1, -1),
                                 W["mla_kv_norm_w"].reshape(1, -1), W["mla_wq"], cos_t, sin_t,
                                 min(tm, 256), n_tab_blk)
    if ckv_cache is None:
        kcat_n, v_n = _kv_up(ckv, kpe_pad, W["mla_wk"], W["mla_wv"], min(tm, 256))
        o = _attn_prompt(qcat, kcat_n, v_n, B, L)
    else:
        o = _attn_sample(qcat, ckv_cache, kpe_cache, ckv, kpe_pad, W["mla_wk"], W["mla_wv"],
                         B, L, past)
    x = _out_proj(o, W["mla_w_out"], x, nrow(1, 1), tm)
    x, conv1 = _ffn(x, nrow(1, 2), W["ffn_w_in"], W["ffn_conv_w"][1], W["ffn_conv_b"][1],
                    W["ffn_w_out"], nrow(1, 3), None if conv_state is None else conv_state[1],
                    B, L, tm_ffn, 1)

    D = x.shape[-1]
    return (x.reshape(B, L, D), hg_new[None], ckv.reshape(1, B, L, KV_LORA),
            kpe_pad[:, :ROPE_DIM].reshape(1, B, L, ROPE_DIM), jnp.stack([conv0, conv1]))


def kernel(x_prompt, x_sample, state_hgrn, cache_ckv, cache_kpe, state_conv, norm_w, lb_logits,
           hgrn_w_in, hgrn_gnorm_w, hgrn_w_out, mla_w_down, mla_q_norm_w, mla_kv_norm_w,
           mla_w_uq, mla_w_ukv, mla_w_out, ffn_w_in, ffn_conv_w, ffn_conv_b, ffn_w_out):
    wd, wq, wk, wv = _prep_mla_weights(mla_w_down[0], mla_w_uq[0], mla_w_ukv[0])
    W = dict(
        norm_w=norm_w, lb_logits=lb_logits,
        hgrn_w_in=hgrn_w_in[0].astype(BF16), hgrn_gnorm_w=hgrn_gnorm_w[0],
        hgrn_w_out=hgrn_w_out[0].astype(BF16),
        mla_wd=wd, mla_wq=wq, mla_wk=wk, mla_wv=wv,
        mla_q_norm_w=mla_q_norm_w[0], mla_kv_norm_w=mla_kv_norm_w[0],
        mla_w_out=mla_w_out[0].astype(BF16),
        ffn_w_in=ffn_w_in.astype(BF16), ffn_conv_w=ffn_conv_w,
        ffn_conv_b=ffn_conv_b, ffn_w_out=ffn_w_out.astype(BF16),
    )
    Bp, Lp, D = x_prompt.shape
    Bs, Ls, _ = x_sample.shape
    past = cache_ckv.shape[2]
    cos_t, sin_t = _rope_tables(max(Lp, past + Ls))

    yp, hgp, ckvp, kpep, convp = _trunk(
        x_prompt.reshape(Bp * Lp, D), Bp, Lp, 512, 1024, None, None, None, None, W,
        cos_t, sin_t, Lp // 256)

    cos_s = jnp.tile(cos_t[past:past + Ls], (Bs, 1))
    sin_s = jnp.tile(sin_t[past:past + Ls], (Bs, 1))
    ys, hgs, ckvs, kpes, convs = _trunk(
        x_sample.reshape(Bs * Ls, D), Bs, Ls, Bs * Ls, Bs * Ls, state_hgrn[0], cache_ckv[0], cache_kpe[0],
        state_conv, W, cos_s, sin_s, 1)
    return (yp, ys, hgp, hgs, ckvp, ckvs, kpep, kpes, convp, convs)
```

```python
import functools
import math

import jax
import jax.numpy as jnp
from jax import lax
from jax.experimental import pallas as pl
from jax.experimental.pallas import tpu as pltpu

F32 = jnp.float32
BF16 = jnp.bfloat16

EPS = 1e-6
LANES = 128
CHUNK = 64
HG_HEADS = 16
HG_DK = 128
HG_DV = 128
MLA_HEADS = 16
Q_LORA = 512
KV_LORA = 512
NOPE_DIM = 128
ROPE_DIM = 64
V_DIM = 128
ROPE_BASE = 10000.0
D_FF = 5632
FF_TILE = 512
FF_PIECE = 512
QK_PAD = 256
VMEM_LIMIT = 56 * 1024 * 1024


def _cparams(sem):
    return pltpu.CompilerParams(dimension_semantics=sem, vmem_limit_bytes=VMEM_LIMIT)


def _rms(x, w):
    ms = jnp.mean(x * x, axis=-1, keepdims=True)
    return x * lax.rsqrt(ms + EPS) * w


def _sigmoid(x):
    return 1.0 / (1.0 + jnp.exp(-x))


def _dot(a, b):
    return jnp.dot(a, b, preferred_element_type=F32)


def _dot_nt(a, b):
    return lax.dot_general(a, b, (((1,), (1,)), ((), ())), preferred_element_type=F32)


def _dot_tn(a, b):
    return lax.dot_general(a, b, (((0,), (0,)), ((), ())), preferred_element_type=F32)


def _hgrn_in_kernel(x_ref, nw_ref, w_ref, lb_ref, o_ref, h_ref, *, lb_row, blk_per_sec):
    j = pl.program_id(1)

    @pl.when(j == 0)
    def _():
        h_ref[...] = _rms(x_ref[...], nw_ref[...]).astype(BF16)

    r = _dot(h_ref[...], w_ref[...])
    sec = j // blk_per_sec
    heads = o_ref.shape[0]

    def store(val):
        for hh in range(heads):
            o_ref[hh] = val[:, hh * LANES:(hh + 1) * LANES]

    @pl.when((sec == 0) | (sec == 3))
    def _():
        store(r * _sigmoid(r))

    @pl.when(sec == 1)
    def _():
        lg = lb_ref[...]
        e = jnp.exp(lg - jnp.max(lg, axis=0, keepdims=True))
        sm = e / jnp.sum(e, axis=0, keepdims=True)
        lb = jnp.sum(sm[0:lb_row + 1], axis=0, keepdims=True)
        store(lb + (1.0 - lb) * _sigmoid(r))

    @pl.when(sec == 2)
    def _():
        store(r)


def _hgrn_in(x, nw, w_bf, lb_logits, lb_row, tm):
    T, D = x.shape
    N = w_bf.shape[1]
    tn = 1024
    sec_w = N // 4
    blk_per_sec = sec_w // tn
    heads = tn // LANES
    return pl.pallas_call(
        functools.partial(_hgrn_in_kernel, lb_row=lb_row, blk_per_sec=blk_per_sec),
        grid=(T // tm, N // tn),
        in_specs=[
            pl.BlockSpec((tm, D), lambda i, j: (i, 0)),
            pl.BlockSpec((1, D), lambda i, j: (0, 0)),
            pl.BlockSpec((D, tn), lambda i, j: (0, j)),
            pl.BlockSpec((lb_logits.shape[0], tn), lambda i, j: (0, j % blk_per_sec)),
        ],
        out_specs=pl.BlockSpec((heads, tm, LANES), lambda i, j: (j, i, 0)),
        out_shape=jax.ShapeDtypeStruct((N // LANES, T, LANES), F32),
        scratch_shapes=[pltpu.VMEM((tm, D), BF16)],
        compiler_params=_cparams(("arbitrary", "arbitrary")),
        name="hgrn_in_proj",
    )(x, nw, w_bf, lb_logits)


SUB = 8


def _bcast_mid_in_tile(c, s, t8):
    if s == 4:
        return jnp.broadcast_to(c[s - 1:s, :], c.shape)
    dn1 = pltpu.roll(c, 1, 0)
    if s == 1:
        return jnp.where((t8 & 1) == 1, dn1, c)
    ph = t8 & 3
    up1 = pltpu.roll(c, SUB - 1, 0)
    dn2 = pltpu.roll(c, 2, 0)
    return jnp.where(ph == 0, up1, jnp.where(ph == 1, c, jnp.where(ph == 2, dn1, dn2)))


def _gla_masks(C):
    t8 = lax.broadcasted_iota(jnp.int32, (SUB, LANES), 0)
    rr = lax.broadcasted_iota(jnp.int32, (C, C), 0)
    cc = lax.broadcasted_iota(jnp.int32, (C, C), 1)
    pairs = []
    s = 1
    while s < C:
        sh = int(math.log2(2 * s))
        pairs.append(((rr >> sh) == (cc >> sh)) & ((rr & s) != 0) & ((cc & s) == 0))
        s *= 2
    return t8, pairs


def _gla_head(q, fg, v, masks):
    C = q.shape[0]
    nt = C // SUB
    t8, pairs = masks
    k = 1.0 - fg
    tiles = lambda x: [x[i * SUB:(i + 1) * SUB] for i in range(nt)]
    qs, ks = tiles(q), tiles(k)
    cs = [jnp.log2(f) for f in tiles(fg)]
    zero = jnp.zeros((SUB, LANES), F32)
    a = jnp.zeros((C, C), F32)
    s = 1
    for pair in pairs:
        if s < SUB:
            upper = (t8 & s) != 0
            qd, kd = [], []
            for i in range(nt):
                bc = _bcast_mid_in_tile(cs[i], s, t8)
                qd.append(qs[i] * jnp.exp2(cs[i]))
                kd.append(ks[i] * jnp.exp2(jnp.minimum(bc - cs[i], 0.0)))
                cs[i] = cs[i] + jnp.where(upper, bc, 0.0)
        else:
            m = s // SUB
            qd, kd = [zero] * nt, [zero] * nt
            for blk in range(0, nt, 2 * m):
                bc = jnp.broadcast_to(cs[blk + m - 1][SUB - 1:SUB, :], (SUB, LANES))
                for i in range(blk, blk + m):
                    kd[i] = ks[i] * jnp.exp2(bc - cs[i])
                for i in range(blk + m, blk + 2 * m):
                    qd[i] = qs[i] * jnp.exp2(cs[i])
                    cs[i] = cs[i] + bc
        a_s = _dot_nt(jnp.concatenate(qd, axis=0).astype(BF16),
                      jnp.concatenate(kd, axis=0).astype(BF16))
        a = jnp.where(pair, a_s, a)
        s *= 2
    b = jnp.concatenate(cs, axis=0)
    b_last = b[C - 1:C, :]
    diag = jnp.sum(q * k, axis=-1, keepdims=True)
    qb = (q * jnp.exp2(b)).astype(BF16)
    kdec = (k * jnp.exp2(b_last - b)).astype(BF16)
    return a.astype(BF16), qb, kdec, diag * v, jnp.exp2(b_last)


def _gla_tail(a, qb, kdec, dv, e_last, v_bf, st):
    o = _dot(a, v_bf) + dv + _dot_nt(qb, st.astype(BF16))
    st_new = st * e_last + _dot_tn(v_bf, kdec)
    return o, st_new


def _hgrn_rec_kernel(*refs, C, n_chunks, has_state):
    if has_state:
        (q_ref, f_ref, v_ref, g_ref, gw_ref, s0_ref, z_ref, so_ref,
         st_ref, a_buf, qb_buf, kd_buf, dv_buf, el_buf) = refs
    else:
        (q_ref, f_ref, v_ref, g_ref, gw_ref, z_ref, so_ref,
         st_ref, a_buf, qb_buf, kd_buf, dv_buf, el_buf) = refs
    li = pl.program_id(2)
    hb = st_ref.shape[0]

    @pl.when(li == 0)
    def _():
        for hh in range(hb):
            st_ref[hh] = s0_ref[0, hh].T if has_state else jnp.zeros((HG_DV, HG_DK), F32)

    gw = gw_ref[...]
    masks = _gla_masks(C)

    def rows_of(ci):
        return pl.ds(pl.multiple_of(ci * C, C), C)

    def head(ci, slot):
        rows = rows_of(ci)
        for hh in range(hb):
            a, qb, kdec, dv, e_last = _gla_head(q_ref[hh, rows, :], f_ref[hh, rows, :],
                                                v_ref[hh, rows, :], masks)
            a_buf[slot, hh] = a
            qb_buf[slot, hh] = qb
            kd_buf[slot, hh] = kdec
            dv_buf[slot, hh] = dv
            el_buf[slot, hh] = e_last

    def tail(ci, slot):
        rows = rows_of(ci)
        for hh in range(hb):
            o, st_new = _gla_tail(a_buf[slot, hh], qb_buf[slot, hh], kd_buf[slot, hh],
                                  dv_buf[slot, hh], el_buf[slot, hh],
                                  v_ref[hh, rows, :].astype(BF16), st_ref[hh])
            st_ref[hh] = st_new
            z_ref[rows, hh * LANES:(hh + 1) * LANES] = (
                _rms(o, gw) * g_ref[hh, rows, :]).astype(BF16)

    head(0, 0)
    if n_chunks > 1:
        assert n_chunks % 2 == 0

        def body(kk, carry):
            c0 = 2 * kk
            tail(c0, 0)
            head(c0 + 1, 1)
            tail(c0 + 1, 1)
            head(c0 + 2, 0)
            return carry

        lax.fori_loop(0, n_chunks // 2 - 1, body, 0)
        tail(n_chunks - 2, 0)
        head(n_chunks - 1, 1)
        tail(n_chunks - 1, 1)
    else:
        tail(0, 0)

    @pl.when(li == pl.num_programs(2) - 1)
    def _():
        for hh in range(hb):
            so_ref[0, hh] = st_ref[hh].T


def _hgrn_rec(p, gnorm_w, s0, B, L):
    H = HG_HEADS
    T = B * L
    C = min(CHUNK, L)
    hb = 4 if L > CHUNK else H
    lb = min(L, 1024)
    nl = L // lb
    ng = H // hb
    has_state = s0 is not None

    def sec_spec(sec):
        return pl.BlockSpec((hb, lb, LANES), lambda b, g, l: (sec * ng + g, b * nl + l, 0))

    in_specs = [sec_spec(0), sec_spec(1), sec_spec(2), sec_spec(3),
                pl.BlockSpec((1, HG_DV), lambda b, g, l: (0, 0))]
    args = [p, p, p, p, gnorm_w]
    if has_state:
        in_specs.append(pl.BlockSpec((1, hb, HG_DK, HG_DV), lambda b, g, l: (b, g, 0, 0)))
        args.append(s0)
    return pl.pallas_call(
        functools.partial(_hgrn_rec_kernel, C=C, n_chunks=lb // C, has_state=has_state),
        grid=(B, ng, nl),
        in_specs=in_specs,
        out_specs=[
            pl.BlockSpec((lb, hb * LANES), lambda b, g, l: (b * nl + l, g)),
            pl.BlockSpec((1, hb, HG_DK, HG_DV), lambda b, g, l: (b, g, 0, 0)),
        ],
        out_shape=[
            jax.ShapeDtypeStruct((T, H * HG_DV), BF16),
            jax.ShapeDtypeStruct((B, H, HG_DK, HG_DV), F32),
        ],
        scratch_shapes=[
            pltpu.VMEM((hb, HG_DV, HG_DK), F32),
            pltpu.VMEM((2, hb, C, C), BF16),
            pltpu.VMEM((2, hb, C, HG_DK), BF16),
            pltpu.VMEM((2, hb, C, HG_DK), BF16),
            pltpu.VMEM((2, hb, C, HG_DV), F32),
            pltpu.VMEM((2, hb, 1, HG_DK), F32),
        ],
        compiler_params=_cparams(("arbitrary", "arbitrary", "arbitrary")),
        name="hgrn_recurrence",
    )(*args)


def _out_proj_kernel(a_ref, w_ref, x_ref, nw_ref, o_ref):
    y = _dot(a_ref[...], w_ref[...])
    o_ref[...] = x_ref[...] + _rms(y, nw_ref[...])


def _out_proj(a_bf, w_bf, x, nw, tm):
    T, K = a_bf.shape
    N = w_bf.shape[1]
    return pl.pallas_call(
        _out_proj_kernel,
        grid=(T // tm,),
        in_specs=[
            pl.BlockSpec((tm, K), lambda i: (i, 0)),
            pl.BlockSpec((K, N), lambda i: (0, 0)),
            pl.BlockSpec((tm, N), lambda i: (i, 0)),
            pl.BlockSpec((1, N), lambda i: (0, 0)),
        ],
        out_specs=pl.BlockSpec((tm, N), lambda i: (i, 0)),
        out_shape=jax.ShapeDtypeStruct((T, N), F32),
        compiler_params=_cparams(("arbitrary",)),
        name="mixer_out_proj",
    )(a_bf, w_bf, x, nw)


def _ffn_kernel(*refs, ns, ls, tps, nj, has_state):
    if has_state:
        (x_ref, nw_ref, wg_ref, wu_ref, cwg_ref, cwu_ref, cbg_ref, cbu_ref, wo_ref, nw2_ref,
         sg_ref, su_ref, o_ref, csg_ref, csu_ref, h_ref, work_ref, carry_ref, act_ref) = refs
    else:
        (x_ref, nw_ref, wg_ref, wu_ref, cwg_ref, cwu_ref, cbg_ref, cbu_ref, wo_ref, nw2_ref,
         o_ref, csg_ref, csu_ref, h_ref, work_ref, carry_ref, act_ref) = refs
        sg_ref = su_ref = None
    i = pl.program_id(0)
    j = pl.program_id(1)
    tf = wg_ref.shape[1]

    def conv(half, cols, w_ref, cw_ref, cb_ref, s_ref, cs_ref):
        tc = cols.stop - cols.start
        u3 = _dot(h_ref[...], w_ref[:, cols]).reshape(ns, ls, tc)
        work_ref[half, :, 8:8 + ls, cols] = u3
        prev = jnp.zeros((ns, 2, tc), F32) if s_ref is None else s_ref[:, :, cols]
        if tps > 1:
            prev = jnp.where((i % tps) == 0, prev, carry_ref[j, half, :, :, cols])
        work_ref[half, :, 6:8, cols] = prev
        x1 = work_ref[half, :, 7:7 + ls, cols]
        x2 = work_ref[half, :, 6:6 + ls, cols]
        cw = cw_ref[:, cols]
        c = cb_ref[:, cols] + cw[0:1] * x2 + cw[1:2] * x1 + cw[2:3] * u3
        tail = u3[:, ls - 2:ls, :]
        cs_ref[:, :, cols] = tail
        if tps > 1:
            carry_ref[j, half, :, :, cols] = tail
        return c.reshape(ns * ls, tc)

    def up_conv_act(slot):
        for c0 in range(0, tf, FF_PIECE):
            cols = slice(c0, c0 + FF_PIECE)
            cg = conv(0, cols, wg_ref, cwg_ref, cbg_ref, sg_ref, csg_ref)
            cu = conv(1, cols, wu_ref, cwu_ref, cbu_ref, su_ref, csu_ref)
            act_ref[slot, :, cols] = (cg * _sigmoid(cg) * cu).astype(BF16)

    @pl.when(j == 0)
    def _():
        h_ref[...] = _rms(x_ref[...], nw_ref[...]).astype(BF16)
        o_ref[...] = jnp.zeros_like(o_ref)
        up_conv_act(0)

    for par in range(2):
        @pl.when((j > 0) & (j < nj) & (j % 2 == par))
        def _():
            o_ref[...] += _dot(act_ref[1 - par], wo_ref[...])
            up_conv_act(par)

    @pl.when(j == nj)
    def _():
        y = o_ref[...] + _dot(act_ref[(nj - 1) % 2], wo_ref[...])
        o_ref[...] = x_ref[...] + _rms(y, nw2_ref[...])


def _ffn(x, nw, w_in_bf, conv_w, conv_b, w_out_bf, nw2, state, n_streams, L, tm, layer):
    T, D = x.shape
    tf = FF_TILE
    nj = D_FF // tf
    if tm >= L:
        assert tm % L == 0
        ns, ls, tps = tm // L, L, 1
    else:
        assert L % tm == 0
        ns, ls, tps = 1, tm, L // tm
    has_state = state is not None
    cb2 = conv_b.reshape(1, 2 * D_FF)

    def stream_blk(i):
        return (i * tm) // L // ns if ns > 1 else (i * tm) // L

    up = lambda j: jnp.minimum(j, nj - 1)
    dn = lambda j: jnp.maximum(j - 1, 0)
    in_specs = [
        pl.BlockSpec((tm, D), lambda i, j: (i, 0), pipeline_mode=pl.Buffered(1)),
        pl.BlockSpec((1, D), lambda i, j: (0, 0)),
        pl.BlockSpec((None, D, tf), lambda i, j: (layer, 0, up(j))),
        pl.BlockSpec((None, D, tf), lambda i, j: (layer, 0, nj + up(j))),
        pl.BlockSpec((3, tf), lambda i, j: (0, up(j))),
        pl.BlockSpec((3, tf), lambda i, j: (0, nj + up(j))),
        pl.BlockSpec((1, tf), lambda i, j: (0, up(j))),
        pl.BlockSpec((1, tf), lambda i, j: (0, nj + up(j))),
        pl.BlockSpec((None, tf, D), lambda i, j: (layer, dn(j), 0)),
        pl.BlockSpec((1, D), lambda i, j: (0, 0)),
    ]
    args = [x, nw, w_in_bf, w_in_bf, conv_w, conv_w, cb2, cb2, w_out_bf, nw2]
    if has_state:
        in_specs += [
            pl.BlockSpec((ns, 2, tf), lambda i, j: (stream_blk(i), 0, up(j))),
            pl.BlockSpec((ns, 2, tf), lambda i, j: (stream_blk(i), 0, nj + up(j))),
        ]
        args += [state, state]
    out, csg, csu = pl.pallas_call(
        functools.partial(_ffn_kernel, ns=ns, ls=ls, tps=tps, nj=nj, has_state=has_state),
        grid=(T // tm, nj + 1),
        in_specs=in_specs,
        out_specs=[
            pl.BlockSpec((tm, D), lambda i, j: (i, 0)),
            pl.BlockSpec((ns, 2, tf), lambda i, j: (i, 0, up(j))),
            pl.BlockSpec((ns, 2, tf), lambda i, j: (i, 0, up(j))),
        ],
        out_shape=[
            jax.ShapeDtypeStruct((T, D), F32),
            jax.ShapeDtypeStruct((T // ls, 2, D_FF), F32),
            jax.ShapeDtypeStruct((T // ls, 2, D_FF), F32),
        ],
        scratch_shapes=[
            pltpu.VMEM((tm, D), BF16),
            pltpu.VMEM((2, ns, 8 + ls, tf), F32),
            pltpu.VMEM((nj, 2, ns, 2, tf), F32),
            pltpu.VMEM((2, tm, tf), BF16),
        ],
        compiler_params=_cparams(("arbitrary", "arbitrary")),
        name="conv_ffn",
    )(*args)
    tails = jnp.concatenate([csg, csu], axis=-1).reshape(n_streams, tps, 2, 2 * D_FF)
    return out, tails[:, tps - 1]


def _rope_table_kernel(c_ref, s_ref):
    shape = c_ref.shape
    half = ROPE_DIM // 2
    pos = lax.broadcasted_iota(jnp.int32, shape, 0).astype(F32)
    lane = lax.broadcasted_iota(jnp.int32, shape, 1)
    fi = (lane & (half - 1)).astype(F32)
    inv = jnp.exp(fi * (-math.log(ROPE_BASE) / half))
    ang = pos * inv
    valid = lane < ROPE_DIM
    c_ref[...] = jnp.where(valid, jnp.cos(ang), 0.0)
    s_ref[...] = jnp.where(valid, jnp.where(lane < half, -jnp.sin(ang), jnp.sin(ang)), 0.0)


def _rope_tables(n_pos):
    return pl.pallas_call(
        _rope_table_kernel,
        out_shape=[jax.ShapeDtypeStruct((n_pos, LANES), F32)] * 2,
        name="rope_tables",
    )()


def _mla_in_kernel(x_ref, nw_ref, wd_ref, qnw_ref, kvnw_ref, wq_ref, c_ref, s_ref,
                   qcat_ref, ckv_ref, kpe_ref):
    h = _rms(x_ref[...], nw_ref[...]).astype(BF16)
    d = _dot(h, wd_ref[...])
    cq = _rms(d[:, :Q_LORA], qnw_ref[...]).astype(BF16)
    ckv_ref[...] = _rms(d[:, Q_LORA:Q_LORA + KV_LORA], kvnw_ref[...])
    cs = c_ref[...]
    sn = s_ref[...]
    o = Q_LORA + KV_LORA
    kpe_ref[...] = d[:, o:o + LANES] * cs + d[:, o + LANES:o + 2 * LANES] * sn
    hw = MLA_HEADS * LANES
    qn = _dot(cq, wq_ref[:, 0:hw])
    pr = _dot(cq, wq_ref[:, hw:2 * hw])
    ps = _dot(cq, wq_ref[:, 2 * hw:3 * hw])
    for hh in range(MLA_HEADS):
        sl = slice(hh * LANES, (hh + 1) * LANES)
        qcat_ref[hh, :, 0:LANES] = qn[:, sl].astype(BF16)
        qcat_ref[hh, :, LANES:2 * LANES] = (pr[:, sl] * cs + ps[:, sl] * sn).astype(BF16)


def _mla_in(x, nw, wd_bf, qnw, kvnw, wq_bf, cos_t, sin_t, tm, n_tab_blk):
    T, D = x.shape
    return pl.pallas_call(
        _mla_in_kernel,
        grid=(T // tm,),
        in_specs=[
            pl.BlockSpec((tm, D), lambda i: (i, 0)),
            pl.BlockSpec((1, D), lambda i: (0, 0)),
            pl.BlockSpec(wd_bf.shape, lambda i: (0, 0)),
            pl.BlockSpec((1, Q_LORA), lambda i: (0, 0)),
            pl.BlockSpec((1, KV_LORA), lambda i: (0, 0)),
            pl.BlockSpec(wq_bf.shape, lambda i: (0, 0)),
            pl.BlockSpec((tm, LANES), lambda i: (i % n_tab_blk, 0)),
            pl.BlockSpec((tm, LANES), lambda i: (i % n_tab_blk, 0)),
        ],
        out_specs=[
            pl.BlockSpec((MLA_HEADS, tm, QK_PAD), lambda i: (0, i, 0)),
            pl.BlockSpec((tm, KV_LORA), lambda i: (i, 0)),
            pl.BlockSpec((tm, LANES), lambda i: (i, 0)),
        ],
        out_shape=[
            jax.ShapeDtypeStruct((MLA_HEADS, T, QK_PAD), BF16),
            jax.ShapeDtypeStruct((T, KV_LORA), F32),
            jax.ShapeDtypeStruct((T, LANES), F32),
        ],
        compiler_params=_cparams(("arbitrary",)),
        name="mla_in_proj",
    )(x, nw, wd_bf, qnw, kvnw, wq_bf, cos_t, sin_t)


def _kv_up_kernel(ckv_ref, kpe_ref, wkv_ref, kcat_ref, v_ref):
    kv = _dot(ckv_ref[...].astype(BF16), wkv_ref[...])
    kp = kpe_ref[...].astype(BF16)
    hw = NOPE_DIM + V_DIM
    for hh in range(MLA_HEADS):
        kcat_ref[hh, :, 0:LANES] = kv[:, hh * hw:hh * hw + NOPE_DIM].astype(BF16)
        kcat_ref[hh, :, LANES:2 * LANES] = kp
        v_ref[hh] = kv[:, hh * hw + NOPE_DIM:(hh + 1) * hw].astype(BF16)


def _kv_up(ckv, kpe_pad, wkv_bf, tm):
    R = ckv.shape[0]
    return pl.pallas_call(
        _kv_up_kernel,
        grid=(R // tm,),
        in_specs=[
            pl.BlockSpec((tm, KV_LORA), lambda i: (i, 0)),
            pl.BlockSpec((tm, LANES), lambda i: (i, 0)),
            pl.BlockSpec(wkv_bf.shape, lambda i: (0, 0)),
        ],
        out_specs=[
            pl.BlockSpec((MLA_HEADS, tm, QK_PAD), lambda i: (0, i, 0)),
            pl.BlockSpec((MLA_HEADS, tm, V_DIM), lambda i: (0, i, 0)),
        ],
        out_shape=[
            jax.ShapeDtypeStruct((MLA_HEADS, R, QK_PAD), BF16),
            jax.ShapeDtypeStruct((MLA_HEADS, R, V_DIM), BF16),
        ],
        compiler_params=_cparams(("arbitrary",)),
        name="mla_kv_up_proj",
    )(ckv, kpe_pad, wkv_bf)


def _attn_prompt_kernel(q_ref, k_ref, v_ref, o_ref, *, L, tq, scale):
    row = lax.broadcasted_iota(jnp.int32, (tq, tq), 0)
    col = lax.broadcasted_iota(jnp.int32, (tq, tq), 1)
    sh = int(math.log2(CHUNK))
    dmask = (row >> sh) >= (col >> sh)

    c2 = scale * math.log2(math.e)
    def scores(qi):
        n = qi * tq
        q = q_ref[0, n:n + tq, :]
        s_d = jnp.where(dmask, _dot_nt(q, k_ref[0, n:n + tq, :]), -jnp.inf)
        s_p = _dot_nt(q, k_ref[0, 0:n, :]) if qi > 0 else None
        return s_d, s_p

    nq = L // tq
    ahead = 2
    pending = [scores(t) for t in range(min(ahead, nq))]
    for qi in range(nq):
        n = qi * tq
        s_d, s_p = pending.pop(0)
        if qi + ahead < nq:
            pending.append(scores(qi + ahead))
        m = jnp.max(s_d, axis=-1, keepdims=True)
        if qi > 0:
            m = jnp.maximum(m, jnp.max(s_p, axis=-1, keepdims=True))
        p_d = jnp.exp2((s_d - m) * c2)
        l = jnp.sum(p_d, axis=-1, keepdims=True)
        acc = _dot(p_d.astype(BF16), v_ref[0, n:n + tq, :])
        if qi > 0:
            p_p = jnp.exp2((s_p - m) * c2)
            l = l + jnp.sum(p_p, axis=-1, keepdims=True)
            acc = acc + _dot(p_p.astype(BF16), v_ref[0, 0:n, :])
        o_ref[n:n + tq, :] = (acc / l).astype(BF16)


def _attn_prompt(qcat, kcat, v, B, L):
    H = MLA_HEADS
    scale = (NOPE_DIM + ROPE_DIM) ** -0.5
    return pl.pallas_call(
        functools.partial(_attn_prompt_kernel, L=L, tq=256, scale=scale),
        grid=(B, H),
        in_specs=[
            pl.BlockSpec((1, L, QK_PAD), lambda b, h: (h, b, 0)),
            pl.BlockSpec((1, L, QK_PAD), lambda b, h: (h, b, 0)),
            pl.BlockSpec((1, L, V_DIM), lambda b, h: (h, b, 0)),
        ],
        out_specs=pl.BlockSpec((L, V_DIM), lambda b, h: (b, h)),
        out_shape=jax.ShapeDtypeStruct((B * L, H * V_DIM), BF16),
        compiler_params=_cparams(("arbitrary", "arbitrary")),
        name="attn_prompt",
    )(qcat, kcat, v)


def _attn_sample_kernel(q_ref, cc_ref, pc_ref, cn_ref, pn_ref, wkv_ref, o_ref, *, past, scale):
    H, Lq, _ = q_ref.shape
    hw = NOPE_DIM + V_DIM
    qlat, qrope = [], []
    for hh in range(H):
        wk_h = wkv_ref[:, hh * hw:hh * hw + NOPE_DIM]
        qlat.append(_dot_nt(q_ref[hh, :, 0:NOPE_DIM], wk_h).astype(BF16))
        qrope.append(q_ref[hh, :, NOPE_DIM:NOPE_DIM + ROPE_DIM])
    qlat = jnp.concatenate(qlat, axis=0)
    qrope = jnp.concatenate(qrope, axis=0)
    cc = cc_ref[0].astype(BF16)
    cn = cn_ref[...].astype(BF16)
    s1 = _dot_nt(qlat, cc) + _dot_nt(qrope, pc_ref[0].astype(BF16))
    s2 = _dot_nt(qlat, cn) + _dot_nt(qrope, pn_ref[:, 0:ROPE_DIM].astype(BF16))
    sh = int(math.log2(CHUNK))

    def q_chunk(n):
        t = lax.broadcasted_iota(jnp.int32, (H, Lq, n), 1).reshape(H * Lq, n)
        return (t + past) >> sh

    kc1 = lax.broadcasted_iota(jnp.int32, s1.shape, 1) >> sh
    kc2 = (lax.broadcasted_iota(jnp.int32, s2.shape, 1) + past) >> sh
    s1 = jnp.where(kc1 <= q_chunk(s1.shape[1]), s1 * scale, -jnp.inf)
    s2 = jnp.where(kc2 <= q_chunk(s2.shape[1]), s2 * scale, -jnp.inf)
    m = jnp.maximum(jnp.max(s1, axis=-1, keepdims=True), jnp.max(s2, axis=-1, keepdims=True))
    p1 = jnp.exp(s1 - m)
    p2 = jnp.exp(s2 - m)
    l = jnp.sum(p1, axis=-1, keepdims=True) + jnp.sum(p2, axis=-1, keepdims=True)
    olat = ((_dot(p1.astype(BF16), cc) + _dot(p2.astype(BF16), cn)) / l).astype(BF16)
    for hh in range(H):
        sl = slice(hh * LANES, (hh + 1) * LANES)
        wv_h = wkv_ref[:, hh * hw + NOPE_DIM:(hh + 1) * hw]
        o_ref[:, sl] = _dot(olat[hh * Lq:(hh + 1) * Lq], wv_h).astype(BF16)


def _attn_sample(qcat, ckv_cache, kpe_cache, ckv_new, kpe_new, wkv_bf, B, L, past):
    H = MLA_HEADS
    scale = (NOPE_DIM + ROPE_DIM) ** -0.5
    return pl.pallas_call(
        functools.partial(_attn_sample_kernel, past=past, scale=scale),
        grid=(B,),
        in_specs=[
            pl.BlockSpec((H, L, QK_PAD), lambda b: (0, b, 0)),
            pl.BlockSpec((1, past, KV_LORA), lambda b: (b, 0, 0)),
            pl.BlockSpec((1, past, ROPE_DIM), lambda b: (b, 0, 0)),
            pl.BlockSpec((L, KV_LORA), lambda b: (b, 0)),
            pl.BlockSpec((L, LANES), lambda b: (b, 0)),
            pl.BlockSpec(wkv_bf.shape, lambda b: (0, 0)),
        ],
        out_specs=pl.BlockSpec((L, H * V_DIM), lambda b: (b, 0)),
        out_shape=jax.ShapeDtypeStruct((B * L, H * V_DIM), BF16),
        compiler_params=_cparams(("arbitrary",)),
        name="attn_sample",
    )(qcat, ckv_cache, kpe_cache, ckv_new, kpe_new, wkv_bf)


def _swap_halves(w):
    half = w.shape[-1] // 2
    return jnp.concatenate([w[..., half:], w[..., :half]], axis=-1)


def _pad_lanes(w):
    return jnp.pad(w, [(0, 0)] * (w.ndim - 1) + [(0, LANES - w.shape[-1])])


def _prep_mla_weights(w_down, w_uq, w_ukv):
    o = Q_LORA + KV_LORA
    wpe = w_down[:, o:]
    wd = jnp.concatenate([w_down[:, :o], _pad_lanes(wpe), _pad_lanes(_swap_halves(wpe))], axis=-1)
    wq3 = w_uq.reshape(Q_LORA, MLA_HEADS, NOPE_DIM + ROPE_DIM)
    wn = wq3[..., :NOPE_DIM].reshape(Q_LORA, -1)
    wr = wq3[..., NOPE_DIM:]
    wrp = _pad_lanes(wr).reshape(Q_LORA, -1)
    wrs = _pad_lanes(_swap_halves(wr)).reshape(Q_LORA, -1)
    wq = jnp.concatenate([wn, wrp, wrs], axis=-1)
    return wd.astype(BF16), wq.astype(BF16), w_ukv.astype(BF16)


def _trunk(x, B, L, tm, tm_ffn, hg_state, ckv_cache, kpe_cache, conv_state, W, cos_t, sin_t, n_tab_blk):
    norm_w = W["norm_w"]
    nrow = lambda l, k: norm_w[l, k].reshape(1, -1)
    past = 0 if ckv_cache is None else ckv_cache.shape[1]

    p = _hgrn_in(x, nrow(0, 0), W["hgrn_w_in"], W["lb_logits"], 0, tm)
    z, hg_new = _hgrn_rec(p, W["hgrn_gnorm_w"].reshape(1, -1), hg_state, B, L)
    x = _out_proj(z, W["hgrn_w_out"], x, nrow(0, 1), tm)
    x, conv0 = _ffn(x, nrow(0, 2), W["ffn_w_in"], W["ffn_conv_w"][0], W["ffn_conv_b"][0],
                    W["ffn_w_out"], nrow(0, 3), None if conv_state is None else conv_state[0],
                    B, L, tm_ffn, 0)

    qcat, ckv, kpe_pad = _mla_in(x, nrow(1, 0), W["mla_wd"], W["mla_q_norm_w"].reshape(1, -1),
                                 W["mla_kv_norm_w"].reshape(1, -1), W["mla_wq"], cos_t, sin_t,
                                 min(tm, 256), n_tab_blk)
    if ckv_cache is None:
        kcat_n, v_n = _kv_up(ckv, kpe_pad, W["mla_wkv"], min(tm, 256))
        o = _attn_prompt(qcat, kcat_n, v_n, B, L)
    else:
        o = _attn_sample(qcat, ckv_cache, kpe_cache, ckv, kpe_pad, W["mla_wkv"], B, L, past)
    x = _out_proj(o, W["mla_w_out"], x, nrow(1, 1), tm)
    x, conv1 = _ffn(x, nrow(1, 2), W["ffn_w_in"], W["ffn_conv_w"][1], W["ffn_conv_b"][1],
                    W["ffn_w_out"], nrow(1, 3), None if conv_state is None else conv_state[1],
                    B, L, tm_ffn, 1)

    D = x.shape[-1]
    return (x.reshape(B, L, D), hg_new[None], ckv.reshape(1, B, L, KV_LORA),
            kpe_pad[:, :ROPE_DIM].reshape(1, B, L, ROPE_DIM), jnp.stack([conv0, conv1]))


def kernel(x_prompt, x_sample, state_hgrn, cache_ckv, cache_kpe, state_conv, norm_w, lb_logits,
           hgrn_w_in, hgrn_gnorm_w, hgrn_w_out, mla_w_down, mla_q_norm_w, mla_kv_norm_w,
           mla_w_uq, mla_w_ukv, mla_w_out, ffn_w_in, ffn_conv_w, ffn_conv_b, ffn_w_out):
    wd, wq, wkv = _prep_mla_weights(mla_w_down[0], mla_w_uq[0], mla_w_ukv[0])
    W = dict(
        norm_w=norm_w, lb_logits=lb_logits,
        hgrn_w_in=hgrn_w_in[0].astype(BF16), hgrn_gnorm_w=hgrn_gnorm_w[0],
        hgrn_w_out=hgrn_w_out[0].astype(BF16),
        mla_wd=wd, mla_wq=wq, mla_wkv=wkv,
        mla_q_norm_w=mla_q_norm_w[0], mla_kv_norm_w=mla_kv_norm_w[0],
        mla_w_out=mla_w_out[0].astype(BF16),
        ffn_w_in=ffn_w_in.astype(BF16), ffn_conv_w=ffn_conv_w,
        ffn_conv_b=ffn_conv_b, ffn_w_out=ffn_w_out.astype(BF16),
    )
    Bp, Lp, D = x_prompt.shape
    Bs, Ls, _ = x_sample.shape
    past = cache_ckv.shape[2]
    cos_t, sin_t = _rope_tables(max(Lp, past + Ls))

    yp, hgp, ckvp, kpep, convp = _trunk(
        x_prompt.reshape(Bp * Lp, D), Bp, Lp, 512, 1024, None, None, None, None, W,
        cos_t, sin_t, Lp // 256)

    cos_s = jnp.tile(cos_t[past:past + Ls], (Bs, 1))
    sin_s = jnp.tile(sin_t[past:past + Ls], (Bs, 1))
    ys, hgs, ckvs, kpes, convs = _trunk(
        x_sample.reshape(Bs * Ls, D), Bs, Ls, Bs * Ls, Bs * Ls, state_hgrn[0], cache_ckv[0], cache_kpe[0],
        state_conv, W, cos_s, sin_s, 1)
    return (yp, ys, hgp, hgs, ckvp, ckvs, kpep, kpes, convp, convs)
```

```python
import functools
import math

import jax
import jax.numpy as jnp
from jax import lax
from jax.experimental import pallas as pl
from jax.experimental.pallas import tpu as pltpu

F32 = jnp.float32
BF16 = jnp.bfloat16

EPS = 1e-6
LANES = 128
CHUNK = 64
HG_HEADS = 16
HG_DK = 128
HG_DV = 128
MLA_HEADS = 16
Q_LORA = 512
KV_LORA = 512
NOPE_DIM = 128
ROPE_DIM = 64
V_DIM = 128
ROPE_BASE = 10000.0
D_FF = 5632
FF_TILE = 512
FF_ROWS = 1024
QK_PAD = 256
VMEM_LIMIT = 56 * 1024 * 1024


def _cparams(sem):
    return pltpu.CompilerParams(dimension_semantics=sem, vmem_limit_bytes=VMEM_LIMIT)


def _rms(x, w):
    ms = jnp.mean(x * x, axis=-1, keepdims=True)
    return x * lax.rsqrt(ms + EPS) * w


def _sigmoid(x):
    return 0.5 * jnp.tanh(0.5 * x) + 0.5


def _dot(a, b):
    return jnp.dot(a, b, preferred_element_type=F32)


def _dot_nt(a, b):
    return lax.dot_general(a, b, (((1,), (1,)), ((), ())), preferred_element_type=F32)


def _dot_tn(a, b):
    return lax.dot_general(a, b, (((0,), (0,)), ((), ())), preferred_element_type=F32)


def _hgrn_in_kernel(x_ref, nw_ref, w_ref, lb_ref, o_ref, h_ref, *, lb_row, blk_per_sec):
    j = pl.program_id(1)

    @pl.when(j == 0)
    def _():
        h_ref[...] = _rms(x_ref[...], nw_ref[...]).astype(BF16)

    r = _dot(h_ref[...], w_ref[...])
    sec = j // blk_per_sec
    heads = o_ref.shape[0]

    def store(val):
        for hh in range(heads):
            o_ref[hh] = val[:, hh * LANES:(hh + 1) * LANES]

    @pl.when((sec == 0) | (sec == 3))
    def _():
        store(r * _sigmoid(r))

    @pl.when(sec == 1)
    def _():
        lg = lb_ref[...]
        e = jnp.exp(lg - jnp.max(lg, axis=0, keepdims=True))
        sm = e / jnp.sum(e, axis=0, keepdims=True)
        lb = jnp.sum(sm[0:lb_row + 1], axis=0, keepdims=True)
        store(lb + (1.0 - lb) * _sigmoid(r))

    @pl.when(sec == 2)
    def _():
        store(r)


def _hgrn_in(x, nw, w_bf, lb_logits, lb_row, tm):
    T, D = x.shape
    N = w_bf.shape[1]
    tn = 1024
    sec_w = N // 4
    blk_per_sec = sec_w // tn
    heads = tn // LANES
    return pl.pallas_call(
        functools.partial(_hgrn_in_kernel, lb_row=lb_row, blk_per_sec=blk_per_sec),
        grid=(T // tm, N // tn),
        in_specs=[
            pl.BlockSpec((tm, D), lambda i, j: (i, 0)),
            pl.BlockSpec((1, D), lambda i, j: (0, 0)),
            pl.BlockSpec((D, tn), lambda i, j: (0, j)),
            pl.BlockSpec((lb_logits.shape[0], tn), lambda i, j: (0, j % blk_per_sec)),
        ],
        out_specs=pl.BlockSpec((heads, tm, LANES), lambda i, j: (j, i, 0)),
        out_shape=jax.ShapeDtypeStruct((N // LANES, T, LANES), F32),
        scratch_shapes=[pltpu.VMEM((tm, D), BF16)],
        compiler_params=_cparams(("arbitrary", "arbitrary")),
        name="hgrn_in_proj",
    )(x, nw, w_bf, lb_logits)


SUB = 8


def _bcast_mid_in_tile(c, s, t8):
    if s == 4:
        return jnp.broadcast_to(c[s - 1:s, :], c.shape)
    dn1 = pltpu.roll(c, 1, 0)
    if s == 1:
        return jnp.where((t8 & 1) == 1, dn1, c)
    ph = t8 & 3
    up1 = pltpu.roll(c, SUB - 1, 0)
    dn2 = pltpu.roll(c, 2, 0)
    return jnp.where(ph == 0, up1, jnp.where(ph == 1, c, jnp.where(ph == 2, dn1, dn2)))


def _gla_masks(C):
    t8 = lax.broadcasted_iota(jnp.int32, (SUB, LANES), 0)
    rr = lax.broadcasted_iota(jnp.int32, (C, C), 0)
    cc = lax.broadcasted_iota(jnp.int32, (C, C), 1)
    pairs = []
    s = 1
    while s < C:
        sh = int(math.log2(2 * s))
        pairs.append(((rr >> sh) == (cc >> sh)) & ((rr & s) != 0) & ((cc & s) == 0))
        s *= 2
    return t8, pairs


def _gla_head(q, fg, v, masks):
    C = q.shape[0]
    nt = C // SUB
    t8, pairs = masks
    k = 1.0 - fg
    tiles = lambda x: [x[i * SUB:(i + 1) * SUB] for i in range(nt)]
    qs, ks = tiles(q), tiles(k)
    cs = [jnp.log2(f) for f in tiles(fg)]
    zero = jnp.zeros((SUB, LANES), F32)
    a = jnp.zeros((C, C), F32)
    s = 1
    for pair in pairs:
        if s < SUB:
            upper = (t8 & s) != 0
            qd, kd = [], []
            for i in range(nt):
                bc = _bcast_mid_in_tile(cs[i], s, t8)
                qd.append(qs[i] * jnp.exp2(cs[i]))
                kd.append(ks[i] * jnp.exp2(jnp.minimum(bc - cs[i], 0.0)))
                cs[i] = cs[i] + jnp.where(upper, bc, 0.0)
        else:
            m = s // SUB
            qd, kd = [zero] * nt, [zero] * nt
            for blk in range(0, nt, 2 * m):
                bc = jnp.broadcast_to(cs[blk + m - 1][SUB - 1:SUB, :], (SUB, LANES))
                for i in range(blk, blk + m):
                    kd[i] = ks[i] * jnp.exp2(bc - cs[i])
                for i in range(blk + m, blk + 2 * m):
                    qd[i] = qs[i] * jnp.exp2(cs[i])
                    cs[i] = cs[i] + bc
        a_s = _dot_nt(jnp.concatenate(qd, axis=0).astype(BF16),
                      jnp.concatenate(kd, axis=0).astype(BF16))
        a = jnp.where(pair, a_s, a)
        s *= 2
    b = jnp.concatenate(cs, axis=0)
    b_last = b[C - 1:C, :]
    diag = jnp.sum(q * k, axis=-1, keepdims=True)
    qb = (q * jnp.exp2(b)).astype(BF16)
    kdec = (k * jnp.exp2(b_last - b)).astype(BF16)
    return a.astype(BF16), qb, kdec, diag * v, jnp.exp2(b_last)


def _gla_tail(a, qb, kdec, dv, e_last, v_bf, st):
    o = _dot(a, v_bf) + dv + _dot_nt(qb, st.astype(BF16))
    st_new = st * e_last + _dot_tn(v_bf, kdec)
    return o, st_new


def _hgrn_rec_kernel(*refs, C, n_chunks, has_state):
    if has_state:
        (q_ref, f_ref, v_ref, g_ref, gw_ref, s0_ref, z_ref, so_ref,
         st_ref, a_buf, qb_buf, kd_buf, dv_buf, el_buf) = refs
    else:
        (q_ref, f_ref, v_ref, g_ref, gw_ref, z_ref, so_ref,
         st_ref, a_buf, qb_buf, kd_buf, dv_buf, el_buf) = refs
    li = pl.program_id(2)
    hb = st_ref.shape[0]

    @pl.when(li == 0)
    def _():
        for hh in range(hb):
            st_ref[hh] = s0_ref[0, hh].T if has_state else jnp.zeros((HG_DV, HG_DK), F32)

    gw = gw_ref[...]
    masks = _gla_masks(C)

    def rows_of(ci):
        return pl.ds(pl.multiple_of(ci * C, C), C)

    def head(ci, slot):
        rows = rows_of(ci)
        for hh in range(hb):
            a, qb, kdec, dv, e_last = _gla_head(q_ref[hh, rows, :], f_ref[hh, rows, :],
                                                v_ref[hh, rows, :], masks)
            a_buf[slot, hh] = a
            qb_buf[slot, hh] = qb
            kd_buf[slot, hh] = kdec
            dv_buf[slot, hh] = dv
            el_buf[slot, hh] = e_last

    def tail(ci, slot):
        rows = rows_of(ci)
        for hh in range(hb):
            o, st_new = _gla_tail(a_buf[slot, hh], qb_buf[slot, hh], kd_buf[slot, hh],
                                  dv_buf[slot, hh], el_buf[slot, hh],
                                  v_ref[hh, rows, :].astype(BF16), st_ref[hh])
            st_ref[hh] = st_new
            z_ref[rows, hh * LANES:(hh + 1) * LANES] = (
                _rms(o, gw) * g_ref[hh, rows, :]).astype(BF16)

    head(0, 0)
    if n_chunks > 1:
        assert n_chunks % 2 == 0

        def body(kk, carry):
            c0 = 2 * kk
            tail(c0, 0)
            head(c0 + 1, 1)
            tail(c0 + 1, 1)
            head(c0 + 2, 0)
            return carry

        lax.fori_loop(0, n_chunks // 2 - 1, body, 0)
        tail(n_chunks - 2, 0)
        head(n_chunks - 1, 1)
        tail(n_chunks - 1, 1)
    else:
        tail(0, 0)

    @pl.when(li == pl.num_programs(2) - 1)
    def _():
        for hh in range(hb):
            so_ref[0, hh] = st_ref[hh].T


def _hgrn_rec(p, gnorm_w, s0, B, L):
    H = HG_HEADS
    T = B * L
    C = min(CHUNK, L)
    hb = 4 if L > CHUNK else H
    lb = min(L, 1024)
    nl = L // lb
    ng = H // hb
    has_state = s0 is not None

    def sec_spec(sec):
        return pl.BlockSpec((hb, lb, LANES), lambda b, g, l: (sec * ng + g, b * nl + l, 0))

    in_specs = [sec_spec(0), sec_spec(1), sec_spec(2), sec_spec(3),
                pl.BlockSpec((1, HG_DV), lambda b, g, l: (0, 0))]
    args = [p, p, p, p, gnorm_w]
    if has_state:
        in_specs.append(pl.BlockSpec((1, hb, HG_DK, HG_DV), lambda b, g, l: (b, g, 0, 0)))
        args.append(s0)
    return pl.pallas_call(
        functools.partial(_hgrn_rec_kernel, C=C, n_chunks=lb // C, has_state=has_state),
        grid=(B, ng, nl),
        in_specs=in_specs,
        out_specs=[
            pl.BlockSpec((lb, hb * LANES), lambda b, g, l: (b * nl + l, g)),
            pl.BlockSpec((1, hb, HG_DK, HG_DV), lambda b, g, l: (b, g, 0, 0)),
        ],
        out_shape=[
            jax.ShapeDtypeStruct((T, H * HG_DV), BF16),
            jax.ShapeDtypeStruct((B, H, HG_DK, HG_DV), F32),
        ],
        scratch_shapes=[
            pltpu.VMEM((hb, HG_DV, HG_DK), F32),
            pltpu.VMEM((2, hb, C, C), BF16),
            pltpu.VMEM((2, hb, C, HG_DK), BF16),
            pltpu.VMEM((2, hb, C, HG_DK), BF16),
            pltpu.VMEM((2, hb, C, HG_DV), F32),
            pltpu.VMEM((2, hb, 1, HG_DK), F32),
        ],
        compiler_params=_cparams(("arbitrary", "arbitrary", "arbitrary")),
        name="hgrn_recurrence",
    )(*args)


def _out_proj_kernel(a_ref, w_ref, x_ref, nw_ref, o_ref):
    y = _dot(a_ref[...], w_ref[...])
    o_ref[...] = x_ref[...] + _rms(y, nw_ref[...])


def _out_proj(a_bf, w_bf, x, nw, tm):
    T, K = a_bf.shape
    N = w_bf.shape[1]
    return pl.pallas_call(
        _out_proj_kernel,
        grid=(T // tm,),
        in_specs=[
            pl.BlockSpec((tm, K), lambda i: (i, 0)),
            pl.BlockSpec((K, N), lambda i: (0, 0)),
            pl.BlockSpec((tm, N), lambda i: (i, 0)),
            pl.BlockSpec((1, N), lambda i: (0, 0)),
        ],
        out_specs=pl.BlockSpec((tm, N), lambda i: (i, 0)),
        out_shape=jax.ShapeDtypeStruct((T, N), F32),
        compiler_params=_cparams(("arbitrary",)),
        name="mixer_out_proj",
    )(a_bf, w_bf, x, nw)


def _ffn_kernel(*refs, ns, ls, tps, nj, has_state):
    if has_state:
        (x_ref, nw_ref, wg_ref, wu_ref, cwg_ref, cwu_ref, cbg_ref, cbu_ref, wo_ref, nw2_ref,
         sg_ref, su_ref, o_ref, csg_ref, csu_ref, h_ref, work_ref, carry_ref, act_ref) = refs
    else:
        (x_ref, nw_ref, wg_ref, wu_ref, cwg_ref, cwu_ref, cbg_ref, cbu_ref, wo_ref, nw2_ref,
         o_ref, csg_ref, csu_ref, h_ref, work_ref, carry_ref, act_ref) = refs
        sg_ref = su_ref = None
    i = pl.program_id(0)
    j = pl.program_id(1)
    tf = wg_ref.shape[1]

    def conv(half, r0, rn, w_ref, cw_ref, cb_ref, s_ref, cs_ref):
        hrows = h_ref[r0:r0 + rn, :] if ns == 1 else h_ref[...]
        u3 = _dot(hrows, w_ref[...]).reshape(ns, rn, tf)
        work_ref[half, :, 8 + r0:8 + r0 + rn, :] = u3
        if r0 == 0:
            prev = jnp.zeros((ns, 2, tf), F32) if s_ref is None else s_ref[...]
            if tps > 1:
                prev = jnp.where((i % tps) == 0, prev, carry_ref[j, half])
            work_ref[half, :, 6:8, :] = prev
        x1 = work_ref[half, :, 7 + r0:7 + r0 + rn, :]
        x2 = work_ref[half, :, 6 + r0:6 + r0 + rn, :]
        cw = cw_ref[...]
        c = cb_ref[...] + cw[0:1] * x2 + cw[1:2] * x1 + cw[2:3] * u3
        if r0 + rn == ls:
            tail = u3[:, rn - 2:rn, :]
            cs_ref[...] = tail
            if tps > 1:
                carry_ref[j, half] = tail
        return c.reshape(ns * rn, tf)

    def up_conv_act(slot):
        rn = min(ls, FF_ROWS) if ns == 1 else ls
        for r0 in range(0, ls, rn):
            cg = conv(0, r0, rn, wg_ref, cwg_ref, cbg_ref, sg_ref, csg_ref)
            cu = conv(1, r0, rn, wu_ref, cwu_ref, cbu_ref, su_ref, csu_ref)
            act_ref[slot, ns * r0:ns * (r0 + rn), :] = (cg * _sigmoid(cg) * cu).astype(BF16)

    @pl.when(j == 0)
    def _():
        h_ref[...] = _rms(x_ref[...], nw_ref[...]).astype(BF16)
        o_ref[...] = jnp.zeros_like(o_ref)
        up_conv_act(0)

    for par in range(2):
        @pl.when((j > 0) & (j < nj) & (j % 2 == par))
        def _():
            o_ref[...] += _dot(act_ref[1 - par], wo_ref[...])
            up_conv_act(par)

    @pl.when(j == nj)
    def _():
        y = o_ref[...] + _dot(act_ref[(nj - 1) % 2], wo_ref[...])
        o_ref[...] = x_ref[...] + _rms(y, nw2_ref[...])


def _ffn(x, nw, w_in_bf, conv_w, conv_b, w_out_bf, nw2, state, n_streams, L, tm, layer):
    T, D = x.shape
    tf = FF_TILE
    nj = D_FF // tf
    if tm >= L:
        assert tm % L == 0
        ns, ls, tps = tm // L, L, 1
    else:
        assert L % tm == 0
        ns, ls, tps = 1, tm, L // tm
    has_state = state is not None
    cb2 = conv_b.reshape(1, 2 * D_FF)

    def stream_blk(i):
        return (i * tm) // L // ns if ns > 1 else (i * tm) // L

    up = lambda j: jnp.minimum(j, nj - 1)
    dn = lambda j: jnp.maximum(j - 1, 0)
    in_specs = [
        pl.BlockSpec((tm, D), lambda i, j: (i, 0), pipeline_mode=pl.Buffered(1)),
        pl.BlockSpec((1, D), lambda i, j: (0, 0)),
        pl.BlockSpec((None, D, tf), lambda i, j: (layer, 0, up(j))),
        pl.BlockSpec((None, D, tf), lambda i, j: (layer, 0, nj + up(j))),
        pl.BlockSpec((3, tf), lambda i, j: (0, up(j))),
        pl.BlockSpec((3, tf), lambda i, j: (0, nj + up(j))),
        pl.BlockSpec((1, tf), lambda i, j: (0, up(j))),
        pl.BlockSpec((1, tf), lambda i, j: (0, nj + up(j))),
        pl.BlockSpec((None, tf, D), lambda i, j: (layer, dn(j), 0)),
        pl.BlockSpec((1, D), lambda i, j: (0, 0)),
    ]
    args = [x, nw, w_in_bf, w_in_bf, conv_w, conv_w, cb2, cb2, w_out_bf, nw2]
    if has_state:
        in_specs += [
            pl.BlockSpec((ns, 2, tf), lambda i, j: (stream_blk(i), 0, up(j))),
            pl.BlockSpec((ns, 2, tf), lambda i, j: (stream_blk(i), 0, nj + up(j))),
        ]
        args += [state, state]
    out, csg, csu = pl.pallas_call(
        functools.partial(_ffn_kernel, ns=ns, ls=ls, tps=tps, nj=nj, has_state=has_state),
        grid=(T // tm, nj + 1),
        in_specs=in_specs,
        out_specs=[
            pl.BlockSpec((tm, D), lambda i, j: (i, 0)),
            pl.BlockSpec((ns, 2, tf), lambda i, j: (i, 0, up(j))),
            pl.BlockSpec((ns, 2, tf), lambda i, j: (i, 0, up(j))),
        ],
        out_shape=[
            jax.ShapeDtypeStruct((T, D), F32),
            jax.ShapeDtypeStruct((T // ls, 2, D_FF), F32),
            jax.ShapeDtypeStruct((T // ls, 2, D_FF), F32),
        ],
        scratch_shapes=[
            pltpu.VMEM((tm, D), BF16),
            pltpu.VMEM((2, ns, 8 + ls, tf), F32),
            pltpu.VMEM((nj, 2, ns, 2, tf), F32),
            pltpu.VMEM((2, tm, tf), BF16),
        ],
        compiler_params=_cparams(("arbitrary", "arbitrary")),
        name="conv_ffn",
    )(*args)
    tails = jnp.concatenate([csg, csu], axis=-1).reshape(n_streams, tps, 2, 2 * D_FF)
    return out, tails[:, tps - 1]


def _rope_table_kernel(c_ref, s_ref):
    shape = c_ref.shape
    half = ROPE_DIM // 2
    pos = lax.broadcasted_iota(jnp.int32, shape, 0).astype(F32)
    lane = lax.broadcasted_iota(jnp.int32, shape, 1)
    fi = (lane & (half - 1)).astype(F32)
    inv = jnp.exp(fi * (-math.log(ROPE_BASE) / half))
    ang = pos * inv
    valid = lane < ROPE_DIM
    c_ref[...] = jnp.where(valid, jnp.cos(ang), 0.0)
    s_ref[...] = jnp.where(valid, jnp.where(lane < half, -jnp.sin(ang), jnp.sin(ang)), 0.0)


def _rope_tables(n_pos):
    return pl.pallas_call(
        _rope_table_kernel,
        out_shape=[jax.ShapeDtypeStruct((n_pos, LANES), F32)] * 2,
        name="rope_tables",
    )()


def _mla_in_kernel(x_ref, nw_ref, wd_ref, qnw_ref, kvnw_ref, wq_ref, c_ref, s_ref,
                   qcat_ref, ckv_ref, kpe_ref):
    h = _rms(x_ref[...], nw_ref[...]).astype(BF16)
    d = _dot(h, wd_ref[...])
    cq = _rms(d[:, :Q_LORA], qnw_ref[...]).astype(BF16)
    ckv_ref[...] = _rms(d[:, Q_LORA:Q_LORA + KV_LORA], kvnw_ref[...])
    cs = c_ref[...]
    sn = s_ref[...]
    o = Q_LORA + KV_LORA
    kpe_ref[...] = d[:, o:o + LANES] * cs + d[:, o + LANES:o + 2 * LANES] * sn
    hw = MLA_HEADS * LANES
    qn = _dot(cq, wq_ref[:, 0:hw])
    pr = _dot(cq, wq_ref[:, hw:2 * hw])
    ps = _dot(cq, wq_ref[:, 2 * hw:3 * hw])
    for hh in range(MLA_HEADS):
        sl = slice(hh * LANES, (hh + 1) * LANES)
        qcat_ref[hh, :, 0:LANES] = qn[:, sl].astype(BF16)
        qcat_ref[hh, :, LANES:2 * LANES] = (pr[:, sl] * cs + ps[:, sl] * sn).astype(BF16)


def _mla_in(x, nw, wd_bf, qnw, kvnw, wq_bf, cos_t, sin_t, tm, n_tab_blk):
    T, D = x.shape
    return pl.pallas_call(
        _mla_in_kernel,
        grid=(T // tm,),
        in_specs=[
            pl.BlockSpec((tm, D), lambda i: (i, 0)),
            pl.BlockSpec((1, D), lambda i: (0, 0)),
            pl.BlockSpec(wd_bf.shape, lambda i: (0, 0)),
            pl.BlockSpec((1, Q_LORA), lambda i: (0, 0)),
            pl.BlockSpec((1, KV_LORA), lambda i: (0, 0)),
            pl.BlockSpec(wq_bf.shape, lambda i: (0, 0)),
            pl.BlockSpec((tm, LANES), lambda i: (i % n_tab_blk, 0)),
            pl.BlockSpec((tm, LANES), lambda i: (i % n_tab_blk, 0)),
        ],
        out_specs=[
            pl.BlockSpec((MLA_HEADS, tm, QK_PAD), lambda i: (0, i, 0)),
            pl.BlockSpec((tm, KV_LORA), lambda i: (i, 0)),
            pl.BlockSpec((tm, LANES), lambda i: (i, 0)),
        ],
        out_shape=[
            jax.ShapeDtypeStruct((MLA_HEADS, T, QK_PAD), BF16),
            jax.ShapeDtypeStruct((T, KV_LORA), F32),
            jax.ShapeDtypeStruct((T, LANES), F32),
        ],
        compiler_params=_cparams(("arbitrary",)),
        name="mla_in_proj",
    )(x, nw, wd_bf, qnw, kvnw, wq_bf, cos_t, sin_t)


def _kv_up_kernel(ckv_ref, kpe_ref, wkv_ref, kcat_ref, v_ref):
    kv = _dot(ckv_ref[...].astype(BF16), wkv_ref[...])
    kp = kpe_ref[...].astype(BF16)
    hw = NOPE_DIM + V_DIM
    for hh in range(MLA_HEADS):
        kcat_ref[hh, :, 0:LANES] = kv[:, hh * hw:hh * hw + NOPE_DIM].astype(BF16)
        kcat_ref[hh, :, LANES:2 * LANES] = kp
        v_ref[hh] = kv[:, hh * hw + NOPE_DIM:(hh + 1) * hw].astype(BF16)


def _kv_up(ckv, kpe_pad, wkv_bf, tm):
    R = ckv.shape[0]
    return pl.pallas_call(
        _kv_up_kernel,
        grid=(R // tm,),
        in_specs=[
            pl.BlockSpec((tm, KV_LORA), lambda i: (i, 0)),
            pl.BlockSpec((tm, LANES), lambda i: (i, 0)),
            pl.BlockSpec(wkv_bf.shape, lambda i: (0, 0)),
        ],
        out_specs=[
            pl.BlockSpec((MLA_HEADS, tm, QK_PAD), lambda i: (0, i, 0)),
            pl.BlockSpec((MLA_HEADS, tm, V_DIM), lambda i: (0, i, 0)),
        ],
        out_shape=[
            jax.ShapeDtypeStruct((MLA_HEADS, R, QK_PAD), BF16),
            jax.ShapeDtypeStruct((MLA_HEADS, R, V_DIM), BF16),
        ],
        compiler_params=_cparams(("arbitrary",)),
        name="mla_kv_up_proj",
    )(ckv, kpe_pad, wkv_bf)


def _attn_prompt_kernel(q_ref, k_ref, v_ref, o_ref, *, L, tq, scale):
    row = lax.broadcasted_iota(jnp.int32, (tq, tq), 0)
    col = lax.broadcasted_iota(jnp.int32, (tq, tq), 1)
    sh = int(math.log2(CHUNK))
    dmask = (row >> sh) >= (col >> sh)

    c2 = scale * math.log2(math.e)
    def scores(qi):
        n = qi * tq
        q = q_ref[0, n:n + tq, :]
        s_d = jnp.where(dmask, _dot_nt(q, k_ref[0, n:n + tq, :]), -jnp.inf)
        s_p = _dot_nt(q, k_ref[0, 0:n, :]) if qi > 0 else None
        return s_d, s_p

    nq = L // tq
    ahead = 2
    pending = [scores(t) for t in range(min(ahead, nq))]
    for qi in range(nq):
        n = qi * tq
        s_d, s_p = pending.pop(0)
        if qi + ahead < nq:
            pending.append(scores(qi + ahead))
        m = jnp.max(s_d, axis=-1, keepdims=True)
        if qi > 0:
            m = jnp.maximum(m, jnp.max(s_p, axis=-1, keepdims=True))
        p_d = jnp.exp2((s_d - m) * c2)
        l = jnp.sum(p_d, axis=-1, keepdims=True)
        acc = _dot(p_d.astype(BF16), v_ref[0, n:n + tq, :])
        if qi > 0:
            p_p = jnp.exp2((s_p - m) * c2)
            l = l + jnp.sum(p_p, axis=-1, keepdims=True)
            acc = acc + _dot(p_p.astype(BF16), v_ref[0, 0:n, :])
        o_ref[n:n + tq, :] = (acc / l).astype(BF16)


def _attn_prompt(qcat, kcat, v, B, L):
    H = MLA_HEADS
    scale = (NOPE_DIM + ROPE_DIM) ** -0.5
    return pl.pallas_call(
        functools.partial(_attn_prompt_kernel, L=L, tq=256, scale=scale),
        grid=(B, H),
        in_specs=[
            pl.BlockSpec((1, L, QK_PAD), lambda b, h: (h, b, 0)),
            pl.BlockSpec((1, L, QK_PAD), lambda b, h: (h, b, 0)),
            pl.BlockSpec((1, L, V_DIM), lambda b, h: (h, b, 0)),
        ],
        out_specs=pl.BlockSpec((L, V_DIM), lambda b, h: (b, h)),
        out_shape=jax.ShapeDtypeStruct((B * L, H * V_DIM), BF16),
        compiler_params=_cparams(("arbitrary", "arbitrary")),
        name="attn_prompt",
    )(qcat, kcat, v)


def _attn_sample_kernel(q_ref, cc_ref, pc_ref, cn_ref, pn_ref, wkv_ref, o_ref, *, past, scale):
    H, Lq, _ = q_ref.shape
    hw = NOPE_DIM + V_DIM
    qlat, qrope = [], []
    for hh in range(H):
        wk_h = wkv_ref[:, hh * hw:hh * hw + NOPE_DIM]
        qlat.append(_dot_nt(q_ref[hh, :, 0:NOPE_DIM], wk_h).astype(BF16))
        qrope.append(q_ref[hh, :, NOPE_DIM:NOPE_DIM + ROPE_DIM])
    qlat = jnp.concatenate(qlat, axis=0)
    qrope = jnp.concatenate(qrope, axis=0)
    cc = cc_ref[0].astype(BF16)
    cn = cn_ref[...].astype(BF16)
    s1 = _dot_nt(qlat, cc) + _dot_nt(qrope, pc_ref[0].astype(BF16))
    s2 = _dot_nt(qlat, cn) + _dot_nt(qrope, pn_ref[:, 0:ROPE_DIM].astype(BF16))
    sh = int(math.log2(CHUNK))

    def q_chunk(n):
        t = lax.broadcasted_iota(jnp.int32, (H, Lq, n), 1).reshape(H * Lq, n)
        return (t + past) >> sh

    kc1 = lax.broadcasted_iota(jnp.int32, s1.shape, 1) >> sh
    kc2 = (lax.broadcasted_iota(jnp.int32, s2.shape, 1) + past) >> sh
    s1 = jnp.where(kc1 <= q_chunk(s1.shape[1]), s1 * scale, -jnp.inf)
    s2 = jnp.where(kc2 <= q_chunk(s2.shape[1]), s2 * scale, -jnp.inf)
    m = jnp.maximum(jnp.max(s1, axis=-1, keepdims=True), jnp.max(s2, axis=-1, keepdims=True))
    p1 = jnp.exp(s1 - m)
    p2 = jnp.exp(s2 - m)
    l = jnp.sum(p1, axis=-1, keepdims=True) + jnp.sum(p2, axis=-1, keepdims=True)
    olat = ((_dot(p1.astype(BF16), cc) + _dot(p2.astype(BF16), cn)) / l).astype(BF16)
    for hh in range(H):
        sl = slice(hh * LANES, (hh + 1) * LANES)
        wv_h = wkv_ref[:, hh * hw + NOPE_DIM:(hh + 1) * hw]
        o_ref[:, sl] = _dot(olat[hh * Lq:(hh + 1) * Lq], wv_h).astype(BF16)


def _attn_sample(qcat, ckv_cache, kpe_cache, ckv_new, kpe_new, wkv_bf, B, L, past):
    H = MLA_HEADS
    scale = (NOPE_DIM + ROPE_DIM) ** -0.5
    return pl.pallas_call(
        functools.partial(_attn_sample_kernel, past=past, scale=scale),
        grid=(B,),
        in_specs=[
            pl.BlockSpec((H, L, QK_PAD), lambda b: (0, b, 0)),
            pl.BlockSpec((1, past, KV_LORA), lambda b: (b, 0, 0)),
            pl.BlockSpec((1, past, ROPE_DIM), lambda b: (b, 0, 0)),
            pl.BlockSpec((L, KV_LORA), lambda b: (b, 0)),
            pl.BlockSpec((L, LANES), lambda b: (b, 0)),
            pl.BlockSpec(wkv_bf.shape, lambda b: (0, 0)),
        ],
        out_specs=pl.BlockSpec((L, H * V_DIM), lambda b: (b, 0)),
        out_shape=jax.ShapeDtypeStruct((B * L, H * V_DIM), BF16),
        compiler_params=_cparams(("arbitrary",)),
        name="attn_sample",
    )(qcat, ckv_cache, kpe_cache, ckv_new, kpe_new, wkv_bf)


def _swap_halves(w):
    half = w.shape[-1] // 2
    return jnp.concatenate([w[..., half:], w[..., :half]], axis=-1)


def _pad_lanes(w):
    return jnp.pad(w, [(0, 0)] * (w.ndim - 1) + [(0, LANES - w.shape[-1])])


def _prep_mla_weights(w_down, w_uq, w_ukv):
    o = Q_LORA + KV_LORA
    wpe = w_down[:, o:]
    wd = jnp.concatenate([w_down[:, :o], _pad_lanes(wpe), _pad_lanes(_swap_halves(wpe))], axis=-1)
    wq3 = w_uq.reshape(Q_LORA, MLA_HEADS, NOPE_DIM + ROPE_DIM)
    wn = wq3[..., :NOPE_DIM].reshape(Q_LORA, -1)
    wr = wq3[..., NOPE_DIM:]
    wrp = _pad_lanes(wr).reshape(Q_LORA, -1)
    wrs = _pad_lanes(_swap_halves(wr)).reshape(Q_LORA, -1)
    wq = jnp.concatenate([wn, wrp, wrs], axis=-1)
    return wd.astype(BF16), wq.astype(BF16), w_ukv.astype(BF16)


def _trunk(x, B, L, tm, tm_ffn, hg_state, ckv_cache, kpe_cache, conv_state, W, cos_t, sin_t, n_tab_blk):
    norm_w = W["norm_w"]
    nrow = lambda l, k: norm_w[l, k].reshape(1, -1)
    past = 0 if ckv_cache is None else ckv_cache.shape[1]

    p = _hgrn_in(x, nrow(0, 0), W["hgrn_w_in"], W["lb_logits"], 0, tm)
    z, hg_new = _hgrn_rec(p, W["hgrn_gnorm_w"].reshape(1, -1), hg_state, B, L)
    x = _out_proj(z, W["hgrn_w_out"], x, nrow(0, 1), tm)
    x, conv0 = _ffn(x, nrow(0, 2), W["ffn_w_in"], W["ffn_conv_w"][0], W["ffn_conv_b"][0],
                    W["ffn_w_out"], nrow(0, 3), None if conv_state is None else conv_state[0],
                    B, L, tm_ffn, 0)

    qcat, ckv, kpe_pad = _mla_in(x, nrow(1, 0), W["mla_wd"], W["mla_q_norm_w"].reshape(1, -1),
                                 W["mla_kv_norm_w"].reshape(1, -1), W["mla_wq"], cos_t, sin_t,
                                 min(tm, 256), n_tab_blk)
    if ckv_cache is None:
        kcat_n, v_n = _kv_up(ckv, kpe_pad, W["mla_wkv"], min(tm, 256))
        o = _attn_prompt(qcat, kcat_n, v_n, B, L)
    else:
        o = _attn_sample(qcat, ckv_cache, kpe_cache, ckv, kpe_pad, W["mla_wkv"], B, L, past)
    x = _out_proj(o, W["mla_w_out"], x, nrow(1, 1), tm)
    x, conv1 = _ffn(x, nrow(1, 2), W["ffn_w_in"], W["ffn_conv_w"][1], W["ffn_conv_b"][1],
                    W["ffn_w_out"], nrow(1, 3), None if conv_state is None else conv_state[1],
                    B, L, tm_ffn, 1)

    D = x.shape[-1]
    return (x.reshape(B, L, D), hg_new[None], ckv.reshape(1, B, L, KV_LORA),
            kpe_pad[:, :ROPE_DIM].reshape(1, B, L, ROPE_DIM), jnp.stack([conv0, conv1]))


def kernel(x_prompt, x_sample, state_hgrn, cache_ckv, cache_kpe, state_conv, norm_w, lb_logits,
           hgrn_w_in, hgrn_gnorm_w, hgrn_w_out, mla_w_down, mla_q_norm_w, mla_kv_norm_w,
           mla_w_uq, mla_w_ukv, mla_w_out, ffn_w_in, ffn_conv_w, ffn_conv_b, ffn_w_out):
    wd, wq, wkv = _prep_mla_weights(mla_w_down[0], mla_w_uq[0], mla_w_ukv[0])
    W = dict(
        norm_w=norm_w, lb_logits=lb_logits,
        hgrn_w_in=hgrn_w_in[0].astype(BF16), hgrn_gnorm_w=hgrn_gnorm_w[0],
        hgrn_w_out=hgrn_w_out[0].astype(BF16),
        mla_wd=wd, mla_wq=wq, mla_wkv=wkv,
        mla_q_norm_w=mla_q_norm_w[0], mla_kv_norm_w=mla_kv_norm_w[0],
        mla_w_out=mla_w_out[0].astype(BF16),
        ffn_w_in=ffn_w_in.astype(BF16), ffn_conv_w=ffn_conv_w,
        ffn_conv_b=ffn_conv_b, ffn_w_out=ffn_w_out.astype(BF16),
    )
    Bp, Lp, D = x_prompt.shape
    Bs, Ls, _ = x_sample.shape
    past = cache_ckv.shape[2]
    cos_t, sin_t = _rope_tables(max(Lp, past + Ls))

    yp, hgp, ckvp, kpep, convp = _trunk(
        x_prompt.reshape(Bp * Lp, D), Bp, Lp, 512, 1024, None, None, None, None, W,
        cos_t, sin_t, Lp // 256)

    cos_s = jnp.tile(cos_t[past:past + Ls], (Bs, 1))
    sin_s = jnp.tile(sin_t[past:past + Ls], (Bs, 1))
    ys, hgs, ckvs, kpes, convs = _trunk(
        x_sample.reshape(Bs * Ls, D), Bs, Ls, Bs * Ls, Bs * Ls, state_hgrn[0], cache_ckv[0], cache_kpe[0],
        state_conv, W, cos_s, sin_s, 1)
    return (yp, ys, hgp, hgs, ckvp, ckvs, kpep, kpes, convp, convs)
```

```python
import functools
import math

import jax
import jax.numpy as jnp
from jax import lax
from jax.experimental import pallas as pl
from jax.experimental.pallas import tpu as pltpu

F32 = jnp.float32
BF16 = jnp.bfloat16

EPS = 1e-6
LANES = 128
CHUNK = 64
HG_HEADS = 16
HG_DK = 128
HG_DV = 128
MLA_HEADS = 16
Q_LORA = 512
KV_LORA = 512
NOPE_DIM = 128
ROPE_DIM = 64
V_DIM = 128
ROPE_BASE = 10000.0
D_FF = 5632
FF_TILE = 512
TOKEN_TILE = 512
FFN_TOKEN_TILE = 1024
MLA_TOKEN_TILE = 256
IN_PROJ_COLS = 2048
ATTN_Q_TILE = 256
REC_HEADS = 4
REC_ROWS = 1024
SCORES_AHEAD = 2
QK_PAD = 256
VMEM_LIMIT = 56 * 1024 * 1024


def _cparams(sem):
    return pltpu.CompilerParams(dimension_semantics=sem, vmem_limit_bytes=VMEM_LIMIT)


def _rms(x, w):
    ms = jnp.mean(x * x, axis=-1, keepdims=True)
    return x * lax.rsqrt(ms + EPS) * w


def _sigmoid(x):
    return 0.5 * jnp.tanh(0.5 * x) + 0.5


def _dot(a, b):
    return jnp.dot(a, b, preferred_element_type=F32)


def _dot_nt(a, b):
    return lax.dot_general(a, b, (((1,), (1,)), ((), ())), preferred_element_type=F32)


def _dot_tn(a, b):
    return lax.dot_general(a, b, (((0,), (0,)), ((), ())), preferred_element_type=F32)


def _side_cast_specs(srcs, layer, n_steps, step):
    in_specs, out_specs, out_shapes = [], [], []
    for w in srcs:
        _, R, C = w.shape
        rows = R // n_steps
        assert rows * n_steps == R and rows % 16 == 0, (R, n_steps)
        in_specs.append(pl.BlockSpec((None, rows, C), lambda *g: (layer, step(*g), 0)))
        out_specs.append(pl.BlockSpec((rows, C), lambda *g: (step(*g), 0)))
        out_shapes.append(jax.ShapeDtypeStruct((R, C), BF16))
    return in_specs, out_specs, out_shapes


def _side_cast(src_refs, dst_refs):
    for src, dst in zip(src_refs, dst_refs):
        dst[...] = src[...].astype(BF16)


def _hgrn_in_kernel(*refs, lb_row, blk_per_sec, n_side):
    x_ref, nw_ref, w_ref, lb_ref = refs[:4]
    o_ref, h_ref = refs[4 + n_side], refs[-1]
    _side_cast(refs[4:4 + n_side], refs[5 + n_side:5 + 2 * n_side])
    j = pl.program_id(1)

    @pl.when(j == 0)
    def _():
        h_ref[...] = _rms(x_ref[...], nw_ref[...]).astype(BF16)

    sec = j // blk_per_sec
    heads = o_ref.shape[0]
    lg = lb_ref[...]
    e = jnp.exp(lg - jnp.max(lg, axis=0, keepdims=True))
    sm = e / jnp.sum(e, axis=0, keepdims=True)
    lb_all = jnp.sum(sm[0:lb_row + 1], axis=0, keepdims=True)
    hp = heads // 2
    for c0 in (0, hp):
        cols = slice(c0 * LANES, (c0 + hp) * LANES)
        r = _dot(h_ref[...], w_ref[:, cols])
        lb = lb_all[:, cols]
        sig = _sigmoid(r)
        val = jnp.where(sec == 2, r, jnp.where(sec == 1, lb + (1.0 - lb) * sig, r * sig))
        for hh in range(hp):
            o_ref[c0 + hh] = val[:, hh * LANES:(hh + 1) * LANES]


def _hgrn_in(x, nw, w_bf, lb_logits, lb_row, tm, side=(), side_layer=0):
    T, D = x.shape
    N = w_bf.shape[1]
    tn = IN_PROJ_COLS
    sec_w = N // 4
    blk_per_sec = sec_w // tn
    heads = tn // LANES
    nj = N // tn
    s_in, s_out, s_shape = _side_cast_specs(side, side_layer, (T // tm) * nj,
                                            lambda i, j: i * nj + j)
    outs = pl.pallas_call(
        functools.partial(_hgrn_in_kernel, lb_row=lb_row, blk_per_sec=blk_per_sec,
                          n_side=len(side)),
        grid=(T // tm, nj),
        in_specs=[
            pl.BlockSpec((tm, D), lambda i, j: (i, 0)),
            pl.BlockSpec((1, D), lambda i, j: (0, 0)),
            pl.BlockSpec((D, tn), lambda i, j: (0, j)),
            pl.BlockSpec((lb_logits.shape[0], tn), lambda i, j: (0, j % blk_per_sec)),
        ] + s_in,
        out_specs=[pl.BlockSpec((heads, tm, LANES), lambda i, j: (j, i, 0))] + s_out,
        out_shape=[jax.ShapeDtypeStruct((N // LANES, T, LANES), F32)] + s_shape,
        scratch_shapes=[pltpu.VMEM((tm, D), BF16)],
        compiler_params=_cparams(("arbitrary", "arbitrary")),
        name="hgrn_in_proj",
    )(x, nw, w_bf, lb_logits, *side)
    return outs[0], outs[1:]


SUB = 8


def _bcast_mid_in_tile(c, s, t8):
    if s == 4:
        return jnp.broadcast_to(c[s - 1:s, :], c.shape)
    dn1 = pltpu.roll(c, 1, 0)
    if s == 1:
        return jnp.where((t8 & 1) == 1, dn1, c)
    ph = t8 & 3
    up1 = pltpu.roll(c, SUB - 1, 0)
    dn2 = pltpu.roll(c, 2, 0)
    return jnp.where(ph == 0, up1, jnp.where(ph == 1, c, jnp.where(ph == 2, dn1, dn2)))


def _gla_masks(C):
    t8 = lax.broadcasted_iota(jnp.int32, (SUB, LANES), 0)
    rr = lax.broadcasted_iota(jnp.int32, (C, C), 0)
    cc = lax.broadcasted_iota(jnp.int32, (C, C), 1)
    pairs = []
    s = 1
    while s < C:
        sh = int(math.log2(2 * s))
        pairs.append(((rr >> sh) == (cc >> sh)) & ((rr & s) != 0) & ((cc & s) == 0))
        s *= 2
    return t8, pairs


def _gla_head(q, fg, v, masks):
    C = q.shape[0]
    nt = C // SUB
    t8, pairs = masks
    k = 1.0 - fg
    tiles = lambda x: [x[i * SUB:(i + 1) * SUB] for i in range(nt)]
    qs, ks = tiles(q), tiles(k)
    cs = [jnp.log2(f) for f in tiles(fg)]
    zero = jnp.zeros((SUB, LANES), F32)
    a = jnp.zeros((C, C), F32)
    s = 1
    for pair in pairs:
        if s < SUB:
            upper = (t8 & s) != 0
            qd, kd = [], []
            for i in range(nt):
                bc = _bcast_mid_in_tile(cs[i], s, t8)
                qd.append(qs[i] * jnp.exp2(cs[i]))
                kd.append(ks[i] * jnp.exp2(jnp.minimum(bc - cs[i], 0.0)))
                cs[i] = cs[i] + jnp.where(upper, bc, 0.0)
        else:
            m = s // SUB
            qd, kd = [zero] * nt, [zero] * nt
            for blk in range(0, nt, 2 * m):
                bc = jnp.broadcast_to(cs[blk + m - 1][SUB - 1:SUB, :], (SUB, LANES))
                for i in range(blk, blk + m):
                    kd[i] = ks[i] * jnp.exp2(bc - cs[i])
                for i in range(blk + m, blk + 2 * m):
                    qd[i] = qs[i] * jnp.exp2(cs[i])
                    cs[i] = cs[i] + bc
        a_s = _dot_nt(jnp.concatenate(qd, axis=0).astype(BF16),
                      jnp.concatenate(kd, axis=0).astype(BF16))
        a = jnp.where(pair, a_s, a)
        s *= 2
    b = jnp.concatenate(cs, axis=0)
    b_last = b[C - 1:C, :]
    diag = jnp.sum(q * k, axis=-1, keepdims=True)
    qb = (q * jnp.exp2(b)).astype(BF16)
    kdec = (k * jnp.exp2(b_last - b)).astype(BF16)
    return a.astype(BF16), qb, kdec, diag * v, jnp.exp2(b_last)


def _gla_tail(a, qb, kdec, dv, e_last, v_bf, st):
    o = _dot(a, v_bf) + dv + _dot_nt(qb, st.astype(BF16))
    st_new = st * e_last + _dot_tn(v_bf, kdec)
    return o, st_new


def _hgrn_rec_kernel(*refs, C, n_chunks, has_state):
    if has_state:
        (q_ref, f_ref, v_ref, g_ref, gw_ref, s0_ref, z_ref, so_ref,
         st_ref, a_buf, qb_buf, kd_buf, dv_buf, el_buf) = refs
    else:
        (q_ref, f_ref, v_ref, g_ref, gw_ref, z_ref, so_ref,
         st_ref, a_buf, qb_buf, kd_buf, dv_buf, el_buf) = refs
    li = pl.program_id(2)
    hb = st_ref.shape[0]

    @pl.when(li == 0)
    def _():
        for hh in range(hb):
            st_ref[hh] = s0_ref[0, hh].T if has_state else jnp.zeros((HG_DV, HG_DK), F32)

    gw = gw_ref[...]
    masks = _gla_masks(C)

    def rows_of(ci):
        return pl.ds(pl.multiple_of(ci * C, C), C)

    def head(ci, slot):
        rows = rows_of(ci)
        for hh in range(hb):
            a, qb, kdec, dv, e_last = _gla_head(q_ref[hh, rows, :], f_ref[hh, rows, :],
                                                v_ref[hh, rows, :], masks)
            a_buf[slot, hh] = a
            qb_buf[slot, hh] = qb
            kd_buf[slot, hh] = kdec
            dv_buf[slot, hh] = dv
            el_buf[slot, hh] = e_last

    def tail(ci, slot):
        rows = rows_of(ci)
        for hh in range(hb):
            o, st_new = _gla_tail(a_buf[slot, hh], qb_buf[slot, hh], kd_buf[slot, hh],
                                  dv_buf[slot, hh], el_buf[slot, hh],
                                  v_ref[hh, rows, :].astype(BF16), st_ref[hh])
            st_ref[hh] = st_new
            z_ref[rows, hh * LANES:(hh + 1) * LANES] = (
                _rms(o, gw) * g_ref[hh, rows, :]).astype(BF16)

    head(0, 0)
    if n_chunks > 1:
        assert n_chunks % 2 == 0

        def body(kk, carry):
            c0 = 2 * kk
            tail(c0, 0)
            head(c0 + 1, 1)
            tail(c0 + 1, 1)
            head(c0 + 2, 0)
            return carry

        lax.fori_loop(0, n_chunks // 2 - 1, body, 0)
        tail(n_chunks - 2, 0)
        head(n_chunks - 1, 1)
        tail(n_chunks - 1, 1)
    else:
        tail(0, 0)

    @pl.when(li == pl.num_programs(2) - 1)
    def _():
        for hh in range(hb):
            so_ref[0, hh] = st_ref[hh].T


def _hgrn_rec(p, gnorm_w, s0, B, L):
    H = HG_HEADS
    T = B * L
    C = min(CHUNK, L)
    hb = REC_HEADS if L > CHUNK else H
    lb = min(L, REC_ROWS)
    nl = L // lb
    ng = H // hb
    has_state = s0 is not None

    def sec_spec(sec):
        return pl.BlockSpec((hb, lb, LANES), lambda b, g, l: (sec * ng + g, b * nl + l, 0))

    in_specs = [sec_spec(0), sec_spec(1), sec_spec(2), sec_spec(3),
                pl.BlockSpec((1, HG_DV), lambda b, g, l: (0, 0))]
    args = [p, p, p, p, gnorm_w]
    if has_state:
        in_specs.append(pl.BlockSpec((1, hb, HG_DK, HG_DV), lambda b, g, l: (b, g, 0, 0)))
        args.append(s0)
    return pl.pallas_call(
        functools.partial(_hgrn_rec_kernel, C=C, n_chunks=lb // C, has_state=has_state),
        grid=(B, ng, nl),
        in_specs=in_specs,
        out_specs=[
            pl.BlockSpec((lb, hb * LANES), lambda b, g, l: (b * nl + l, g)),
            pl.BlockSpec((1, hb, HG_DK, HG_DV), lambda b, g, l: (b, g, 0, 0)),
        ],
        out_shape=[
            jax.ShapeDtypeStruct((T, H * HG_DV), BF16),
            jax.ShapeDtypeStruct((B, H, HG_DK, HG_DV), F32),
        ],
        scratch_shapes=[
            pltpu.VMEM((hb, HG_DV, HG_DK), F32),
            pltpu.VMEM((2, hb, C, C), BF16),
            pltpu.VMEM((2, hb, C, HG_DK), BF16),
            pltpu.VMEM((2, hb, C, HG_DK), BF16),
            pltpu.VMEM((2, hb, C, HG_DV), F32),
            pltpu.VMEM((2, hb, 1, HG_DK), F32),
        ],
        compiler_params=_cparams(("arbitrary", "arbitrary", "arbitrary")),
        name="hgrn_recurrence",
    )(*args)


def _out_proj_kernel(a_ref, w_ref, x_ref, nw_ref, o_ref):
    y = _dot(a_ref[...], w_ref[...])
    o_ref[...] = x_ref[...] + _rms(y, nw_ref[...])


def _out_proj(a_bf, w_bf, x, nw, tm):
    T, K = a_bf.shape
    N = w_bf.shape[1]
    return pl.pallas_call(
        _out_proj_kernel,
        grid=(T // tm,),
        in_specs=[
            pl.BlockSpec((tm, K), lambda i: (i, 0)),
            pl.BlockSpec((K, N), lambda i: (0, 0)),
            pl.BlockSpec((tm, N), lambda i: (i, 0)),
            pl.BlockSpec((1, N), lambda i: (0, 0)),
        ],
        out_specs=pl.BlockSpec((tm, N), lambda i: (i, 0)),
        out_shape=jax.ShapeDtypeStruct((T, N), F32),
        compiler_params=_cparams(("arbitrary",)),
        name="mixer_out_proj",
    )(a_bf, w_bf, x, nw)


def _ffn_kernel(*refs, ns, ls, tps, nj, has_state):
    if has_state:
        (x_ref, nw_ref, wg_ref, wu_ref, cwg_ref, cwu_ref, cbg_ref, cbu_ref, wo_ref, nw2_ref,
         sg_ref, su_ref, o_ref, csg_ref, csu_ref, h_ref, work_ref, carry_ref, act_ref) = refs
    else:
        (x_ref, nw_ref, wg_ref, wu_ref, cwg_ref, cwu_ref, cbg_ref, cbu_ref, wo_ref, nw2_ref,
         o_ref, csg_ref, csu_ref, h_ref, work_ref, carry_ref, act_ref) = refs
        sg_ref = su_ref = None
    i = pl.program_id(0)
    j = pl.program_id(1)
    tf = wg_ref.shape[1]

    def conv(half, w_ref, cw_ref, cb_ref, s_ref, cs_ref):
        u3 = _dot(h_ref[...], w_ref[...]).reshape(ns, ls, tf)
        work_ref[half, :, 8:8 + ls, :] = u3
        prev = jnp.zeros((ns, 2, tf), F32) if s_ref is None else s_ref[...]
        if tps > 1:
            prev = jnp.where((i % tps) == 0, prev, carry_ref[j, half])
        work_ref[half, :, 6:8, :] = prev
        x1 = work_ref[half, :, 7:7 + ls, :]
        x2 = work_ref[half, :, 6:6 + ls, :]
        cw = cw_ref[...]
        c = cb_ref[...] + cw[0:1] * x2 + cw[1:2] * x1 + cw[2:3] * u3
        tail = u3[:, ls - 2:ls, :]
        cs_ref[...] = tail
        if tps > 1:
            carry_ref[j, half] = tail
        return c.reshape(ns * ls, tf)

    def up_conv_act(slot):
        cg = conv(0, wg_ref, cwg_ref, cbg_ref, sg_ref, csg_ref)
        cu = conv(1, wu_ref, cwu_ref, cbu_ref, su_ref, csu_ref)
        act_ref[slot] = (cg * _sigmoid(cg) * cu).astype(BF16)

    @pl.when(j == 0)
    def _():
        h_ref[...] = _rms(x_ref[...], nw_ref[...]).astype(BF16)
        o_ref[...] = jnp.zeros_like(o_ref)
        up_conv_act(0)

    for par in range(2):
        @pl.when((j > 0) & (j < nj) & (j % 2 == par))
        def _():
            o_ref[...] += _dot(act_ref[1 - par], wo_ref[...])
            up_conv_act(par)

    @pl.when(j == nj)
    def _():
        y = o_ref[...] + _dot(act_ref[(nj - 1) % 2], wo_ref[...])
        o_ref[...] = x_ref[...] + _rms(y, nw2_ref[...])


def _ffn(x, nw, w_in_bf, conv_w, conv_b, w_out_bf, nw2, state, n_streams, L, tm):
    T, D = x.shape
    tf = FF_TILE
    nj = D_FF // tf
    if tm >= L:
        assert tm % L == 0
        ns, ls, tps = tm // L, L, 1
    else:
        assert L % tm == 0
        ns, ls, tps = 1, tm, L // tm
    has_state = state is not None
    cb2 = conv_b.reshape(1, 2 * D_FF)

    def stream_blk(i):
        return (i * tm) // L // ns if ns > 1 else (i * tm) // L

    up = lambda j: jnp.minimum(j, nj - 1)
    dn = lambda j: jnp.maximum(j - 1, 0)
    in_specs = [
        pl.BlockSpec((tm, D), lambda i, j: (i, 0), pipeline_mode=pl.Buffered(1)),
        pl.BlockSpec((1, D), lambda i, j: (0, 0)),
        pl.BlockSpec((D, tf), lambda i, j: (0, up(j))),
        pl.BlockSpec((D, tf), lambda i, j: (0, nj + up(j))),
        pl.BlockSpec((3, tf), lambda i, j: (0, up(j))),
        pl.BlockSpec((3, tf), lambda i, j: (0, nj + up(j))),
        pl.BlockSpec((1, tf), lambda i, j: (0, up(j))),
        pl.BlockSpec((1, tf), lambda i, j: (0, nj + up(j))),
        pl.BlockSpec((tf, D), lambda i, j: (dn(j), 0)),
        pl.BlockSpec((1, D), lambda i, j: (0, 0)),
    ]
    args = [x, nw, w_in_bf, w_in_bf, conv_w, conv_w, cb2, cb2, w_out_bf, nw2]
    if has_state:
        in_specs += [
            pl.BlockSpec((ns, 2, tf), lambda i, j: (stream_blk(i), 0, up(j))),
            pl.BlockSpec((ns, 2, tf), lambda i, j: (stream_blk(i), 0, nj + up(j))),
        ]
        args += [state, state]
    out, csg, csu = pl.pallas_call(
        functools.partial(_ffn_kernel, ns=ns, ls=ls, tps=tps, nj=nj, has_state=has_state),
        grid=(T // tm, nj + 1),
        in_specs=in_specs,
        out_specs=[
            pl.BlockSpec((tm, D), lambda i, j: (i, 0)),
            pl.BlockSpec((ns, 2, tf), lambda i, j: (i, 0, up(j))),
            pl.BlockSpec((ns, 2, tf), lambda i, j: (i, 0, up(j))),
        ],
        out_shape=[
            jax.ShapeDtypeStruct((T, D), F32),
            jax.ShapeDtypeStruct((T // ls, 2, D_FF), F32),
            jax.ShapeDtypeStruct((T // ls, 2, D_FF), F32),
        ],
        scratch_shapes=[
            pltpu.VMEM((tm, D), BF16),
            pltpu.VMEM((2, ns, 8 + ls, tf), F32),
            pltpu.VMEM((nj, 2, ns, 2, tf), F32),
            pltpu.VMEM((2, tm, tf), BF16),
        ],
        compiler_params=_cparams(("arbitrary", "arbitrary")),
        name="conv_ffn",
    )(*args)
    tails = jnp.concatenate([csg, csu], axis=-1).reshape(n_streams, tps, 2, 2 * D_FF)
    return out, tails[:, tps - 1]


def _rope_table_kernel(c_ref, s_ref):
    shape = c_ref.shape
    half = ROPE_DIM // 2
    pos = lax.broadcasted_iota(jnp.int32, shape, 0).astype(F32)
    lane = lax.broadcasted_iota(jnp.int32, shape, 1)
    fi = (lane & (half - 1)).astype(F32)
    inv = jnp.exp(fi * (-math.log(ROPE_BASE) / half))
    ang = pos * inv
    valid = lane < ROPE_DIM
    c_ref[...] = jnp.where(valid, jnp.cos(ang), 0.0)
    s_ref[...] = jnp.where(valid, jnp.where(lane < half, -jnp.sin(ang), jnp.sin(ang)), 0.0)


def _rope_tables(n_pos):
    return pl.pallas_call(
        _rope_table_kernel,
        out_shape=[jax.ShapeDtypeStruct((n_pos, LANES), F32)] * 2,
        name="rope_tables",
    )()


def _mla_in_kernel(x_ref, nw_ref, wd_ref, qnw_ref, kvnw_ref, wq_ref, c_ref, s_ref,
                   qcat_ref, ckv_ref, kpe_ref):
    h = _rms(x_ref[...], nw_ref[...]).astype(BF16)
    d = _dot(h, wd_ref[...])
    cq = _rms(d[:, :Q_LORA], qnw_ref[...]).astype(BF16)
    ckv_ref[...] = _rms(d[:, Q_LORA:Q_LORA + KV_LORA], kvnw_ref[...])
    cs = c_ref[...]
    sn = s_ref[...]
    o = Q_LORA + KV_LORA
    kpe_ref[...] = d[:, o:o + LANES] * cs + d[:, o + LANES:o + 2 * LANES] * sn
    hw = MLA_HEADS * LANES
    qn = _dot(cq, wq_ref[:, 0:hw])
    pr = _dot(cq, wq_ref[:, hw:2 * hw])
    ps = _dot(cq, wq_ref[:, 2 * hw:3 * hw])
    for hh in range(MLA_HEADS):
        sl = slice(hh * LANES, (hh + 1) * LANES)
        qcat_ref[hh, :, 0:LANES] = qn[:, sl].astype(BF16)
        qcat_ref[hh, :, LANES:2 * LANES] = (pr[:, sl] * cs + ps[:, sl] * sn).astype(BF16)


def _mla_in(x, nw, wd_bf, qnw, kvnw, wq_bf, cos_t, sin_t, tm):
    T, D = x.shape
    n_tab_blk = cos_t.shape[0] // tm
    return pl.pallas_call(
        _mla_in_kernel,
        grid=(T // tm,),
        in_specs=[
            pl.BlockSpec((tm, D), lambda i: (i, 0)),
            pl.BlockSpec((1, D), lambda i: (0, 0)),
            pl.BlockSpec(wd_bf.shape, lambda i: (0, 0)),
            pl.BlockSpec((1, Q_LORA), lambda i: (0, 0)),
            pl.BlockSpec((1, KV_LORA), lambda i: (0, 0)),
            pl.BlockSpec(wq_bf.shape, lambda i: (0, 0)),
            pl.BlockSpec((tm, LANES), lambda i: (i % n_tab_blk, 0)),
            pl.BlockSpec((tm, LANES), lambda i: (i % n_tab_blk, 0)),
        ],
        out_specs=[
            pl.BlockSpec((MLA_HEADS, tm, QK_PAD), lambda i: (0, i, 0)),
            pl.BlockSpec((tm, KV_LORA), lambda i: (i, 0)),
            pl.BlockSpec((tm, LANES), lambda i: (i, 0)),
        ],
        out_shape=[
            jax.ShapeDtypeStruct((MLA_HEADS, T, QK_PAD), BF16),
            jax.ShapeDtypeStruct((T, KV_LORA), F32),
            jax.ShapeDtypeStruct((T, LANES), F32),
        ],
        compiler_params=_cparams(("arbitrary",)),
        name="mla_in_proj",
    )(x, nw, wd_bf, qnw, kvnw, wq_bf, cos_t, sin_t)


def _kv_up_kernel(ckv_ref, kpe_ref, wkv_ref, kcat_ref, v_ref):
    kv = _dot(ckv_ref[...].astype(BF16), wkv_ref[...])
    kp = kpe_ref[...].astype(BF16)
    hw = NOPE_DIM + V_DIM
    for hh in range(MLA_HEADS):
        kcat_ref[hh, :, 0:LANES] = kv[:, hh * hw:hh * hw + NOPE_DIM].astype(BF16)
        kcat_ref[hh, :, LANES:2 * LANES] = kp
        v_ref[hh] = kv[:, hh * hw + NOPE_DIM:(hh + 1) * hw].astype(BF16)


def _kv_up(ckv, kpe_pad, wkv_bf, tm):
    R = ckv.shape[0]
    return pl.pallas_call(
        _kv_up_kernel,
        grid=(R // tm,),
        in_specs=[
            pl.BlockSpec((tm, KV_LORA), lambda i: (i, 0)),
            pl.BlockSpec((tm, LANES), lambda i: (i, 0)),
            pl.BlockSpec(wkv_bf.shape, lambda i: (0, 0)),
        ],
        out_specs=[
            pl.BlockSpec((MLA_HEADS, tm, QK_PAD), lambda i: (0, i, 0)),
            pl.BlockSpec((MLA_HEADS, tm, V_DIM), lambda i: (0, i, 0)),
        ],
        out_shape=[
            jax.ShapeDtypeStruct((MLA_HEADS, R, QK_PAD), BF16),
            jax.ShapeDtypeStruct((MLA_HEADS, R, V_DIM), BF16),
        ],
        compiler_params=_cparams(("arbitrary",)),
        name="mla_kv_up_proj",
    )(ckv, kpe_pad, wkv_bf)


def _attn_prompt_kernel(*refs, L, tq, scale, n_side):
    q_ref, k_ref, v_ref = refs[:3]
    o_ref = refs[3 + n_side]
    _side_cast(refs[3:3 + n_side], refs[4 + n_side:])
    row = lax.broadcasted_iota(jnp.int32, (tq, tq), 0)
    col = lax.broadcasted_iota(jnp.int32, (tq, tq), 1)
    sh = int(math.log2(CHUNK))
    dmask = (row >> sh) >= (col >> sh)

    c2 = scale * math.log2(math.e)
    def scores(qi):
        n = qi * tq
        q = q_ref[0, n:n + tq, :]
        s_d = jnp.where(dmask, _dot_nt(q, k_ref[0, n:n + tq, :]), -jnp.inf)
        s_p = _dot_nt(q, k_ref[0, 0:n, :]) if qi > 0 else None
        return s_d, s_p

    nq = L // tq
    pending = [scores(t) for t in range(min(SCORES_AHEAD, nq))]
    for qi in range(nq):
        n = qi * tq
        s_d, s_p = pending.pop(0)
        if qi + SCORES_AHEAD < nq:
            pending.append(scores(qi + SCORES_AHEAD))
        m = jnp.max(s_d, axis=-1, keepdims=True)
        if qi > 0:
            m = jnp.maximum(m, jnp.max(s_p, axis=-1, keepdims=True))
        p_d = jnp.exp2((s_d - m) * c2)
        l = jnp.sum(p_d, axis=-1, keepdims=True)
        acc = _dot(p_d.astype(BF16), v_ref[0, n:n + tq, :])
        if qi > 0:
            p_p = jnp.exp2((s_p - m) * c2)
            l = l + jnp.sum(p_p, axis=-1, keepdims=True)
            acc = acc + _dot(p_p.astype(BF16), v_ref[0, 0:n, :])
        o_ref[n:n + tq, :] = (acc / l).astype(BF16)


def _attn_prompt(qcat, kcat, v, B, L, side=(), side_layer=0):
    H = MLA_HEADS
    scale = (NOPE_DIM + ROPE_DIM) ** -0.5
    s_in, s_out, s_shape = _side_cast_specs(side, side_layer, B * H, lambda b, h: b * H + h)
    outs = pl.pallas_call(
        functools.partial(_attn_prompt_kernel, L=L, tq=ATTN_Q_TILE, scale=scale,
                          n_side=len(side)),
        grid=(B, H),
        in_specs=[
            pl.BlockSpec((1, L, QK_PAD), lambda b, h: (h, b, 0)),
            pl.BlockSpec((1, L, QK_PAD), lambda b, h: (h, b, 0)),
            pl.BlockSpec((1, L, V_DIM), lambda b, h: (h, b, 0)),
        ] + s_in,
        out_specs=[pl.BlockSpec((L, V_DIM), lambda b, h: (b, h))] + s_out,
        out_shape=[jax.ShapeDtypeStruct((B * L, H * V_DIM), BF16)] + s_shape,
        compiler_params=_cparams(("arbitrary", "arbitrary")),
        name="attn_prompt",
    )(qcat, kcat, v, *side)
    return outs[0], outs[1:]


def _attn_sample_kernel(q_ref, cc_ref, pc_ref, cn_ref, pn_ref, wkv_ref, o_ref, *, past, scale):
    H, Lq, _ = q_ref.shape
    hw = NOPE_DIM + V_DIM
    qlat, qrope = [], []
    for hh in range(H):
        wk_h = wkv_ref[:, hh * hw:hh * hw + NOPE_DIM]
        qlat.append(_dot_nt(q_ref[hh, :, 0:NOPE_DIM], wk_h).astype(BF16))
        qrope.append(q_ref[hh, :, NOPE_DIM:NOPE_DIM + ROPE_DIM])
    qlat = jnp.concatenate(qlat, axis=0)
    qrope = jnp.concatenate(qrope, axis=0)
    cc = cc_ref[0].astype(BF16)
    cn = cn_ref[...].astype(BF16)
    s1 = _dot_nt(qlat, cc) + _dot_nt(qrope, pc_ref[0].astype(BF16))
    s2 = _dot_nt(qlat, cn) + _dot_nt(qrope, pn_ref[:, 0:ROPE_DIM].astype(BF16))
    sh = int(math.log2(CHUNK))

    def q_chunk(n):
        t = lax.broadcasted_iota(jnp.int32, (H, Lq, n), 1).reshape(H * Lq, n)
        return (t + past) >> sh

    kc1 = lax.broadcasted_iota(jnp.int32, s1.shape, 1) >> sh
    kc2 = (lax.broadcasted_iota(jnp.int32, s2.shape, 1) + past) >> sh
    s1 = jnp.where(kc1 <= q_chunk(s1.shape[1]), s1 * scale, -jnp.inf)
    s2 = jnp.where(kc2 <= q_chunk(s2.shape[1]), s2 * scale, -jnp.inf)
    m = jnp.maximum(jnp.max(s1, axis=-1, keepdims=True), jnp.max(s2, axis=-1, keepdims=True))
    p1 = jnp.exp(s1 - m)
    p2 = jnp.exp(s2 - m)
    l = jnp.sum(p1, axis=-1, keepdims=True) + jnp.sum(p2, axis=-1, keepdims=True)
    olat = ((_dot(p1.astype(BF16), cc) + _dot(p2.astype(BF16), cn)) / l).astype(BF16)
    for hh in range(H):
        sl = slice(hh * LANES, (hh + 1) * LANES)
        wv_h = wkv_ref[:, hh * hw + NOPE_DIM:(hh + 1) * hw]
        o_ref[:, sl] = _dot(olat[hh * Lq:(hh + 1) * Lq], wv_h).astype(BF16)


def _attn_sample(qcat, ckv_cache, kpe_cache, ckv_new, kpe_new, wkv_bf, B, L, past):
    H = MLA_HEADS
    scale = (NOPE_DIM + ROPE_DIM) ** -0.5
    return pl.pallas_call(
        functools.partial(_attn_sample_kernel, past=past, scale=scale),
        grid=(B,),
        in_specs=[
            pl.BlockSpec((H, L, QK_PAD), lambda b: (0, b, 0)),
            pl.BlockSpec((1, past, KV_LORA), lambda b: (b, 0, 0)),
            pl.BlockSpec((1, past, ROPE_DIM), lambda b: (b, 0, 0)),
            pl.BlockSpec((L, KV_LORA), lambda b: (b, 0)),
            pl.BlockSpec((L, LANES), lambda b: (b, 0)),
            pl.BlockSpec(wkv_bf.shape, lambda b: (0, 0)),
        ],
        out_specs=pl.BlockSpec((L, H * V_DIM), lambda b: (b, 0)),
        out_shape=jax.ShapeDtypeStruct((B * L, H * V_DIM), BF16),
        compiler_params=_cparams(("arbitrary",)),
        name="attn_sample",
    )(qcat, ckv_cache, kpe_cache, ckv_new, kpe_new, wkv_bf)


def _swap_halves(w):
    half = w.shape[-1] // 2
    return jnp.concatenate([w[..., half:], w[..., :half]], axis=-1)


def _pad_lanes(w):
    return jnp.pad(w, [(0, 0)] * (w.ndim - 1) + [(0, LANES - w.shape[-1])])


def _prep_mla_weights(w_down, w_uq, w_ukv):
    o = Q_LORA + KV_LORA
    wpe = w_down[:, o:]
    wd = jnp.concatenate([w_down[:, :o], _pad_lanes(wpe), _pad_lanes(_swap_halves(wpe))], axis=-1)
    wq3 = w_uq.reshape(Q_LORA, MLA_HEADS, NOPE_DIM + ROPE_DIM)
    wn = wq3[..., :NOPE_DIM].reshape(Q_LORA, -1)
    wr = wq3[..., NOPE_DIM:]
    wrp = _pad_lanes(wr).reshape(Q_LORA, -1)
    wrs = _pad_lanes(_swap_halves(wr)).reshape(Q_LORA, -1)
    wq = jnp.concatenate([wn, wrp, wrs], axis=-1)
    return wd.astype(BF16), wq.astype(BF16), w_ukv.astype(BF16)


def _trunk(x, B, L, hg_state, ckv_cache, kpe_cache, conv_state, W, cos_t, sin_t, ffn_bf=None):
    norm_w = W["norm_w"]
    nrow = lambda l, k: norm_w[l, k].reshape(1, -1)
    past = 0 if ckv_cache is None else ckv_cache.shape[1]
    T = B * L
    D = x.shape[-1]
    tm, tm_ffn, tm_mla = min(T, TOKEN_TILE), min(T, FFN_TOKEN_TILE), min(T, MLA_TOKEN_TILE)
    side = () if ffn_bf is not None else W["ffn_f32"]
    ffn_bf = list(ffn_bf) if ffn_bf is not None else [None, None]

    def as_ffn_weights(cast):
        w_in, w_out = cast
        return w_in, w_out.reshape(D_FF, D)

    p, cast = _hgrn_in(x, nrow(0, 0), W["hgrn_w_in"], W["lb_logits"], 0, tm, side, 0)
    if side:
        ffn_bf[0] = as_ffn_weights(cast)
    z, hg_new = _hgrn_rec(p, W["hgrn_gnorm_w"].reshape(1, -1), hg_state, B, L)
    x = _out_proj(z, W["hgrn_w_out"], x, nrow(0, 1), tm)
    x, conv0 = _ffn(x, nrow(0, 2), ffn_bf[0][0], W["ffn_conv_w"][0], W["ffn_conv_b"][0],
                    ffn_bf[0][1], nrow(0, 3), None if conv_state is None else conv_state[0],
                    B, L, tm_ffn)

    qcat, ckv, kpe_pad = _mla_in(x, nrow(1, 0), W["mla_wd"], W["mla_q_norm_w"].reshape(1, -1),
                                 W["mla_kv_norm_w"].reshape(1, -1), W["mla_wq"], cos_t, sin_t,
                                 tm_mla)
    if ckv_cache is None:
        kcat_n, v_n = _kv_up(ckv, kpe_pad, W["mla_wkv"], tm_mla)
        o, cast = _attn_prompt(qcat, kcat_n, v_n, B, L, side, 1)
        if side:
            ffn_bf[1] = as_ffn_weights(cast)
    else:
        o = _attn_sample(qcat, ckv_cache, kpe_cache, ckv, kpe_pad, W["mla_wkv"], B, L, past)
    x = _out_proj(o, W["mla_w_out"], x, nrow(1, 1), tm)
    x, conv1 = _ffn(x, nrow(1, 2), ffn_bf[1][0], W["ffn_conv_w"][1], W["ffn_conv_b"][1],
                    ffn_bf[1][1], nrow(1, 3), None if conv_state is None else conv_state[1],
                    B, L, tm_ffn)

    return (x.reshape(B, L, D), hg_new[None], ckv.reshape(1, B, L, KV_LORA),
            kpe_pad[:, :ROPE_DIM].reshape(1, B, L, ROPE_DIM), jnp.stack([conv0, conv1]), ffn_bf)


def kernel(x_prompt, x_sample, state_hgrn, cache_ckv, cache_kpe, state_conv, norm_w, lb_logits,
           hgrn_w_in, hgrn_gnorm_w, hgrn_w_out, mla_w_down, mla_q_norm_w, mla_kv_norm_w,
           mla_w_uq, mla_w_ukv, mla_w_out, ffn_w_in, ffn_conv_w, ffn_conv_b, ffn_w_out):
    wd, wq, wkv = _prep_mla_weights(mla_w_down[0], mla_w_uq[0], mla_w_ukv[0])
    W = dict(
        norm_w=norm_w, lb_logits=lb_logits,
        hgrn_w_in=hgrn_w_in[0].astype(BF16), hgrn_gnorm_w=hgrn_gnorm_w[0],
        hgrn_w_out=hgrn_w_out[0].astype(BF16),
        mla_wd=wd, mla_wq=wq, mla_wkv=wkv,
        mla_q_norm_w=mla_q_norm_w[0], mla_kv_norm_w=mla_kv_norm_w[0],
        mla_w_out=mla_w_out[0].astype(BF16),
        ffn_conv_w=ffn_conv_w, ffn_conv_b=ffn_conv_b,
        ffn_f32=(ffn_w_in, ffn_w_out.reshape(ffn_w_out.shape[0], -1, ffn_w_in.shape[-1])),
    )
    Bp, Lp, D = x_prompt.shape
    Bs, Ls, _ = x_sample.shape
    past = cache_ckv.shape[2]
    cos_t, sin_t = _rope_tables(max(Lp, past + Ls))

    yp, hgp, ckvp, kpep, convp, ffn_bf = _trunk(
        x_prompt.reshape(Bp * Lp, D), Bp, Lp, None, None, None, None, W, cos_t[:Lp], sin_t[:Lp])

    cos_s = jnp.tile(cos_t[past:past + Ls], (Bs, 1))
    sin_s = jnp.tile(sin_t[past:past + Ls], (Bs, 1))
    ys, hgs, ckvs, kpes, convs, _ = _trunk(
        x_sample.reshape(Bs * Ls, D), Bs, Ls, state_hgrn[0], cache_ckv[0], cache_kpe[0],
        state_conv, W, cos_s, sin_s, ffn_bf)
    return (yp, ys, hgp, hgs, ckvp, ckvs, kpep, kpes, convp, convs)
```

```python
import functools
import math

import jax
import jax.numpy as jnp
from jax import lax
from jax.experimental import pallas as pl
from jax.experimental.pallas import tpu as pltpu

F32 = jnp.float32
BF16 = jnp.bfloat16

EPS = 1e-6
LANES = 128
BF16_ROWS = 16
CHUNK = 64
HG_HEADS = 16
HG_DK = 128
HG_DV = 128
MLA_HEADS = 16
Q_LORA = 512
KV_LORA = 512
NOPE_DIM = 128
ROPE_DIM = 64
V_DIM = 128
ROPE_BASE = 10000.0
D_FF = 5632
FF_TILE = 512
TOKEN_TILE = 512
FFN_TOKEN_TILE = 1024
MLA_TOKEN_TILE = 256
IN_PROJ_COLS = 2048
ATTN_Q_TILE = 256
REC_HEADS = 4
REC_ROWS = 1024
SCORES_AHEAD = 2
QK_PAD = 256
VMEM_LIMIT = 56 * 1024 * 1024


def _cparams(sem):
    return pltpu.CompilerParams(dimension_semantics=sem, vmem_limit_bytes=VMEM_LIMIT)


def _rms(x, w):
    ms = jnp.mean(x * x, axis=-1, keepdims=True)
    return x * lax.rsqrt(ms + EPS) * w


def _sigmoid(x):
    return 0.5 * jnp.tanh(0.5 * x) + 0.5


def _dot(a, b):
    return jnp.dot(a, b, preferred_element_type=F32)


def _dot_nt(a, b):
    return lax.dot_general(a, b, (((1,), (1,)), ((), ())), preferred_element_type=F32)


def _dot_tn(a, b):
    return lax.dot_general(a, b, (((0,), (0,)), ((), ())), preferred_element_type=F32)


def _side_cast_specs(srcs, layer, n_steps, step):
    in_specs, out_specs, out_shapes, holds = [], [], [], []
    for w in srcs:
        _, R, C = w.shape
        hold = 1
        while (R * hold) % (n_steps * BF16_ROWS) != 0:
            hold *= 2
            assert hold <= n_steps, (R, n_steps)
        rows = R * hold // n_steps

        def slab(*g, hold=hold):
            return step(*g) // hold

        in_specs.append(pl.BlockSpec((None, rows, C), lambda *g, slab=slab: (layer, slab(*g), 0)))
        out_specs.append(pl.BlockSpec((rows, C), lambda *g, slab=slab: (slab(*g), 0)))
        out_shapes.append(jax.ShapeDtypeStruct((R, C), BF16))
        holds.append(hold)
    return in_specs, out_specs, out_shapes, tuple(holds)


def _side_cast(src_refs, dst_refs, holds):
    step = pl.program_id(0) * pl.num_programs(1) + pl.program_id(1)
    for src, dst, hold in zip(src_refs, dst_refs, holds):
        @pl.when(step % hold == 0)
        def _():
            dst[...] = src[...].astype(BF16)


def _hgrn_in_kernel(*refs, lb_row, blk_per_sec, side_holds):
    n_side = len(side_holds)
    x_ref, nw_ref, w_ref, lb_ref = refs[:4]
    o_ref, h_ref = refs[4 + n_side], refs[-1]
    _side_cast(refs[4:4 + n_side], refs[5 + n_side:5 + 2 * n_side], side_holds)
    j = pl.program_id(1)

    @pl.when(j == 0)
    def _():
        h_ref[...] = _rms(x_ref[...], nw_ref[...]).astype(BF16)

    sec = j // blk_per_sec
    heads = o_ref.shape[0]
    lg = lb_ref[...]
    e = jnp.exp(lg - jnp.max(lg, axis=0, keepdims=True))
    sm = e / jnp.sum(e, axis=0, keepdims=True)
    lb_all = jnp.sum(sm[0:lb_row + 1], axis=0, keepdims=True)
    hp = heads // 2
    for c0 in (0, hp):
        cols = slice(c0 * LANES, (c0 + hp) * LANES)
        r = _dot(h_ref[...], w_ref[:, cols])
        lb = lb_all[:, cols]
        sig = _sigmoid(r)
        val = jnp.where(sec == 2, r, jnp.where(sec == 1, lb + (1.0 - lb) * sig, r * sig))
        for hh in range(hp):
            o_ref[c0 + hh] = val[:, hh * LANES:(hh + 1) * LANES]


def _hgrn_in(x, nw, w_bf, lb_logits, lb_row, tm, side=(), side_layer=0):
    T, D = x.shape
    N = w_bf.shape[1]
    tn = IN_PROJ_COLS
    sec_w = N // 4
    blk_per_sec = sec_w // tn
    heads = tn // LANES
    nj = N // tn
    s_in, s_out, s_shape, holds = _side_cast_specs(side, side_layer, (T // tm) * nj,
                                                   lambda i, j: i * nj + j)
    outs = pl.pallas_call(
        functools.partial(_hgrn_in_kernel, lb_row=lb_row, blk_per_sec=blk_per_sec,
                          side_holds=holds),
        grid=(T // tm, nj),
        in_specs=[
            pl.BlockSpec((tm, D), lambda i, j: (i, 0)),
            pl.BlockSpec((1, D), lambda i, j: (0, 0)),
            pl.BlockSpec((D, tn), lambda i, j: (0, j)),
            pl.BlockSpec((lb_logits.shape[0], tn), lambda i, j: (0, j % blk_per_sec)),
        ] + s_in,
        out_specs=[pl.BlockSpec((heads, tm, LANES), lambda i, j: (j, i, 0))] + s_out,
        out_shape=[jax.ShapeDtypeStruct((N // LANES, T, LANES), F32)] + s_shape,
        scratch_shapes=[pltpu.VMEM((tm, D), BF16)],
        compiler_params=_cparams(("arbitrary", "arbitrary")),
        name="hgrn_in_proj",
    )(x, nw, w_bf, lb_logits, *side)
    return outs[0], outs[1:]


SUB = 8


def _bcast_mid_in_tile(c, s, t8):
    if s == 4:
        return jnp.broadcast_to(c[s - 1:s, :], c.shape)
    dn1 = pltpu.roll(c, 1, 0)
    if s == 1:
        return jnp.where((t8 & 1) == 1, dn1, c)
    ph = t8 & 3
    up1 = pltpu.roll(c, SUB - 1, 0)
    dn2 = pltpu.roll(c, 2, 0)
    return jnp.where(ph == 0, up1, jnp.where(ph == 1, c, jnp.where(ph == 2, dn1, dn2)))


def _gla_masks(C):
    t8 = lax.broadcasted_iota(jnp.int32, (SUB, LANES), 0)
    rr = lax.broadcasted_iota(jnp.int32, (C, C), 0)
    cc = lax.broadcasted_iota(jnp.int32, (C, C), 1)
    pairs = []
    s = 1
    while s < C:
        sh = int(math.log2(2 * s))
        pairs.append(((rr >> sh) == (cc >> sh)) & ((rr & s) != 0) & ((cc & s) == 0))
        s *= 2
    return t8, pairs


def _gla_head(q, fg, v, masks):
    C = q.shape[0]
    nt = C // SUB
    t8, pairs = masks
    k = 1.0 - fg
    tiles = lambda x: [x[i * SUB:(i + 1) * SUB] for i in range(nt)]
    qs, ks = tiles(q), tiles(k)
    cs = [jnp.log2(f) for f in tiles(fg)]
    zero = jnp.zeros((SUB, LANES), F32)
    a = jnp.zeros((C, C), F32)
    s = 1
    for pair in pairs:
        if s < SUB:
            upper = (t8 & s) != 0
            qd, kd = [], []
            for i in range(nt):
                bc = _bcast_mid_in_tile(cs[i], s, t8)
                qd.append(qs[i] * jnp.exp2(cs[i]))
                kd.append(ks[i] * jnp.exp2(jnp.minimum(bc - cs[i], 0.0)))
                cs[i] = cs[i] + jnp.where(upper, bc, 0.0)
        else:
            m = s // SUB
            qd, kd = [zero] * nt, [zero] * nt
            for blk in range(0, nt, 2 * m):
                bc = jnp.broadcast_to(cs[blk + m - 1][SUB - 1:SUB, :], (SUB, LANES))
                for i in range(blk, blk + m):
                    kd[i] = ks[i] * jnp.exp2(bc - cs[i])
                for i in range(blk + m, blk + 2 * m):
                    qd[i] = qs[i] * jnp.exp2(cs[i])
                    cs[i] = cs[i] + bc
        a_s = _dot_nt(jnp.concatenate(qd, axis=0).astype(BF16),
                      jnp.concatenate(kd, axis=0).astype(BF16))
        a = jnp.where(pair, a_s, a)
        s *= 2
    b = jnp.concatenate(cs, axis=0)
    b_last = b[C - 1:C, :]
    diag = jnp.sum(q * k, axis=-1, keepdims=True)
    qb = (q * jnp.exp2(b)).astype(BF16)
    kdec = (k * jnp.exp2(b_last - b)).astype(BF16)
    return a.astype(BF16), qb, kdec, diag * v, jnp.exp2(b_last)


def _gla_tail(a, qb, kdec, dv, e_last, v_bf, st):
    o = _dot(a, v_bf) + dv + _dot_nt(qb, st.astype(BF16))
    st_new = st * e_last + _dot_tn(v_bf, kdec)
    return o, st_new


def _hgrn_rec_kernel(*refs, C, n_chunks, has_state):
    if has_state:
        (q_ref, f_ref, v_ref, g_ref, gw_ref, s0_ref, z_ref, so_ref,
         st_ref, a_buf, qb_buf, kd_buf, dv_buf, el_buf) = refs
    else:
        (q_ref, f_ref, v_ref, g_ref, gw_ref, z_ref, so_ref,
         st_ref, a_buf, qb_buf, kd_buf, dv_buf, el_buf) = refs
    li = pl.program_id(2)
    hb = st_ref.shape[0]

    @pl.when(li == 0)
    def _():
        for hh in range(hb):
            st_ref[hh] = s0_ref[0, hh].T if has_state else jnp.zeros((HG_DV, HG_DK), F32)

    gw = gw_ref[...]
    masks = _gla_masks(C)

    def rows_of(ci):
        return pl.ds(pl.multiple_of(ci * C, C), C)

    def head(ci, slot):
        rows = rows_of(ci)
        for hh in range(hb):
            a, qb, kdec, dv, e_last = _gla_head(q_ref[hh, rows, :], f_ref[hh, rows, :],
                                                v_ref[hh, rows, :], masks)
            a_buf[slot, hh] = a
            qb_buf[slot, hh] = qb
            kd_buf[slot, hh] = kdec
            dv_buf[slot, hh] = dv
            el_buf[slot, hh] = e_last

    def tail(ci, slot):
        rows = rows_of(ci)
        for hh in range(hb):
            o, st_new = _gla_tail(a_buf[slot, hh], qb_buf[slot, hh], kd_buf[slot, hh],
                                  dv_buf[slot, hh], el_buf[slot, hh],
                                  v_ref[hh, rows, :].astype(BF16), st_ref[hh])
            st_ref[hh] = st_new
            z_ref[rows, hh * LANES:(hh + 1) * LANES] = (
                _rms(o, gw) * g_ref[hh, rows, :]).astype(BF16)

    head(0, 0)
    if n_chunks > 1:
        assert n_chunks % 2 == 0

        def body(kk, carry):
            c0 = 2 * kk
            tail(c0, 0)
            head(c0 + 1, 1)
            tail(c0 + 1, 1)
            head(c0 + 2, 0)
            return carry

        lax.fori_loop(0, n_chunks // 2 - 1, body, 0)
        tail(n_chunks - 2, 0)
        head(n_chunks - 1, 1)
        tail(n_chunks - 1, 1)
    else:
        tail(0, 0)

    @pl.when(li == pl.num_programs(2) - 1)
    def _():
        for hh in range(hb):
            so_ref[0, hh] = st_ref[hh].T


def _hgrn_rec(p, gnorm_w, s0, B, L):
    H = HG_HEADS
    T = B * L
    C = min(CHUNK, L)
    hb = REC_HEADS if L > CHUNK else H
    lb = min(L, REC_ROWS)
    nl = L // lb
    ng = H // hb
    has_state = s0 is not None

    def sec_spec(sec):
        return pl.BlockSpec((hb, lb, LANES), lambda b, g, l: (sec * ng + g, b * nl + l, 0))

    in_specs = [sec_spec(0), sec_spec(1), sec_spec(2), sec_spec(3),
                pl.BlockSpec((1, HG_DV), lambda b, g, l: (0, 0))]
    args = [p, p, p, p, gnorm_w]
    if has_state:
        in_specs.append(pl.BlockSpec((1, hb, HG_DK, HG_DV), lambda b, g, l: (b, g, 0, 0)))
        args.append(s0)
    return pl.pallas_call(
        functools.partial(_hgrn_rec_kernel, C=C, n_chunks=lb // C, has_state=has_state),
        grid=(B, ng, nl),
        in_specs=in_specs,
        out_specs=[
            pl.BlockSpec((lb, hb * LANES), lambda b, g, l: (b * nl + l, g)),
            pl.BlockSpec((1, hb, HG_DK, HG_DV), lambda b, g, l: (b, g, 0, 0)),
        ],
        out_shape=[
            jax.ShapeDtypeStruct((T, H * HG_DV), BF16),
            jax.ShapeDtypeStruct((B, H, HG_DK, HG_DV), F32),
        ],
        scratch_shapes=[
            pltpu.VMEM((hb, HG_DV, HG_DK), F32),
            pltpu.VMEM((2, hb, C, C), BF16),
            pltpu.VMEM((2, hb, C, HG_DK), BF16),
            pltpu.VMEM((2, hb, C, HG_DK), BF16),
            pltpu.VMEM((2, hb, C, HG_DV), F32),
            pltpu.VMEM((2, hb, 1, HG_DK), F32),
        ],
        compiler_params=_cparams(("arbitrary", "arbitrary", "arbitrary")),
        name="hgrn_recurrence",
    )(*args)


def _out_proj_kernel(a_ref, w_ref, x_ref, nw_ref, o_ref):
    y = _dot(a_ref[...], w_ref[...])
    o_ref[...] = x_ref[...] + _rms(y, nw_ref[...])


def _out_proj(a_bf, w_bf, x, nw, tm):
    T, K = a_bf.shape
    N = w_bf.shape[1]
    return pl.pallas_call(
        _out_proj_kernel,
        grid=(T // tm,),
        in_specs=[
            pl.BlockSpec((tm, K), lambda i: (i, 0)),
            pl.BlockSpec((K, N), lambda i: (0, 0)),
            pl.BlockSpec((tm, N), lambda i: (i, 0)),
            pl.BlockSpec((1, N), lambda i: (0, 0)),
        ],
        out_specs=pl.BlockSpec((tm, N), lambda i: (i, 0)),
        out_shape=jax.ShapeDtypeStruct((T, N), F32),
        compiler_params=_cparams(("arbitrary",)),
        name="mixer_out_proj",
    )(a_bf, w_bf, x, nw)


def _ffn_kernel(*refs, ns, ls, tps, nj, has_state):
    if has_state:
        (x_ref, nw_ref, wg_ref, wu_ref, cwg_ref, cwu_ref, cbg_ref, cbu_ref, wo_ref, nw2_ref,
         sg_ref, su_ref, o_ref, csg_ref, csu_ref, h_ref, work_ref, carry_ref, act_ref) = refs
    else:
        (x_ref, nw_ref, wg_ref, wu_ref, cwg_ref, cwu_ref, cbg_ref, cbu_ref, wo_ref, nw2_ref,
         o_ref, csg_ref, csu_ref, h_ref, work_ref, carry_ref, act_ref) = refs
        sg_ref = su_ref = None
    i = pl.program_id(0)
    j = pl.program_id(1)
    tf = wg_ref.shape[1]

    def conv(half, w_ref, cw_ref, cb_ref, s_ref, cs_ref):
        u3 = _dot(h_ref[...], w_ref[...]).reshape(ns, ls, tf)
        work_ref[half, :, 8:8 + ls, :] = u3
        prev = jnp.zeros((ns, 2, tf), F32) if s_ref is None else s_ref[...]
        if tps > 1:
            prev = jnp.where((i % tps) == 0, prev, carry_ref[j, half])
        work_ref[half, :, 6:8, :] = prev
        x1 = work_ref[half, :, 7:7 + ls, :]
        x2 = work_ref[half, :, 6:6 + ls, :]
        cw = cw_ref[...]
        c = cb_ref[...] + cw[0:1] * x2 + cw[1:2] * x1 + cw[2:3] * u3
        tail = u3[:, ls - 2:ls, :]
        cs_ref[...] = tail
        if tps > 1:
            carry_ref[j, half] = tail
        return c.reshape(ns * ls, tf)

    def up_conv_act(slot):
        cg = conv(0, wg_ref, cwg_ref, cbg_ref, sg_ref, csg_ref)
        cu = conv(1, wu_ref, cwu_ref, cbu_ref, su_ref, csu_ref)
        act_ref[slot] = (cg * _sigmoid(cg) * cu).astype(BF16)

    @pl.when(j == 0)
    def _():
        h_ref[...] = _rms(x_ref[...], nw_ref[...]).astype(BF16)
        o_ref[...] = jnp.zeros_like(o_ref)
        up_conv_act(0)

    for par in range(2):
        @pl.when((j > 0) & (j < nj) & (j % 2 == par))
        def _():
            o_ref[...] += _dot(act_ref[1 - par], wo_ref[...])
            up_conv_act(par)

    @pl.when(j == nj)
    def _():
        y = o_ref[...] + _dot(act_ref[(nj - 1) % 2], wo_ref[...])
        o_ref[...] = x_ref[...] + _rms(y, nw2_ref[...])


def _ffn(x, nw, w_in_bf, conv_w, conv_b, w_out_bf, nw2, state, n_streams, L, tm):
    T, D = x.shape
    tf = FF_TILE
    nj = D_FF // tf
    if tm >= L:
        assert tm % L == 0
        ns, ls, tps = tm // L, L, 1
    else:
        assert L % tm == 0
        ns, ls, tps = 1, tm, L // tm
    has_state = state is not None
    cb2 = conv_b.reshape(1, 2 * D_FF)

    def stream_blk(i):
        return (i * tm) // L // ns if ns > 1 else (i * tm) // L

    up = lambda j: jnp.minimum(j, nj - 1)
    dn = lambda j: jnp.maximum(j - 1, 0)
    in_specs = [
        pl.BlockSpec((tm, D), lambda i, j: (i, 0), pipeline_mode=pl.Buffered(1)),
        pl.BlockSpec((1, D), lambda i, j: (0, 0)),
        pl.BlockSpec((D, tf), lambda i, j: (0, up(j))),
        pl.BlockSpec((D, tf), lambda i, j: (0, nj + up(j))),
        pl.BlockSpec((3, tf), lambda i, j: (0, up(j))),
        pl.BlockSpec((3, tf), lambda i, j: (0, nj + up(j))),
        pl.BlockSpec((1, tf), lambda i, j: (0, up(j))),
        pl.BlockSpec((1, tf), lambda i, j: (0, nj + up(j))),
        pl.BlockSpec((tf, D), lambda i, j: (dn(j), 0)),
        pl.BlockSpec((1, D), lambda i, j: (0, 0)),
    ]
    args = [x, nw, w_in_bf, w_in_bf, conv_w, conv_w, cb2, cb2, w_out_bf, nw2]
    if has_state:
        in_specs += [
            pl.BlockSpec((ns, 2, tf), lambda i, j: (stream_blk(i), 0, up(j))),
            pl.BlockSpec((ns, 2, tf), lambda i, j: (stream_blk(i), 0, nj + up(j))),
        ]
        args += [state, state]
    out, csg, csu = pl.pallas_call(
        functools.partial(_ffn_kernel, ns=ns, ls=ls, tps=tps, nj=nj, has_state=has_state),
        grid=(T // tm, nj + 1),
        in_specs=in_specs,
        out_specs=[
            pl.BlockSpec((tm, D), lambda i, j: (i, 0)),
            pl.BlockSpec((ns, 2, tf), lambda i, j: (i, 0, up(j))),
            pl.BlockSpec((ns, 2, tf), lambda i, j: (i, 0, up(j))),
        ],
        out_shape=[
            jax.ShapeDtypeStruct((T, D), F32),
            jax.ShapeDtypeStruct((T // ls, 2, D_FF), F32),
            jax.ShapeDtypeStruct((T // ls, 2, D_FF), F32),
        ],
        scratch_shapes=[
            pltpu.VMEM((tm, D), BF16),
            pltpu.VMEM((2, ns, 8 + ls, tf), F32),
            pltpu.VMEM((nj, 2, ns, 2, tf), F32),
            pltpu.VMEM((2, tm, tf), BF16),
        ],
        compiler_params=_cparams(("arbitrary", "arbitrary")),
        name="conv_ffn",
    )(*args)
    tails = jnp.concatenate([csg, csu], axis=-1).reshape(n_streams, tps, 2, 2 * D_FF)
    return out, tails[:, tps - 1]


def _rope_table_kernel(c_ref, s_ref):
    shape = c_ref.shape
    half = ROPE_DIM // 2
    pos = lax.broadcasted_iota(jnp.int32, shape, 0).astype(F32)
    lane = lax.broadcasted_iota(jnp.int32, shape, 1)
    fi = (lane & (half - 1)).astype(F32)
    inv = jnp.exp(fi * (-math.log(ROPE_BASE) / half))
    ang = pos * inv
    valid = lane < ROPE_DIM
    c_ref[...] = jnp.where(valid, jnp.cos(ang), 0.0)
    s_ref[...] = jnp.where(valid, jnp.where(lane < half, -jnp.sin(ang), jnp.sin(ang)), 0.0)


def _rope_tables(n_pos):
    return pl.pallas_call(
        _rope_table_kernel,
        out_shape=[jax.ShapeDtypeStruct((n_pos, LANES), F32)] * 2,
        name="rope_tables",
    )()


def _mla_in_kernel(x_ref, nw_ref, wd_ref, qnw_ref, kvnw_ref, wq_ref, c_ref, s_ref,
                   qcat_ref, ckv_ref, kpe_ref):
    h = _rms(x_ref[...], nw_ref[...]).astype(BF16)
    d = _dot(h, wd_ref[...])
    cq = _rms(d[:, :Q_LORA], qnw_ref[...]).astype(BF16)
    ckv_ref[...] = _rms(d[:, Q_LORA:Q_LORA + KV_LORA], kvnw_ref[...])
    cs = c_ref[...]
    sn = s_ref[...]
    o = Q_LORA + KV_LORA
    kpe_ref[...] = d[:, o:o + LANES] * cs + d[:, o + LANES:o + 2 * LANES] * sn
    hw = MLA_HEADS * LANES
    qn = _dot(cq, wq_ref[:, 0:hw])
    pr = _dot(cq, wq_ref[:, hw:2 * hw])
    ps = _dot(cq, wq_ref[:, 2 * hw:3 * hw])
    for hh in range(MLA_HEADS):
        sl = slice(hh * LANES, (hh + 1) * LANES)
        qcat_ref[hh, :, 0:LANES] = qn[:, sl].astype(BF16)
        qcat_ref[hh, :, LANES:2 * LANES] = (pr[:, sl] * cs + ps[:, sl] * sn).astype(BF16)


def _mla_in(x, nw, wd_bf, qnw, kvnw, wq_bf, cos_t, sin_t, tm):
    T, D = x.shape
    n_tab_blk = cos_t.shape[0] // tm
    return pl.pallas_call(
        _mla_in_kernel,
        grid=(T // tm,),
        in_specs=[
            pl.BlockSpec((tm, D), lambda i: (i, 0)),
            pl.BlockSpec((1, D), lambda i: (0, 0)),
            pl.BlockSpec(wd_bf.shape, lambda i: (0, 0)),
            pl.BlockSpec((1, Q_LORA), lambda i: (0, 0)),
            pl.BlockSpec((1, KV_LORA), lambda i: (0, 0)),
            pl.BlockSpec(wq_bf.shape, lambda i: (0, 0)),
            pl.BlockSpec((tm, LANES), lambda i: (i % n_tab_blk, 0)),
            pl.BlockSpec((tm, LANES), lambda i: (i % n_tab_blk, 0)),
        ],
        out_specs=[
            pl.BlockSpec((MLA_HEADS, tm, QK_PAD), lambda i: (0, i, 0)),
            pl.BlockSpec((tm, KV_LORA), lambda i: (i, 0)),
            pl.BlockSpec((tm, LANES), lambda i: (i, 0)),
        ],
        out_shape=[
            jax.ShapeDtypeStruct((MLA_HEADS, T, QK_PAD), BF16),
            jax.ShapeDtypeStruct((T, KV_LORA), F32),
            jax.ShapeDtypeStruct((T, LANES), F32),
        ],
        compiler_params=_cparams(("arbitrary",)),
        name="mla_in_proj",
    )(x, nw, wd_bf, qnw, kvnw, wq_bf, cos_t, sin_t)


def _kv_up_kernel(ckv_ref, kpe_ref, wkv_ref, kcat_ref, v_ref):
    kv = _dot(ckv_ref[...].astype(BF16), wkv_ref[...])
    kp = kpe_ref[...].astype(BF16)
    hw = NOPE_DIM + V_DIM
    for hh in range(MLA_HEADS):
        kcat_ref[hh, :, 0:LANES] = kv[:, hh * hw:hh * hw + NOPE_DIM].astype(BF16)
        kcat_ref[hh, :, LANES:2 * LANES] = kp
        v_ref[hh] = kv[:, hh * hw + NOPE_DIM:(hh + 1) * hw].astype(BF16)


def _kv_up(ckv, kpe_pad, wkv_bf, tm):
    R = ckv.shape[0]
    return pl.pallas_call(
        _kv_up_kernel,
        grid=(R // tm,),
        in_specs=[
            pl.BlockSpec((tm, KV_LORA), lambda i: (i, 0)),
            pl.BlockSpec((tm, LANES), lambda i: (i, 0)),
            pl.BlockSpec(wkv_bf.shape, lambda i: (0, 0)),
        ],
        out_specs=[
            pl.BlockSpec((MLA_HEADS, tm, QK_PAD), lambda i: (0, i, 0)),
            pl.BlockSpec((MLA_HEADS, tm, V_DIM), lambda i: (0, i, 0)),
        ],
        out_shape=[
            jax.ShapeDtypeStruct((MLA_HEADS, R, QK_PAD), BF16),
            jax.ShapeDtypeStruct((MLA_HEADS, R, V_DIM), BF16),
        ],
        compiler_params=_cparams(("arbitrary",)),
        name="mla_kv_up_proj",
    )(ckv, kpe_pad, wkv_bf)


def _attn_prompt_kernel(*refs, L, tq, scale, side_holds):
    n_side = len(side_holds)
    q_ref, k_ref, v_ref = refs[:3]
    o_ref = refs[3 + n_side]
    _side_cast(refs[3:3 + n_side], refs[4 + n_side:], side_holds)
    row = lax.broadcasted_iota(jnp.int32, (tq, tq), 0)
    col = lax.broadcasted_iota(jnp.int32, (tq, tq), 1)
    sh = int(math.log2(CHUNK))
    dmask = (row >> sh) >= (col >> sh)

    c2 = scale * math.log2(math.e)
    def scores(qi):
        n = qi * tq
        q = q_ref[0, n:n + tq, :]
        s_d = jnp.where(dmask, _dot_nt(q, k_ref[0, n:n + tq, :]), -jnp.inf)
        s_p = _dot_nt(q, k_ref[0, 0:n, :]) if qi > 0 else None
        return s_d, s_p

    nq = L // tq
    pending = [scores(t) for t in range(min(SCORES_AHEAD, nq))]
    for qi in range(nq):
        n = qi * tq
        s_d, s_p = pending.pop(0)
        if qi + SCORES_AHEAD < nq:
            pending.append(scores(qi + SCORES_AHEAD))
        m = jnp.max(s_d, axis=-1, keepdims=True)
        if qi > 0:
            m = jnp.maximum(m, jnp.max(s_p, axis=-1, keepdims=True))
        p_d = jnp.exp2((s_d - m) * c2)
        l = jnp.sum(p_d, axis=-1, keepdims=True)
        acc = _dot(p_d.astype(BF16), v_ref[0, n:n + tq, :])
        if qi > 0:
            p_p = jnp.exp2((s_p - m) * c2)
            l = l + jnp.sum(p_p, axis=-1, keepdims=True)
            acc = acc + _dot(p_p.astype(BF16), v_ref[0, 0:n, :])
        o_ref[n:n + tq, :] = (acc / l).astype(BF16)


def _attn_prompt(qcat, kcat, v, B, L, side=(), side_layer=0):
    H = MLA_HEADS
    scale = (NOPE_DIM + ROPE_DIM) ** -0.5
    s_in, s_out, s_shape, holds = _side_cast_specs(side, side_layer, B * H,
                                                   lambda b, h: b * H + h)
    outs = pl.pallas_call(
        functools.partial(_attn_prompt_kernel, L=L, tq=ATTN_Q_TILE, scale=scale,
                          side_holds=holds),
        grid=(B, H),
        in_specs=[
            pl.BlockSpec((1, L, QK_PAD), lambda b, h: (h, b, 0)),
            pl.BlockSpec((1, L, QK_PAD), lambda b, h: (h, b, 0)),
            pl.BlockSpec((1, L, V_DIM), lambda b, h: (h, b, 0)),
        ] + s_in,
        out_specs=[pl.BlockSpec((L, V_DIM), lambda b, h: (b, h))] + s_out,
        out_shape=[jax.ShapeDtypeStruct((B * L, H * V_DIM), BF16)] + s_shape,
        compiler_params=_cparams(("arbitrary", "arbitrary")),
        name="attn_prompt",
    )(qcat, kcat, v, *side)
    return outs[0], outs[1:]


def _attn_sample_kernel(q_ref, cc_ref, pc_ref, cn_ref, pn_ref, wkv_ref, o_ref, *, past, scale):
    H, Lq, _ = q_ref.shape
    hw = NOPE_DIM + V_DIM
    qlat, qrope = [], []
    for hh in range(H):
        wk_h = wkv_ref[:, hh * hw:hh * hw + NOPE_DIM]
        qlat.append(_dot_nt(q_ref[hh, :, 0:NOPE_DIM], wk_h).astype(BF16))
        qrope.append(q_ref[hh, :, NOPE_DIM:NOPE_DIM + ROPE_DIM])
    qlat = jnp.concatenate(qlat, axis=0)
    qrope = jnp.concatenate(qrope, axis=0)
    cc = cc_ref[0].astype(BF16)
    cn = cn_ref[...].astype(BF16)
    s1 = _dot_nt(qlat, cc) + _dot_nt(qrope, pc_ref[0].astype(BF16))
    s2 = _dot_nt(qlat, cn) + _dot_nt(qrope, pn_ref[:, 0:ROPE_DIM].astype(BF16))
    sh = int(math.log2(CHUNK))

    def q_chunk(n):
        t = lax.broadcasted_iota(jnp.int32, (H, Lq, n), 1).reshape(H * Lq, n)
        return (t + past) >> sh

    kc1 = lax.broadcasted_iota(jnp.int32, s1.shape, 1) >> sh
    kc2 = (lax.broadcasted_iota(jnp.int32, s2.shape, 1) + past) >> sh
    s1 = jnp.where(kc1 <= q_chunk(s1.shape[1]), s1 * scale, -jnp.inf)
    s2 = jnp.where(kc2 <= q_chunk(s2.shape[1]), s2 * scale, -jnp.inf)
    m = jnp.maximum(jnp.max(s1, axis=-1, keepdims=True), jnp.max(s2, axis=-1, keepdims=True))
    p1 = jnp.exp(s1 - m)
    p2 = jnp.exp(s2 - m)
    l = jnp.sum(p1, axis=-1, keepdims=True) + jnp.sum(p2, axis=-1, keepdims=True)
    olat = ((_dot(p1.astype(BF16), cc) + _dot(p2.astype(BF16), cn)) / l).astype(BF16)
    for hh in range(H):
        sl = slice(hh * LANES, (hh + 1) * LANES)
        wv_h = wkv_ref[:, hh * hw + NOPE_DIM:(hh + 1) * hw]
        o_ref[:, sl] = _dot(olat[hh * Lq:(hh + 1) * Lq], wv_h).astype(BF16)


def _attn_sample(qcat, ckv_cache, kpe_cache, ckv_new, kpe_new, wkv_bf, B, L, past):
    H = MLA_HEADS
    scale = (NOPE_DIM + ROPE_DIM) ** -0.5
    return pl.pallas_call(
        functools.partial(_attn_sample_kernel, past=past, scale=scale),
        grid=(B,),
        in_specs=[
            pl.BlockSpec((H, L, QK_PAD), lambda b: (0, b, 0)),
            pl.BlockSpec((1, past, KV_LORA), lambda b: (b, 0, 0)),
            pl.BlockSpec((1, past, ROPE_DIM), lambda b: (b, 0, 0)),
            pl.BlockSpec((L, KV_LORA), lambda b: (b, 0)),
            pl.BlockSpec((L, LANES), lambda b: (b, 0)),
            pl.BlockSpec(wkv_bf.shape, lambda b: (0, 0)),
        ],
        out_specs=pl.BlockSpec((L, H * V_DIM), lambda b: (b, 0)),
        out_shape=jax.ShapeDtypeStruct((B * L, H * V_DIM), BF16),
        compiler_params=_cparams(("arbitrary",)),
        name="attn_sample",
    )(qcat, ckv_cache, kpe_cache, ckv_new, kpe_new, wkv_bf)


def _swap_halves(w):
    half = w.shape[-1] // 2
    return jnp.concatenate([w[..., half:], w[..., :half]], axis=-1)


def _pad_lanes(w):
    return jnp.pad(w, [(0, 0)] * (w.ndim - 1) + [(0, LANES - w.shape[-1])])


def _prep_mla_weights(w_down, w_uq, w_ukv):
    o = Q_LORA + KV_LORA
    wpe = w_down[:, o:]
    wd = jnp.concatenate([w_down[:, :o], _pad_lanes(wpe), _pad_lanes(_swap_halves(wpe))], axis=-1)
    wq3 = w_uq.reshape(Q_LORA, MLA_HEADS, NOPE_DIM + ROPE_DIM)
    wn = wq3[..., :NOPE_DIM].reshape(Q_LORA, -1)
    wr = wq3[..., NOPE_DIM:]
    wrp = _pad_lanes(wr).reshape(Q_LORA, -1)
    wrs = _pad_lanes(_swap_halves(wr)).reshape(Q_LORA, -1)
    wq = jnp.concatenate([wn, wrp, wrs], axis=-1)
    return wd.astype(BF16), wq.astype(BF16), w_ukv.astype(BF16)


def _trunk(x, B, L, hg_state, ckv_cache, kpe_cache, conv_state, W, cos_t, sin_t, ffn_bf=None):
    norm_w = W["norm_w"]
    nrow = lambda l, k: norm_w[l, k].reshape(1, -1)
    past = 0 if ckv_cache is None else ckv_cache.shape[1]
    T = B * L
    D = x.shape[-1]
    tm, tm_ffn, tm_mla = min(T, TOKEN_TILE), min(T, FFN_TOKEN_TILE), min(T, MLA_TOKEN_TILE)
    side = () if ffn_bf is not None else W["ffn_f32"]
    ffn_bf = list(ffn_bf) if ffn_bf is not None else [None, None]

    p, cast = _hgrn_in(x, nrow(0, 0), W["hgrn_w_in"], W["lb_logits"], 0, tm, side, 0)
    if side:
        ffn_bf[0] = cast
    z, hg_new = _hgrn_rec(p, W["hgrn_gnorm_w"].reshape(1, -1), hg_state, B, L)
    x = _out_proj(z, W["hgrn_w_out"], x, nrow(0, 1), tm)
    x, conv0 = _ffn(x, nrow(0, 2), ffn_bf[0][0], W["ffn_conv_w"][0], W["ffn_conv_b"][0],
                    ffn_bf[0][1], nrow(0, 3), None if conv_state is None else conv_state[0],
                    B, L, tm_ffn)

    qcat, ckv, kpe_pad = _mla_in(x, nrow(1, 0), W["mla_wd"], W["mla_q_norm_w"].reshape(1, -1),
                                 W["mla_kv_norm_w"].reshape(1, -1), W["mla_wq"], cos_t, sin_t,
                                 tm_mla)
    if ckv_cache is None:
        kcat_n, v_n = _kv_up(ckv, kpe_pad, W["mla_wkv"], tm_mla)
        o, cast = _attn_prompt(qcat, kcat_n, v_n, B, L, side, 1)
        if side:
            ffn_bf[1] = cast
    else:
        o = _attn_sample(qcat, ckv_cache, kpe_cache, ckv, kpe_pad, W["mla_wkv"], B, L, past)
    x = _out_proj(o, W["mla_w_out"], x, nrow(1, 1), tm)
    x, conv1 = _ffn(x, nrow(1, 2), ffn_bf[1][0], W["ffn_conv_w"][1], W["ffn_conv_b"][1],
                    ffn_bf[1][1], nrow(1, 3), None if conv_state is None else conv_state[1],
                    B, L, tm_ffn)

    return (x.reshape(B, L, D), hg_new[None], ckv.reshape(1, B, L, KV_LORA),
            kpe_pad[:, :ROPE_DIM].reshape(1, B, L, ROPE_DIM), jnp.stack([conv0, conv1]), ffn_bf)


def kernel(x_prompt, x_sample, state_hgrn, cache_ckv, cache_kpe, state_conv, norm_w, lb_logits,
           hgrn_w_in, hgrn_gnorm_w, hgrn_w_out, mla_w_down, mla_q_norm_w, mla_kv_norm_w,
           mla_w_uq, mla_w_ukv, mla_w_out, ffn_w_in, ffn_conv_w, ffn_conv_b, ffn_w_out):
    wd, wq, wkv = _prep_mla_weights(mla_w_down[0], mla_w_uq[0], mla_w_ukv[0])
    W = dict(
        norm_w=norm_w, lb_logits=lb_logits,
        hgrn_w_in=hgrn_w_in[0].astype(BF16), hgrn_gnorm_w=hgrn_gnorm_w[0],
        hgrn_w_out=hgrn_w_out[0].astype(BF16),
        mla_wd=wd, mla_wq=wq, mla_wkv=wkv,
        mla_q_norm_w=mla_q_norm_w[0], mla_kv_norm_w=mla_kv_norm_w[0],
        mla_w_out=mla_w_out[0].astype(BF16),
        ffn_conv_w=ffn_conv_w, ffn_conv_b=ffn_conv_b,
        ffn_f32=(ffn_w_in, ffn_w_out),
    )
    Bp, Lp, D = x_prompt.shape
    Bs, Ls, _ = x_sample.shape
    past = cache_ckv.shape[2]
    cos_t, sin_t = _rope_tables(max(Lp, past + Ls))

    yp, hgp, ckvp, kpep, convp, ffn_bf = _trunk(
        x_prompt.reshape(Bp * Lp, D), Bp, Lp, None, None, None, None, W, cos_t[:Lp], sin_t[:Lp])

    cos_s = jnp.tile(cos_t[past:past + Ls], (Bs, 1))
    sin_s = jnp.tile(sin_t[past:past + Ls], (Bs, 1))
    ys, hgs, ckvs, kpes, convs, _ = _trunk(
        x_sample.reshape(Bs * Ls, D), Bs, Ls, state_hgrn[0], cache_ckv[0], cache_kpe[0],
        state_conv, W, cos_s, sin_s, ffn_bf)
    return (yp, ys, hgp, hgs, ckvp, ckvs, kpep, kpes, convp, convs)
```

```python
import functools
import math

import jax
import jax.numpy as jnp
from jax import lax
from jax.experimental import pallas as pl
from jax.experimental.pallas import tpu as pltpu

F32 = jnp.float32
BF16 = jnp.bfloat16

EPS = 1e-6
LANES = 128
BF16_ROWS = 16
CHUNK = 64
HG_HEADS = 16
HG_DK = 128
HG_DV = 128
MLA_HEADS = 16
Q_LORA = 512
KV_LORA = 512
NOPE_DIM = 128
ROPE_DIM = 64
V_DIM = 128
ROPE_BASE = 10000.0
D_FF = 5632
FF_TILE = 512
TOKEN_TILE = 512
FFN_TOKEN_TILE = 1024
MLA_TOKEN_TILE = 256
IN_PROJ_COLS = 2048
ATTN_Q_TILE = 256
REC_HEADS = 4
REC_ROWS = 1024
SCORES_AHEAD = 2
QK_PAD = 256
VMEM_LIMIT = 56 * 1024 * 1024


def _cparams(sem):
    return pltpu.CompilerParams(dimension_semantics=sem, vmem_limit_bytes=VMEM_LIMIT)


def _rms(x, w):
    ms = jnp.mean(x * x, axis=-1, keepdims=True)
    return x * lax.rsqrt(ms + EPS) * w


def _sigmoid(x):
    return 0.5 * jnp.tanh(0.5 * x) + 0.5


def _dot(a, b):
    return jnp.dot(a, b, preferred_element_type=F32)


def _dot_nt(a, b):
    return lax.dot_general(a, b, (((1,), (1,)), ((), ())), preferred_element_type=F32)


def _dot_tn(a, b):
    return lax.dot_general(a, b, (((0,), (0,)), ((), ())), preferred_element_type=F32)


def _side_cast_specs(srcs, layer, n_steps, step):
    in_specs, out_specs, out_shapes, holds = [], [], [], []
    for w in srcs:
        _, R, C = w.shape
        hold = 1
        while (R * hold) % (n_steps * BF16_ROWS) != 0:
            hold *= 2
            assert hold <= n_steps, (R, n_steps)
        rows = R * hold // n_steps

        def slab(*g, hold=hold):
            return step(*g) // hold

        in_specs.append(pl.BlockSpec((None, rows, C), lambda *g, slab=slab: (layer, slab(*g), 0)))
        out_specs.append(pl.BlockSpec((rows, C), lambda *g, slab=slab: (slab(*g), 0)))
        out_shapes.append(jax.ShapeDtypeStruct((R, C), BF16))
        holds.append(hold)
    return in_specs, out_specs, out_shapes, tuple(holds)


def _side_cast(src_refs, dst_refs, holds, grid_rank=2):
    step = pl.program_id(0)
    for d in range(1, grid_rank):
        step = step * pl.num_programs(d) + pl.program_id(d)
    for src, dst, hold in zip(src_refs, dst_refs, holds):
        @pl.when(step % hold == 0)
        def _():
            dst[...] = src[...].astype(BF16)


def _hgrn_in_kernel(*refs, lb_row, blk_per_sec, side_holds):
    n_side = len(side_holds)
    x_ref, nw_ref, w_ref, lb_ref = refs[:4]
    o_ref, h_ref = refs[4 + n_side], refs[-1]
    _side_cast(refs[4:4 + n_side], refs[5 + n_side:5 + 2 * n_side], side_holds)
    j = pl.program_id(1)

    @pl.when(j == 0)
    def _():
        h_ref[...] = _rms(x_ref[...], nw_ref[...]).astype(BF16)

    sec = j // blk_per_sec
    heads = o_ref.shape[0]
    lg = lb_ref[...]
    e = jnp.exp(lg - jnp.max(lg, axis=0, keepdims=True))
    sm = e / jnp.sum(e, axis=0, keepdims=True)
    lb_all = jnp.sum(sm[0:lb_row + 1], axis=0, keepdims=True)
    hp = heads // 2
    for c0 in (0, hp):
        cols = slice(c0 * LANES, (c0 + hp) * LANES)
        r = _dot(h_ref[...], w_ref[:, cols])
        lb = lb_all[:, cols]
        sig = _sigmoid(r)
        val = jnp.where(sec == 2, r, jnp.where(sec == 1, lb + (1.0 - lb) * sig, r * sig))
        for hh in range(hp):
            o_ref[c0 + hh] = val[:, hh * LANES:(hh + 1) * LANES]


def _hgrn_in(x, nw, w_bf, lb_logits, lb_row, tm, side=(), side_layer=0):
    T, D = x.shape
    N = w_bf.shape[1]
    tn = IN_PROJ_COLS
    sec_w = N // 4
    blk_per_sec = sec_w // tn
    heads = tn // LANES
    nj = N // tn
    s_in, s_out, s_shape, holds = _side_cast_specs(side, side_layer, (T // tm) * nj,
                                                   lambda i, j: i * nj + j)
    outs = pl.pallas_call(
        functools.partial(_hgrn_in_kernel, lb_row=lb_row, blk_per_sec=blk_per_sec,
                          side_holds=holds),
        grid=(T // tm, nj),
        in_specs=[
            pl.BlockSpec((tm, D), lambda i, j: (i, 0)),
            pl.BlockSpec((1, D), lambda i, j: (0, 0)),
            pl.BlockSpec((D, tn), lambda i, j: (0, j)),
            pl.BlockSpec((lb_logits.shape[0], tn), lambda i, j: (0, j % blk_per_sec)),
        ] + s_in,
        out_specs=[pl.BlockSpec((heads, tm, LANES), lambda i, j: (j, i, 0))] + s_out,
        out_shape=[jax.ShapeDtypeStruct((N // LANES, T, LANES), F32)] + s_shape,
        scratch_shapes=[pltpu.VMEM((tm, D), BF16)],
        compiler_params=_cparams(("arbitrary", "arbitrary")),
        name="hgrn_in_proj",
    )(x, nw, w_bf, lb_logits, *side)
    return outs[0], outs[1:]


SUB = 8


def _bcast_mid_in_tile(c, s, t8):
    if s == 4:
        return jnp.broadcast_to(c[s - 1:s, :], c.shape)
    dn1 = pltpu.roll(c, 1, 0)
    if s == 1:
        return jnp.where((t8 & 1) == 1, dn1, c)
    ph = t8 & 3
    up1 = pltpu.roll(c, SUB - 1, 0)
    dn2 = pltpu.roll(c, 2, 0)
    return jnp.where(ph == 0, up1, jnp.where(ph == 1, c, jnp.where(ph == 2, dn1, dn2)))


def _gla_masks(C):
    t8 = lax.broadcasted_iota(jnp.int32, (SUB, LANES), 0)
    rr = lax.broadcasted_iota(jnp.int32, (C, C), 0)
    cc = lax.broadcasted_iota(jnp.int32, (C, C), 1)
    pairs = []
    s = 1
    while s < C:
        sh = int(math.log2(2 * s))
        pairs.append(((rr >> sh) == (cc >> sh)) & ((rr & s) != 0) & ((cc & s) == 0))
        s *= 2
    return t8, pairs


def _gla_head(q, fg, v, masks):
    C = q.shape[0]
    nt = C // SUB
    t8, pairs = masks
    k = 1.0 - fg
    tiles = lambda x: [x[i * SUB:(i + 1) * SUB] for i in range(nt)]
    qs, ks = tiles(q), tiles(k)
    cs = [jnp.log2(f) for f in tiles(fg)]
    zero = jnp.zeros((SUB, LANES), F32)
    a = jnp.zeros((C, C), F32)
    s = 1
    for pair in pairs:
        if s < SUB:
            upper = (t8 & s) != 0
            qd, kd = [], []
            for i in range(nt):
                bc = _bcast_mid_in_tile(cs[i], s, t8)
                qd.append(qs[i] * jnp.exp2(cs[i]))
                kd.append(ks[i] * jnp.exp2(jnp.minimum(bc - cs[i], 0.0)))
                cs[i] = cs[i] + jnp.where(upper, bc, 0.0)
        else:
            m = s // SUB
            qd, kd = [zero] * nt, [zero] * nt
            for blk in range(0, nt, 2 * m):
                bc = jnp.broadcast_to(cs[blk + m - 1][SUB - 1:SUB, :], (SUB, LANES))
                for i in range(blk, blk + m):
                    kd[i] = ks[i] * jnp.exp2(bc - cs[i])
                for i in range(blk + m, blk + 2 * m):
                    qd[i] = qs[i] * jnp.exp2(cs[i])
                    cs[i] = cs[i] + bc
        a_s = _dot_nt(jnp.concatenate(qd, axis=0).astype(BF16),
                      jnp.concatenate(kd, axis=0).astype(BF16))
        a = jnp.where(pair, a_s, a)
        s *= 2
    b = jnp.concatenate(cs, axis=0)
    b_last = b[C - 1:C, :]
    diag = jnp.sum(q * k, axis=-1, keepdims=True)
    qb = (q * jnp.exp2(b)).astype(BF16)
    kdec = (k * jnp.exp2(b_last - b)).astype(BF16)
    return a.astype(BF16), qb, kdec, diag * v, jnp.exp2(b_last)


def _gla_tail(a, qb, kdec, dv, e_last, v_bf, st):
    o = _dot(a, v_bf) + dv + _dot_nt(qb, st.astype(BF16))
    st_new = st * e_last + _dot_tn(v_bf, kdec)
    return o, st_new


def _hgrn_rec_kernel(*refs, C, n_chunks, has_state, side_holds):
    n_side = len(side_holds)
    n_in = 6 if has_state else 5
    q_ref, f_ref, v_ref, g_ref, gw_ref = refs[:5]
    if has_state:
        s0_ref = refs[5]
    z_ref, so_ref = refs[n_in + n_side:n_in + n_side + 2]
    st_ref, a_buf, qb_buf, kd_buf, dv_buf, el_buf = refs[n_in + 2 * n_side + 2:]
    _side_cast(refs[n_in:n_in + n_side],
               refs[n_in + n_side + 2:n_in + 2 * n_side + 2], side_holds, grid_rank=3)
    li = pl.program_id(2)
    hb = st_ref.shape[0]

    @pl.when(li == 0)
    def _():
        for hh in range(hb):
            st_ref[hh] = s0_ref[0, hh].T if has_state else jnp.zeros((HG_DV, HG_DK), F32)

    gw = gw_ref[...]
    masks = _gla_masks(C)

    def rows_of(ci):
        return pl.ds(pl.multiple_of(ci * C, C), C)

    def head(ci, slot):
        rows = rows_of(ci)
        for hh in range(hb):
            a, qb, kdec, dv, e_last = _gla_head(q_ref[hh, rows, :], f_ref[hh, rows, :],
                                                v_ref[hh, rows, :], masks)
            a_buf[slot, hh] = a
            qb_buf[slot, hh] = qb
            kd_buf[slot, hh] = kdec
            dv_buf[slot, hh] = dv
            el_buf[slot, hh] = e_last

    def tail(ci, slot):
        rows = rows_of(ci)
        for hh in range(hb):
            o, st_new = _gla_tail(a_buf[slot, hh], qb_buf[slot, hh], kd_buf[slot, hh],
                                  dv_buf[slot, hh], el_buf[slot, hh],
                                  v_ref[hh, rows, :].astype(BF16), st_ref[hh])
            st_ref[hh] = st_new
            z_ref[rows, hh * LANES:(hh + 1) * LANES] = (
                _rms(o, gw) * g_ref[hh, rows, :]).astype(BF16)

    head(0, 0)
    if n_chunks > 1:
        assert n_chunks % 2 == 0

        def body(kk, carry):
            c0 = 2 * kk
            tail(c0, 0)
            head(c0 + 1, 1)
            tail(c0 + 1, 1)
            head(c0 + 2, 0)
            return carry

        lax.fori_loop(0, n_chunks // 2 - 1, body, 0)
        tail(n_chunks - 2, 0)
        head(n_chunks - 1, 1)
        tail(n_chunks - 1, 1)
    else:
        tail(0, 0)

    @pl.when(li == pl.num_programs(2) - 1)
    def _():
        for hh in range(hb):
            so_ref[0, hh] = st_ref[hh].T


def _hgrn_rec(p, gnorm_w, s0, B, L, side=(), side_layer=0):
    H = HG_HEADS
    T = B * L
    C = min(CHUNK, L)
    hb = REC_HEADS if L > CHUNK else H
    lb = min(L, REC_ROWS)
    nl = L // lb
    ng = H // hb
    has_state = s0 is not None

    def sec_spec(sec):
        return pl.BlockSpec((hb, lb, LANES), lambda b, g, l: (sec * ng + g, b * nl + l, 0))

    in_specs = [sec_spec(0), sec_spec(1), sec_spec(2), sec_spec(3),
                pl.BlockSpec((1, HG_DV), lambda b, g, l: (0, 0))]
    args = [p, p, p, p, gnorm_w]
    if has_state:
        in_specs.append(pl.BlockSpec((1, hb, HG_DK, HG_DV), lambda b, g, l: (b, g, 0, 0)))
        args.append(s0)
    s_in, s_out, s_shape, holds = _side_cast_specs(
        side, side_layer, B * ng * nl, lambda b, g, l: (b * ng + g) * nl + l)
    outs = pl.pallas_call(
        functools.partial(_hgrn_rec_kernel, C=C, n_chunks=lb // C, has_state=has_state,
                          side_holds=holds),
        grid=(B, ng, nl),
        in_specs=in_specs + s_in,
        out_specs=[
            pl.BlockSpec((lb, hb * LANES), lambda b, g, l: (b * nl + l, g)),
            pl.BlockSpec((1, hb, HG_DK, HG_DV), lambda b, g, l: (b, g, 0, 0)),
        ] + s_out,
        out_shape=[
            jax.ShapeDtypeStruct((T, H * HG_DV), BF16),
            jax.ShapeDtypeStruct((B, H, HG_DK, HG_DV), F32),
        ] + s_shape,
        scratch_shapes=[
            pltpu.VMEM((hb, HG_DV, HG_DK), F32),
            pltpu.VMEM((2, hb, C, C), BF16),
            pltpu.VMEM((2, hb, C, HG_DK), BF16),
            pltpu.VMEM((2, hb, C, HG_DK), BF16),
            pltpu.VMEM((2, hb, C, HG_DV), F32),
            pltpu.VMEM((2, hb, 1, HG_DK), F32),
        ],
        compiler_params=_cparams(("arbitrary", "arbitrary", "arbitrary")),
        name="hgrn_recurrence",
    )(*args, *side)
    return outs[0], outs[1], outs[2:]


def _out_proj_kernel(a_ref, w_ref, x_ref, nw_ref, o_ref):
    y = _dot(a_ref[...], w_ref[...])
    o_ref[...] = x_ref[...] + _rms(y, nw_ref[...])


def _out_proj(a_bf, w_bf, x, nw, tm):
    T, K = a_bf.shape
    N = w_bf.shape[1]
    return pl.pallas_call(
        _out_proj_kernel,
        grid=(T // tm,),
        in_specs=[
            pl.BlockSpec((tm, K), lambda i: (i, 0)),
            pl.BlockSpec((K, N), lambda i: (0, 0)),
            pl.BlockSpec((tm, N), lambda i: (i, 0)),
            pl.BlockSpec((1, N), lambda i: (0, 0)),
        ],
        out_specs=pl.BlockSpec((tm, N), lambda i: (i, 0)),
        out_shape=jax.ShapeDtypeStruct((T, N), F32),
        compiler_params=_cparams(("arbitrary",)),
        name="mixer_out_proj",
    )(a_bf, w_bf, x, nw)


def _ffn_kernel(*refs, ns, ls, tps, nj, has_state):
    if has_state:
        (x_ref, nw_ref, wg_ref, wu_ref, cwg_ref, cwu_ref, cbg_ref, cbu_ref, wo_ref, nw2_ref,
         sg_ref, su_ref, o_ref, csg_ref, csu_ref, h_ref, work_ref, carry_ref, act_ref) = refs
    else:
        (x_ref, nw_ref, wg_ref, wu_ref, cwg_ref, cwu_ref, cbg_ref, cbu_ref, wo_ref, nw2_ref,
         o_ref, csg_ref, csu_ref, h_ref, work_ref, carry_ref, act_ref) = refs
        sg_ref = su_ref = None
    i = pl.program_id(0)
    j = pl.program_id(1)
    tf = wg_ref.shape[1]

    def conv(half, w_ref, cw_ref, cb_ref, s_ref, cs_ref):
        u3 = _dot(h_ref[...], w_ref[...]).reshape(ns, ls, tf)
        work_ref[half, :, 8:8 + ls, :] = u3
        prev = jnp.zeros((ns, 2, tf), F32) if s_ref is None else s_ref[...]
        if tps > 1:
            prev = jnp.where((i % tps) == 0, prev, carry_ref[j, half])
        work_ref[half, :, 6:8, :] = prev
        x1 = work_ref[half, :, 7:7 + ls, :]
        x2 = work_ref[half, :, 6:6 + ls, :]
        cw = cw_ref[...]
        c = cb_ref[...] + cw[0:1] * x2 + cw[1:2] * x1 + cw[2:3] * u3
        tail = u3[:, ls - 2:ls, :]
        cs_ref[...] = tail
        if tps > 1:
            carry_ref[j, half] = tail
        return c.reshape(ns * ls, tf)

    def up_conv_act(slot):
        cg = conv(0, wg_ref, cwg_ref, cbg_ref, sg_ref, csg_ref)
        cu = conv(1, wu_ref, cwu_ref, cbu_ref, su_ref, csu_ref)
        act_ref[slot] = (cg * _sigmoid(cg) * cu).astype(BF16)

    @pl.when(j == 0)
    def _():
        h_ref[...] = _rms(x_ref[...], nw_ref[...]).astype(BF16)
        o_ref[...] = jnp.zeros_like(o_ref)
        up_conv_act(0)

    for par in range(2):
        @pl.when((j > 0) & (j < nj) & (j % 2 == par))
        def _():
            o_ref[...] += _dot(act_ref[1 - par], wo_ref[...])
            up_conv_act(par)

    @pl.when(j == nj)
    def _():
        y = o_ref[...] + _dot(act_ref[(nj - 1) % 2], wo_ref[...])
        o_ref[...] = x_ref[...] + _rms(y, nw2_ref[...])


def _ffn(x, nw, w_in_bf, conv_w, conv_b, w_out_bf, nw2, state, n_streams, L, tm):
    T, D = x.shape
    tf = FF_TILE
    nj = D_FF // tf
    if tm >= L:
        assert tm % L == 0
        ns, ls, tps = tm // L, L, 1
    else:
        assert L % tm == 0
        ns, ls, tps = 1, tm, L // tm
    has_state = state is not None
    cb2 = conv_b.reshape(1, 2 * D_FF)

    def stream_blk(i):
        return (i * tm) // L // ns if ns > 1 else (i * tm) // L

    up = lambda j: jnp.minimum(j, nj - 1)
    dn = lambda j: jnp.maximum(j - 1, 0)
    in_specs = [
        pl.BlockSpec((tm, D), lambda i, j: (i, 0), pipeline_mode=pl.Buffered(1)),
        pl.BlockSpec((1, D), lambda i, j: (0, 0)),
        pl.BlockSpec((D, tf), lambda i, j: (0, up(j))),
        pl.BlockSpec((D, tf), lambda i, j: (0, nj + up(j))),
        pl.BlockSpec((3, tf), lambda i, j: (0, up(j))),
        pl.BlockSpec((3, tf), lambda i, j: (0, nj + up(j))),
        pl.BlockSpec((1, tf), lambda i, j: (0, up(j))),
        pl.BlockSpec((1, tf), lambda i, j: (0, nj + up(j))),
        pl.BlockSpec((tf, D), lambda i, j: (dn(j), 0)),
        pl.BlockSpec((1, D), lambda i, j: (0, 0)),
    ]
    args = [x, nw, w_in_bf, w_in_bf, conv_w, conv_w, cb2, cb2, w_out_bf, nw2]
    if has_state:
        in_specs += [
            pl.BlockSpec((ns, 2, tf), lambda i, j: (stream_blk(i), 0, up(j))),
            pl.BlockSpec((ns, 2, tf), lambda i, j: (stream_blk(i), 0, nj + up(j))),
        ]
        args += [state, state]
    out, csg, csu = pl.pallas_call(
        functools.partial(_ffn_kernel, ns=ns, ls=ls, tps=tps, nj=nj, has_state=has_state),
        grid=(T // tm, nj + 1),
        in_specs=in_specs,
        out_specs=[
            pl.BlockSpec((tm, D), lambda i, j: (i, 0)),
            pl.BlockSpec((ns, 2, tf), lambda i, j: (i, 0, up(j))),
            pl.BlockSpec((ns, 2, tf), lambda i, j: (i, 0, up(j))),
        ],
        out_shape=[
            jax.ShapeDtypeStruct((T, D), F32),
            jax.ShapeDtypeStruct((T // ls, 2, D_FF), F32),
            jax.ShapeDtypeStruct((T // ls, 2, D_FF), F32),
        ],
        scratch_shapes=[
            pltpu.VMEM((tm, D), BF16),
            pltpu.VMEM((2, ns, 8 + ls, tf), F32),
            pltpu.VMEM((nj, 2, ns, 2, tf), F32),
            pltpu.VMEM((2, tm, tf), BF16),
        ],
        compiler_params=_cparams(("arbitrary", "arbitrary")),
        name="conv_ffn",
    )(*args)
    tails = jnp.concatenate([csg, csu], axis=-1).reshape(n_streams, tps, 2, 2 * D_FF)
    return out, tails[:, tps - 1]


def _rope_table_kernel(c_ref, s_ref):
    shape = c_ref.shape
    half = ROPE_DIM // 2
    pos = lax.broadcasted_iota(jnp.int32, shape, 0).astype(F32)
    lane = lax.broadcasted_iota(jnp.int32, shape, 1)
    fi = (lane & (half - 1)).astype(F32)
    inv = jnp.exp(fi * (-math.log(ROPE_BASE) / half))
    ang = pos * inv
    valid = lane < ROPE_DIM
    c_ref[...] = jnp.where(valid, jnp.cos(ang), 0.0)
    s_ref[...] = jnp.where(valid, jnp.where(lane < half, -jnp.sin(ang), jnp.sin(ang)), 0.0)


def _rope_tables(n_pos):
    return pl.pallas_call(
        _rope_table_kernel,
        out_shape=[jax.ShapeDtypeStruct((n_pos, LANES), F32)] * 2,
        name="rope_tables",
    )()


def _mla_in_kernel(x_ref, nw_ref, wd_ref, qnw_ref, kvnw_ref, wq_ref, c_ref, s_ref,
                   qcat_ref, ckv_ref, kpe_ref):
    h = _rms(x_ref[...], nw_ref[...]).astype(BF16)
    d = _dot(h, wd_ref[...])
    cq = _rms(d[:, :Q_LORA], qnw_ref[...]).astype(BF16)
    ckv_ref[...] = _rms(d[:, Q_LORA:Q_LORA + KV_LORA], kvnw_ref[...])
    cs = c_ref[...]
    sn = s_ref[...]
    o = Q_LORA + KV_LORA
    kpe_ref[...] = d[:, o:o + LANES] * cs + d[:, o + LANES:o + 2 * LANES] * sn
    hw = MLA_HEADS * LANES
    qn = _dot(cq, wq_ref[:, 0:hw])
    pr = _dot(cq, wq_ref[:, hw:2 * hw])
    ps = _dot(cq, wq_ref[:, 2 * hw:3 * hw])
    for hh in range(MLA_HEADS):
        sl = slice(hh * LANES, (hh + 1) * LANES)
        qcat_ref[hh, :, 0:LANES] = qn[:, sl].astype(BF16)
        qcat_ref[hh, :, LANES:2 * LANES] = (pr[:, sl] * cs + ps[:, sl] * sn).astype(BF16)


def _mla_in(x, nw, wd_bf, qnw, kvnw, wq_bf, cos_t, sin_t, tm):
    T, D = x.shape
    n_tab_blk = cos_t.shape[0] // tm
    return pl.pallas_call(
        _mla_in_kernel,
        grid=(T // tm,),
        in_specs=[
            pl.BlockSpec((tm, D), lambda i: (i, 0)),
            pl.BlockSpec((1, D), lambda i: (0, 0)),
            pl.BlockSpec(wd_bf.shape, lambda i: (0, 0)),
            pl.BlockSpec((1, Q_LORA), lambda i: (0, 0)),
            pl.BlockSpec((1, KV_LORA), lambda i: (0, 0)),
            pl.BlockSpec(wq_bf.shape, lambda i: (0, 0)),
            pl.BlockSpec((tm, LANES), lambda i: (i % n_tab_blk, 0)),
            pl.BlockSpec((tm, LANES), lambda i: (i % n_tab_blk, 0)),
        ],
        out_specs=[
            pl.BlockSpec((MLA_HEADS, tm, QK_PAD), lambda i: (0, i, 0)),
            pl.BlockSpec((tm, KV_LORA), lambda i: (i, 0)),
            pl.BlockSpec((tm, LANES), lambda i: (i, 0)),
        ],
        out_shape=[
            jax.ShapeDtypeStruct((MLA_HEADS, T, QK_PAD), BF16),
            jax.ShapeDtypeStruct((T, KV_LORA), F32),
            jax.ShapeDtypeStruct((T, LANES), F32),
        ],
        compiler_params=_cparams(("arbitrary",)),
        name="mla_in_proj",
    )(x, nw, wd_bf, qnw, kvnw, wq_bf, cos_t, sin_t)


def _kv_up_kernel(ckv_ref, kpe_ref, wkv_ref, kcat_ref, v_ref):
    kv = _dot(ckv_ref[...].astype(BF16), wkv_ref[...])
    kp = kpe_ref[...].astype(BF16)
    hw = NOPE_DIM + V_DIM
    for hh in range(MLA_HEADS):
        kcat_ref[hh, :, 0:LANES] = kv[:, hh * hw:hh * hw + NOPE_DIM].astype(BF16)
        kcat_ref[hh, :, LANES:2 * LANES] = kp
        v_ref[hh] = kv[:, hh * hw + NOPE_DIM:(hh + 1) * hw].astype(BF16)


def _kv_up(ckv, kpe_pad, wkv_bf, tm):
    R = ckv.shape[0]
    return pl.pallas_call(
        _kv_up_kernel,
        grid=(R // tm,),
        in_specs=[
            pl.BlockSpec((tm, KV_LORA), lambda i: (i, 0)),
            pl.BlockSpec((tm, LANES), lambda i: (i, 0)),
            pl.BlockSpec(wkv_bf.shape, lambda i: (0, 0)),
        ],
        out_specs=[
            pl.BlockSpec((MLA_HEADS, tm, QK_PAD), lambda i: (0, i, 0)),
            pl.BlockSpec((MLA_HEADS, tm, V_DIM), lambda i: (0, i, 0)),
        ],
        out_shape=[
            jax.ShapeDtypeStruct((MLA_HEADS, R, QK_PAD), BF16),
            jax.ShapeDtypeStruct((MLA_HEADS, R, V_DIM), BF16),
        ],
        compiler_params=_cparams(("arbitrary",)),
        name="mla_kv_up_proj",
    )(ckv, kpe_pad, wkv_bf)


def _attn_prompt_kernel(*refs, L, tq, scale, side_holds):
    n_side = len(side_holds)
    q_ref, k_ref, v_ref = refs[:3]
    o_ref = refs[3 + n_side]
    _side_cast(refs[3:3 + n_side], refs[4 + n_side:], side_holds)
    row = lax.broadcasted_iota(jnp.int32, (tq, tq), 0)
    col = lax.broadcasted_iota(jnp.int32, (tq, tq), 1)
    sh = int(math.log2(CHUNK))
    dmask = (row >> sh) >= (col >> sh)

    c2 = scale * math.log2(math.e)
    def scores(qi):
        n = qi * tq
        q = q_ref[0, n:n + tq, :]
        s_d = jnp.where(dmask, _dot_nt(q, k_ref[0, n:n + tq, :]), -jnp.inf)
        s_p = _dot_nt(q, k_ref[0, 0:n, :]) if qi > 0 else None
        return s_d, s_p

    nq = L // tq
    pending = [scores(t) for t in range(min(SCORES_AHEAD, nq))]
    for qi in range(nq):
        n = qi * tq
        s_d, s_p = pending.pop(0)
        if qi + SCORES_AHEAD < nq:
            pending.append(scores(qi + SCORES_AHEAD))
        m = jnp.max(s_d, axis=-1, keepdims=True)
        if qi > 0:
            m = jnp.maximum(m, jnp.max(s_p, axis=-1, keepdims=True))
        p_d = jnp.exp2((s_d - m) * c2)
        l = jnp.sum(p_d, axis=-1, keepdims=True)
        acc = _dot(p_d.astype(BF16), v_ref[0, n:n + tq, :])
        if qi > 0:
            p_p = jnp.exp2((s_p - m) * c2)
            l = l + jnp.sum(p_p, axis=-1, keepdims=True)
            acc = acc + _dot(p_p.astype(BF16), v_ref[0, 0:n, :])
        o_ref[n:n + tq, :] = (acc / l).astype(BF16)


def _attn_prompt(qcat, kcat, v, B, L, side=(), side_layer=0):
    H = MLA_HEADS
    scale = (NOPE_DIM + ROPE_DIM) ** -0.5
    s_in, s_out, s_shape, holds = _side_cast_specs(side, side_layer, B * H,
                                                   lambda b, h: b * H + h)
    outs = pl.pallas_call(
        functools.partial(_attn_prompt_kernel, L=L, tq=ATTN_Q_TILE, scale=scale,
                          side_holds=holds),
        grid=(B, H),
        in_specs=[
            pl.BlockSpec((1, L, QK_PAD), lambda b, h: (h, b, 0)),
            pl.BlockSpec((1, L, QK_PAD), lambda b, h: (h, b, 0)),
            pl.BlockSpec((1, L, V_DIM), lambda b, h: (h, b, 0)),
        ] + s_in,
        out_specs=[pl.BlockSpec((L, V_DIM), lambda b, h: (b, h))] + s_out,
        out_shape=[jax.ShapeDtypeStruct((B * L, H * V_DIM), BF16)] + s_shape,
        compiler_params=_cparams(("arbitrary", "arbitrary")),
        name="attn_prompt",
    )(qcat, kcat, v, *side)
    return outs[0], outs[1:]


def _attn_sample_kernel(q_ref, cc_ref, pc_ref, cn_ref, pn_ref, wkv_ref, o_ref, *, past, scale):
    H, Lq, _ = q_ref.shape
    hw = NOPE_DIM + V_DIM
    qlat, qrope = [], []
    for hh in range(H):
        wk_h = wkv_ref[:, hh * hw:hh * hw + NOPE_DIM]
        qlat.append(_dot_nt(q_ref[hh, :, 0:NOPE_DIM], wk_h).astype(BF16))
        qrope.append(q_ref[hh, :, NOPE_DIM:NOPE_DIM + ROPE_DIM])
    qlat = jnp.concatenate(qlat, axis=0)
    qrope = jnp.concatenate(qrope, axis=0)
    cc = cc_ref[0].astype(BF16)
    cn = cn_ref[...].astype(BF16)
    s1 = _dot_nt(qlat, cc) + _dot_nt(qrope, pc_ref[0].astype(BF16))
    s2 = _dot_nt(qlat, cn) + _dot_nt(qrope, pn_ref[:, 0:ROPE_DIM].astype(BF16))
    sh = int(math.log2(CHUNK))

    def q_chunk(n):
        t = lax.broadcasted_iota(jnp.int32, (H, Lq, n), 1).reshape(H * Lq, n)
        return (t + past) >> sh

    kc1 = lax.broadcasted_iota(jnp.int32, s1.shape, 1) >> sh
    kc2 = (lax.broadcasted_iota(jnp.int32, s2.shape, 1) + past) >> sh
    s1 = jnp.where(kc1 <= q_chunk(s1.shape[1]), s1 * scale, -jnp.inf)
    s2 = jnp.where(kc2 <= q_chunk(s2.shape[1]), s2 * scale, -jnp.inf)
    m = jnp.maximum(jnp.max(s1, axis=-1, keepdims=True), jnp.max(s2, axis=-1, keepdims=True))
    p1 = jnp.exp(s1 - m)
    p2 = jnp.exp(s2 - m)
    l = jnp.sum(p1, axis=-1, keepdims=True) + jnp.sum(p2, axis=-1, keepdims=True)
    olat = ((_dot(p1.astype(BF16), cc) + _dot(p2.astype(BF16), cn)) / l).astype(BF16)
    for hh in range(H):
        sl = slice(hh * LANES, (hh + 1) * LANES)
        wv_h = wkv_ref[:, hh * hw + NOPE_DIM:(hh + 1) * hw]
        o_ref[:, sl] = _dot(olat[hh * Lq:(hh + 1) * Lq], wv_h).astype(BF16)


def _attn_sample(qcat, ckv_cache, kpe_cache, ckv_new, kpe_new, wkv_bf, B, L, past):
    H = MLA_HEADS
    scale = (NOPE_DIM + ROPE_DIM) ** -0.5
    return pl.pallas_call(
        functools.partial(_attn_sample_kernel, past=past, scale=scale),
        grid=(B,),
        in_specs=[
            pl.BlockSpec((H, L, QK_PAD), lambda b: (0, b, 0)),
            pl.BlockSpec((1, past, KV_LORA), lambda b: (b, 0, 0)),
            pl.BlockSpec((1, past, ROPE_DIM), lambda b: (b, 0, 0)),
            pl.BlockSpec((L, KV_LORA), lambda b: (b, 0)),
            pl.BlockSpec((L, LANES), lambda b: (b, 0)),
            pl.BlockSpec(wkv_bf.shape, lambda b: (0, 0)),
        ],
        out_specs=pl.BlockSpec((L, H * V_DIM), lambda b: (b, 0)),
        out_shape=jax.ShapeDtypeStruct((B * L, H * V_DIM), BF16),
        compiler_params=_cparams(("arbitrary",)),
        name="attn_sample",
    )(qcat, ckv_cache, kpe_cache, ckv_new, kpe_new, wkv_bf)


def _swap_halves(w):
    half = w.shape[-1] // 2
    return jnp.concatenate([w[..., half:], w[..., :half]], axis=-1)


def _pad_lanes(w):
    return jnp.pad(w, [(0, 0)] * (w.ndim - 1) + [(0, LANES - w.shape[-1])])


def _prep_mla_weights(w_down, w_uq, w_ukv):
    o = Q_LORA + KV_LORA
    wpe = w_down[:, o:]
    wd = jnp.concatenate([w_down[:, :o], _pad_lanes(wpe), _pad_lanes(_swap_halves(wpe))], axis=-1)
    wq3 = w_uq.reshape(Q_LORA, MLA_HEADS, NOPE_DIM + ROPE_DIM)
    wn = wq3[..., :NOPE_DIM].reshape(Q_LORA, -1)
    wr = wq3[..., NOPE_DIM:]
    wrp = _pad_lanes(wr).reshape(Q_LORA, -1)
    wrs = _pad_lanes(_swap_halves(wr)).reshape(Q_LORA, -1)
    wq = jnp.concatenate([wn, wrp, wrs], axis=-1)
    return wd.astype(BF16), wq.astype(BF16), w_ukv.astype(BF16)


def _trunk(x, B, L, hg_state, ckv_cache, kpe_cache, conv_state, W, cos_t, sin_t, ffn_bf=None):
    norm_w = W["norm_w"]
    nrow = lambda l, k: norm_w[l, k].reshape(1, -1)
    past = 0 if ckv_cache is None else ckv_cache.shape[1]
    T = B * L
    D = x.shape[-1]
    tm, tm_ffn, tm_mla = min(T, TOKEN_TILE), min(T, FFN_TOKEN_TILE), min(T, MLA_TOKEN_TILE)
    side = () if ffn_bf is not None else W["ffn_f32"]
    ffn_bf = list(ffn_bf) if ffn_bf is not None else [None, None]

    p, _ = _hgrn_in(x, nrow(0, 0), W["hgrn_w_in"], W["lb_logits"], 0, tm)
    z, hg_new, cast = _hgrn_rec(p, W["hgrn_gnorm_w"].reshape(1, -1), hg_state, B, L, side, 0)
    if side:
        ffn_bf[0] = cast
    x = _out_proj(z, W["hgrn_w_out"], x, nrow(0, 1), tm)
    x, conv0 = _ffn(x, nrow(0, 2), ffn_bf[0][0], W["ffn_conv_w"][0], W["ffn_conv_b"][0],
                    ffn_bf[0][1], nrow(0, 3), None if conv_state is None else conv_state[0],
                    B, L, tm_ffn)

    qcat, ckv, kpe_pad = _mla_in(x, nrow(1, 0), W["mla_wd"], W["mla_q_norm_w"].reshape(1, -1),
                                 W["mla_kv_norm_w"].reshape(1, -1), W["mla_wq"], cos_t, sin_t,
                                 tm_mla)
    if ckv_cache is None:
        kcat_n, v_n = _kv_up(ckv, kpe_pad, W["mla_wkv"], tm_mla)
        o, cast = _attn_prompt(qcat, kcat_n, v_n, B, L, side, 1)
        if side:
            ffn_bf[1] = cast
    else:
        o = _attn_sample(qcat, ckv_cache, kpe_cache, ckv, kpe_pad, W["mla_wkv"], B, L, past)
    x = _out_proj(o, W["mla_w_out"], x, nrow(1, 1), tm)
    x, conv1 = _ffn(x, nrow(1, 2), ffn_bf[1][0], W["ffn_conv_w"][1], W["ffn_conv_b"][1],
                    ffn_bf[1][1], nrow(1, 3), None if conv_state is None else conv_state[1],
                    B, L, tm_ffn)

    return (x.reshape(B, L, D), hg_new[None], ckv.reshape(1, B, L, KV_LORA),
            kpe_pad[:, :ROPE_DIM].reshape(1, B, L, ROPE_DIM), jnp.stack([conv0, conv1]), ffn_bf)


def kernel(x_prompt, x_sample, state_hgrn, cache_ckv, cache_kpe, state_conv, norm_w, lb_logits,
           hgrn_w_in, hgrn_gnorm_w, hgrn_w_out, mla_w_down, mla_q_norm_w, mla_kv_norm_w,
           mla_w_uq, mla_w_ukv, mla_w_out, ffn_w_in, ffn_conv_w, ffn_conv_b, ffn_w_out):
    wd, wq, wkv = _prep_mla_weights(mla_w_down[0], mla_w_uq[0], mla_w_ukv[0])
    W = dict(
        norm_w=norm_w, lb_logits=lb_logits,
        hgrn_w_in=hgrn_w_in[0].astype(BF16), hgrn_gnorm_w=hgrn_gnorm_w[0],
        hgrn_w_out=hgrn_w_out[0].astype(BF16),
        mla_wd=wd, mla_wq=wq, mla_wkv=wkv,
        mla_q_norm_w=mla_q_norm_w[0], mla_kv_norm_w=mla_kv_norm_w[0],
        mla_w_out=mla_w_out[0].astype(BF16),
        ffn_conv_w=ffn_conv_w, ffn_conv_b=ffn_conv_b,
        ffn_f32=(ffn_w_in, ffn_w_out),
    )
    Bp, Lp, D = x_prompt.shape
    Bs, Ls, _ = x_sample.shape
    past = cache_ckv.shape[2]
    cos_t, sin_t = _rope_tables(max(Lp, past + Ls))

    yp, hgp, ckvp, kpep, convp, ffn_bf = _trunk(
        x_prompt.reshape(Bp * Lp, D), Bp, Lp, None, None, None, None, W, cos_t[:Lp], sin_t[:Lp])

    cos_s = jnp.tile(cos_t[past:past + Ls], (Bs, 1))
    sin_s = jnp.tile(sin_t[past:past + Ls], (Bs, 1))
    ys, hgs, ckvs, kpes, convs, _ = _trunk(
        x_sample.reshape(Bs * Ls, D), Bs, Ls, state_hgrn[0], cache_ckv[0], cache_kpe[0],
        state_conv, W, cos_s, sin_s, ffn_bf)
    return (yp, ys, hgp, hgs, ckvp, ckvs, kpep, kpes, convp, convs)
```

```python
import functools
import math

import jax
import jax.numpy as jnp
from jax import lax
from jax.experimental import pallas as pl
from jax.experimental.pallas import tpu as pltpu

F32 = jnp.float32
BF16 = jnp.bfloat16

EPS = 1e-6
LANES = 128
BF16_ROWS = 16
CHUNK = 64
HG_HEADS = 16
HG_DK = 128
HG_DV = 128
MLA_HEADS = 16
Q_LORA = 512
KV_LORA = 512
NOPE_DIM = 128
ROPE_DIM = 64
V_DIM = 128
ROPE_BASE = 10000.0
D_FF = 5632
FF_TILE = 512
TOKEN_TILE = 512
FFN_TOKEN_TILE = 1024
MLA_TOKEN_TILE = 256
IN_PROJ_COLS = 2048
ATTN_Q_TILE = 256
REC_HEADS = 4
REC_ROWS = 1024
SCORES_AHEAD = 2
QK_PAD = 256
VMEM_LIMIT = 56 * 1024 * 1024


def _cparams(sem):
    return pltpu.CompilerParams(dimension_semantics=sem, vmem_limit_bytes=VMEM_LIMIT)


def _rms(x, w):
    ms = jnp.mean(x * x, axis=-1, keepdims=True)
    return x * lax.rsqrt(ms + EPS) * w


def _sigmoid(x):
    return 0.5 * jnp.tanh(0.5 * x) + 0.5


def _dot(a, b):
    return jnp.dot(a, b, preferred_element_type=F32)


def _dot_nt(a, b):
    return lax.dot_general(a, b, (((1,), (1,)), ((), ())), preferred_element_type=F32)


def _dot_tn(a, b):
    return lax.dot_general(a, b, (((0,), (0,)), ((), ())), preferred_element_type=F32)


def _side_cast_specs(srcs, n_steps, step):
    in_specs, out_specs, out_shapes, holds = [], [], [], []
    for w, layer in srcs:
        _, R, C = w.shape
        hold = 1
        while (R * hold) % (n_steps * BF16_ROWS) != 0:
            hold *= 2
            assert hold <= n_steps, (R, n_steps)
        rows = R * hold // n_steps

        def slab(*g, hold=hold):
            return step(*g) // hold

        in_specs.append(pl.BlockSpec((None, rows, C),
                                     lambda *g, slab=slab, layer=layer: (layer, slab(*g), 0)))
        out_specs.append(pl.BlockSpec((rows, C), lambda *g, slab=slab: (slab(*g), 0)))
        out_shapes.append(jax.ShapeDtypeStruct((R, C), BF16))
        holds.append(hold)
    return in_specs, out_specs, out_shapes, tuple(holds)


def _side_cast(src_refs, dst_refs, holds, grid_rank=2):
    step = pl.program_id(0)
    for d in range(1, grid_rank):
        step = step * pl.num_programs(d) + pl.program_id(d)
    for src, dst, hold in zip(src_refs, dst_refs, holds):
        @pl.when(step % hold == 0)
        def _():
            dst[...] = src[...].astype(BF16)


def _hgrn_in_kernel(*refs, lb_row, blk_per_sec, side_holds):
    n_side = len(side_holds)
    x_ref, nw_ref, w_ref, lb_ref = refs[:4]
    o_ref, h_ref = refs[4 + n_side], refs[-1]
    _side_cast(refs[4:4 + n_side], refs[5 + n_side:5 + 2 * n_side], side_holds)
    j = pl.program_id(1)

    @pl.when(j == 0)
    def _():
        h_ref[...] = _rms(x_ref[...], nw_ref[...]).astype(BF16)

    sec = j // blk_per_sec
    heads = o_ref.shape[0]
    lg = lb_ref[...]
    e = jnp.exp(lg - jnp.max(lg, axis=0, keepdims=True))
    sm = e / jnp.sum(e, axis=0, keepdims=True)
    lb_all = jnp.sum(sm[0:lb_row + 1], axis=0, keepdims=True)
    hp = heads // 2
    for c0 in (0, hp):
        cols = slice(c0 * LANES, (c0 + hp) * LANES)
        r = _dot(h_ref[...], w_ref[:, cols])
        lb = lb_all[:, cols]
        sig = _sigmoid(r)
        val = jnp.where(sec == 2, r, jnp.where(sec == 1, lb + (1.0 - lb) * sig, r * sig))
        for hh in range(hp):
            o_ref[c0 + hh] = val[:, hh * LANES:(hh + 1) * LANES]


def _hgrn_in(x, nw, w_bf, lb_logits, lb_row, tm, side=()):
    T, D = x.shape
    N = w_bf.shape[1]
    tn = IN_PROJ_COLS
    sec_w = N // 4
    blk_per_sec = sec_w // tn
    heads = tn // LANES
    nj = N // tn
    s_in, s_out, s_shape, holds = _side_cast_specs(side, (T // tm) * nj,
                                                   lambda i, j: i * nj + j)
    outs = pl.pallas_call(
        functools.partial(_hgrn_in_kernel, lb_row=lb_row, blk_per_sec=blk_per_sec,
                          side_holds=holds),
        grid=(T // tm, nj),
        in_specs=[
            pl.BlockSpec((tm, D), lambda i, j: (i, 0)),
            pl.BlockSpec((1, D), lambda i, j: (0, 0)),
            pl.BlockSpec((D, tn), lambda i, j: (0, j)),
            pl.BlockSpec((lb_logits.shape[0], tn), lambda i, j: (0, j % blk_per_sec)),
        ] + s_in,
        out_specs=[pl.BlockSpec((heads, tm, LANES), lambda i, j: (j, i, 0))] + s_out,
        out_shape=[jax.ShapeDtypeStruct((N // LANES, T, LANES), F32)] + s_shape,
        scratch_shapes=[pltpu.VMEM((tm, D), BF16)],
        compiler_params=_cparams(("arbitrary", "arbitrary")),
        name="hgrn_in_proj",
    )(x, nw, w_bf, lb_logits, *[w for w, _ in side])
    return outs[0], outs[1:]


SUB = 8


def _bcast_mid_in_tile(c, s, t8):
    if s == 4:
        return jnp.broadcast_to(c[s - 1:s, :], c.shape)
    dn1 = pltpu.roll(c, 1, 0)
    if s == 1:
        return jnp.where((t8 & 1) == 1, dn1, c)
    ph = t8 & 3
    up1 = pltpu.roll(c, SUB - 1, 0)
    dn2 = pltpu.roll(c, 2, 0)
    return jnp.where(ph == 0, up1, jnp.where(ph == 1, c, jnp.where(ph == 2, dn1, dn2)))


def _gla_masks(C):
    t8 = lax.broadcasted_iota(jnp.int32, (SUB, LANES), 0)
    rr = lax.broadcasted_iota(jnp.int32, (C, C), 0)
    cc = lax.broadcasted_iota(jnp.int32, (C, C), 1)
    pairs = []
    s = 1
    while s < C:
        sh = int(math.log2(2 * s))
        pairs.append(((rr >> sh) == (cc >> sh)) & ((rr & s) != 0) & ((cc & s) == 0))
        s *= 2
    return t8, pairs


def _gla_head(q, fg, v, masks):
    C = q.shape[0]
    nt = C // SUB
    t8, pairs = masks
    k = 1.0 - fg
    tiles = lambda x: [x[i * SUB:(i + 1) * SUB] for i in range(nt)]
    qs, ks = tiles(q), tiles(k)
    cs = [jnp.log2(f) for f in tiles(fg)]
    zero = jnp.zeros((SUB, LANES), F32)
    a = jnp.zeros((C, C), F32)
    s = 1
    for pair in pairs:
        if s < SUB:
            upper = (t8 & s) != 0
            qd, kd = [], []
            for i in range(nt):
                bc = _bcast_mid_in_tile(cs[i], s, t8)
                qd.append(qs[i] * jnp.exp2(cs[i]))
                kd.append(ks[i] * jnp.exp2(jnp.minimum(bc - cs[i], 0.0)))
                cs[i] = cs[i] + jnp.where(upper, bc, 0.0)
        else:
            m = s // SUB
            qd, kd = [zero] * nt, [zero] * nt
            for blk in range(0, nt, 2 * m):
                bc = jnp.broadcast_to(cs[blk + m - 1][SUB - 1:SUB, :], (SUB, LANES))
                for i in range(blk, blk + m):
                    kd[i] = ks[i] * jnp.exp2(bc - cs[i])
                for i in range(blk + m, blk + 2 * m):
                    qd[i] = qs[i] * jnp.exp2(cs[i])
                    cs[i] = cs[i] + bc
        a_s = _dot_nt(jnp.concatenate(qd, axis=0).astype(BF16),
                      jnp.concatenate(kd, axis=0).astype(BF16))
        a = jnp.where(pair, a_s, a)
        s *= 2
    b = jnp.concatenate(cs, axis=0)
    b_last = b[C - 1:C, :]
    diag = jnp.sum(q * k, axis=-1, keepdims=True)
    qb = (q * jnp.exp2(b)).astype(BF16)
    kdec = (k * jnp.exp2(b_last - b)).astype(BF16)
    return a.astype(BF16), qb, kdec, diag * v, jnp.exp2(b_last)


def _gla_tail(a, qb, kdec, dv, e_last, v_bf, st):
    o = _dot(a, v_bf) + dv + _dot_nt(qb, st.astype(BF16))
    st_new = st * e_last + _dot_tn(v_bf, kdec)
    return o, st_new


def _hgrn_rec_kernel(*refs, C, n_chunks, has_state, side_holds):
    n_side = len(side_holds)
    n_in = 6 if has_state else 5
    q_ref, f_ref, v_ref, g_ref, gw_ref = refs[:5]
    if has_state:
        s0_ref = refs[5]
    z_ref, so_ref = refs[n_in + n_side:n_in + n_side + 2]
    st_ref, a_buf, qb_buf, kd_buf, dv_buf, el_buf = refs[n_in + 2 * n_side + 2:]
    _side_cast(refs[n_in:n_in + n_side],
               refs[n_in + n_side + 2:n_in + 2 * n_side + 2], side_holds, grid_rank=3)
    li = pl.program_id(2)
    hb = st_ref.shape[0]

    @pl.when(li == 0)
    def _():
        for hh in range(hb):
            st_ref[hh] = s0_ref[0, hh].T if has_state else jnp.zeros((HG_DV, HG_DK), F32)

    gw = gw_ref[...]
    masks = _gla_masks(C)

    def rows_of(ci):
        return pl.ds(pl.multiple_of(ci * C, C), C)

    def head(ci, slot):
        rows = rows_of(ci)
        for hh in range(hb):
            a, qb, kdec, dv, e_last = _gla_head(q_ref[hh, rows, :], f_ref[hh, rows, :],
                                                v_ref[hh, rows, :], masks)
            a_buf[slot, hh] = a
            qb_buf[slot, hh] = qb
            kd_buf[slot, hh] = kdec
            dv_buf[slot, hh] = dv
            el_buf[slot, hh] = e_last

    def tail(ci, slot):
        rows = rows_of(ci)
        for hh in range(hb):
            o, st_new = _gla_tail(a_buf[slot, hh], qb_buf[slot, hh], kd_buf[slot, hh],
                                  dv_buf[slot, hh], el_buf[slot, hh],
                                  v_ref[hh, rows, :].astype(BF16), st_ref[hh])
            st_ref[hh] = st_new
            z_ref[rows, hh * LANES:(hh + 1) * LANES] = (
                _rms(o, gw) * g_ref[hh, rows, :]).astype(BF16)

    head(0, 0)
    if n_chunks > 1:
        assert n_chunks % 2 == 0

        def body(kk, carry):
            c0 = 2 * kk
            tail(c0, 0)
            head(c0 + 1, 1)
            tail(c0 + 1, 1)
            head(c0 + 2, 0)
            return carry

        lax.fori_loop(0, n_chunks // 2 - 1, body, 0)
        tail(n_chunks - 2, 0)
        head(n_chunks - 1, 1)
        tail(n_chunks - 1, 1)
    else:
        tail(0, 0)

    @pl.when(li == pl.num_programs(2) - 1)
    def _():
        for hh in range(hb):
            so_ref[0, hh] = st_ref[hh].T


def _hgrn_rec(p, gnorm_w, s0, B, L, side=()):
    H = HG_HEADS
    T = B * L
    C = min(CHUNK, L)
    hb = REC_HEADS if L > CHUNK else H
    lb = min(L, REC_ROWS)
    nl = L // lb
    ng = H // hb
    has_state = s0 is not None

    def sec_spec(sec):
        return pl.BlockSpec((hb, lb, LANES), lambda b, g, l: (sec * ng + g, b * nl + l, 0))

    in_specs = [sec_spec(0), sec_spec(1), sec_spec(2), sec_spec(3),
                pl.BlockSpec((1, HG_DV), lambda b, g, l: (0, 0))]
    args = [p, p, p, p, gnorm_w]
    if has_state:
        in_specs.append(pl.BlockSpec((1, hb, HG_DK, HG_DV), lambda b, g, l: (b, g, 0, 0)))
        args.append(s0)
    s_in, s_out, s_shape, holds = _side_cast_specs(
        side, B * ng * nl, lambda b, g, l: (b * ng + g) * nl + l)
    outs = pl.pallas_call(
        functools.partial(_hgrn_rec_kernel, C=C, n_chunks=lb // C, has_state=has_state,
                          side_holds=holds),
        grid=(B, ng, nl),
        in_specs=in_specs + s_in,
        out_specs=[
            pl.BlockSpec((lb, hb * LANES), lambda b, g, l: (b * nl + l, g)),
            pl.BlockSpec((1, hb, HG_DK, HG_DV), lambda b, g, l: (b, g, 0, 0)),
        ] + s_out,
        out_shape=[
            jax.ShapeDtypeStruct((T, H * HG_DV), BF16),
            jax.ShapeDtypeStruct((B, H, HG_DK, HG_DV), F32),
        ] + s_shape,
        scratch_shapes=[
            pltpu.VMEM((hb, HG_DV, HG_DK), F32),
            pltpu.VMEM((2, hb, C, C), BF16),
            pltpu.VMEM((2, hb, C, HG_DK), BF16),
            pltpu.VMEM((2, hb, C, HG_DK), BF16),
            pltpu.VMEM((2, hb, C, HG_DV), F32),
            pltpu.VMEM((2, hb, 1, HG_DK), F32),
        ],
        compiler_params=_cparams(("arbitrary", "arbitrary", "arbitrary")),
        name="hgrn_recurrence",
    )(*args, *[w for w, _ in side])
    return outs[0], outs[1], outs[2:]


def _out_proj_kernel(a_ref, w_ref, x_ref, nw_ref, o_ref):
    y = _dot(a_ref[...], w_ref[...])
    o_ref[...] = x_ref[...] + _rms(y, nw_ref[...])


def _out_proj(a_bf, w_bf, x, nw, tm):
    T, K = a_bf.shape
    N = w_bf.shape[1]
    return pl.pallas_call(
        _out_proj_kernel,
        grid=(T // tm,),
        in_specs=[
            pl.BlockSpec((tm, K), lambda i: (i, 0)),
            pl.BlockSpec((K, N), lambda i: (0, 0)),
            pl.BlockSpec((tm, N), lambda i: (i, 0)),
            pl.BlockSpec((1, N), lambda i: (0, 0)),
        ],
        out_specs=pl.BlockSpec((tm, N), lambda i: (i, 0)),
        out_shape=jax.ShapeDtypeStruct((T, N), F32),
        compiler_params=_cparams(("arbitrary",)),
        name="mixer_out_proj",
    )(a_bf, w_bf, x, nw)


def _ffn_kernel(*refs, ns, ls, tps, nj, has_state):
    if has_state:
        (x_ref, nw_ref, wg_ref, wu_ref, cwg_ref, cwu_ref, cbg_ref, cbu_ref, wo_ref, nw2_ref,
         sg_ref, su_ref, o_ref, csg_ref, csu_ref, h_ref, work_ref, carry_ref, act_ref) = refs
    else:
        (x_ref, nw_ref, wg_ref, wu_ref, cwg_ref, cwu_ref, cbg_ref, cbu_ref, wo_ref, nw2_ref,
         o_ref, csg_ref, csu_ref, h_ref, work_ref, carry_ref, act_ref) = refs
        sg_ref = su_ref = None
    i = pl.program_id(0)
    j = pl.program_id(1)
    tf = wg_ref.shape[1]

    def conv(half, w_ref, cw_ref, cb_ref, s_ref, cs_ref):
        u3 = _dot(h_ref[...], w_ref[...]).reshape(ns, ls, tf)
        work_ref[half, :, 8:8 + ls, :] = u3
        prev = jnp.zeros((ns, 2, tf), F32) if s_ref is None else s_ref[...]
        if tps > 1:
            prev = jnp.where((i % tps) == 0, prev, carry_ref[j, half])
        work_ref[half, :, 6:8, :] = prev
        x1 = work_ref[half, :, 7:7 + ls, :]
        x2 = work_ref[half, :, 6:6 + ls, :]
        cw = cw_ref[...]
        c = cb_ref[...] + cw[0:1] * x2 + cw[1:2] * x1 + cw[2:3] * u3
        tail = u3[:, ls - 2:ls, :]
        cs_ref[...] = tail
        if tps > 1:
            carry_ref[j, half] = tail
        return c.reshape(ns * ls, tf)

    def up_conv_act(slot):
        cg = conv(0, wg_ref, cwg_ref, cbg_ref, sg_ref, csg_ref)
        cu = conv(1, wu_ref, cwu_ref, cbu_ref, su_ref, csu_ref)
        act_ref[slot] = (cg * _sigmoid(cg) * cu).astype(BF16)

    @pl.when(j == 0)
    def _():
        h_ref[...] = _rms(x_ref[...], nw_ref[...]).astype(BF16)
        o_ref[...] = jnp.zeros_like(o_ref)
        up_conv_act(0)

    for par in range(2):
        @pl.when((j > 0) & (j < nj) & (j % 2 == par))
        def _():
            o_ref[...] += _dot(act_ref[1 - par], wo_ref[...])
            up_conv_act(par)

    @pl.when(j == nj)
    def _():
        y = o_ref[...] + _dot(act_ref[(nj - 1) % 2], wo_ref[...])
        o_ref[...] = x_ref[...] + _rms(y, nw2_ref[...])


def _ffn(x, nw, w_in_bf, conv_w, conv_b, w_out_bf, nw2, state, n_streams, L, tm):
    T, D = x.shape
    tf = FF_TILE
    nj = D_FF // tf
    if tm >= L:
        assert tm % L == 0
        ns, ls, tps = tm // L, L, 1
    else:
        assert L % tm == 0
        ns, ls, tps = 1, tm, L // tm
    has_state = state is not None
    cb2 = conv_b.reshape(1, 2 * D_FF)

    def stream_blk(i):
        return (i * tm) // L // ns if ns > 1 else (i * tm) // L

    up = lambda j: jnp.minimum(j, nj - 1)
    dn = lambda j: jnp.maximum(j - 1, 0)
    in_specs = [
        pl.BlockSpec((tm, D), lambda i, j: (i, 0), pipeline_mode=pl.Buffered(1)),
        pl.BlockSpec((1, D), lambda i, j: (0, 0)),
        pl.BlockSpec((D, tf), lambda i, j: (0, up(j))),
        pl.BlockSpec((D, tf), lambda i, j: (0, nj + up(j))),
        pl.BlockSpec((3, tf), lambda i, j: (0, up(j))),
        pl.BlockSpec((3, tf), lambda i, j: (0, nj + up(j))),
        pl.BlockSpec((1, tf), lambda i, j: (0, up(j))),
        pl.BlockSpec((1, tf), lambda i, j: (0, nj + up(j))),
        pl.BlockSpec((tf, D), lambda i, j: (dn(j), 0)),
        pl.BlockSpec((1, D), lambda i, j: (0, 0)),
    ]
    args = [x, nw, w_in_bf, w_in_bf, conv_w, conv_w, cb2, cb2, w_out_bf, nw2]
    if has_state:
        in_specs += [
            pl.BlockSpec((ns, 2, tf), lambda i, j: (stream_blk(i), 0, up(j))),
            pl.BlockSpec((ns, 2, tf), lambda i, j: (stream_blk(i), 0, nj + up(j))),
        ]
        args += [state, state]
    out, csg, csu = pl.pallas_call(
        functools.partial(_ffn_kernel, ns=ns, ls=ls, tps=tps, nj=nj, has_state=has_state),
        grid=(T // tm, nj + 1),
        in_specs=in_specs,
        out_specs=[
            pl.BlockSpec((tm, D), lambda i, j: (i, 0)),
            pl.BlockSpec((ns, 2, tf), lambda i, j: (i, 0, up(j))),
            pl.BlockSpec((ns, 2, tf), lambda i, j: (i, 0, up(j))),
        ],
        out_shape=[
            jax.ShapeDtypeStruct((T, D), F32),
            jax.ShapeDtypeStruct((T // ls, 2, D_FF), F32),
            jax.ShapeDtypeStruct((T // ls, 2, D_FF), F32),
        ],
        scratch_shapes=[
            pltpu.VMEM((tm, D), BF16),
            pltpu.VMEM((2, ns, 8 + ls, tf), F32),
            pltpu.VMEM((nj, 2, ns, 2, tf), F32),
            pltpu.VMEM((2, tm, tf), BF16),
        ],
        compiler_params=_cparams(("arbitrary", "arbitrary")),
        name="conv_ffn",
    )(*args)
    tails = jnp.concatenate([csg, csu], axis=-1).reshape(n_streams, tps, 2, 2 * D_FF)
    return out, tails[:, tps - 1]


def _rope_table_kernel(c_ref, s_ref):
    shape = c_ref.shape
    half = ROPE_DIM // 2
    pos = lax.broadcasted_iota(jnp.int32, shape, 0).astype(F32)
    lane = lax.broadcasted_iota(jnp.int32, shape, 1)
    fi = (lane & (half - 1)).astype(F32)
    inv = jnp.exp(fi * (-math.log(ROPE_BASE) / half))
    ang = pos * inv
    valid = lane < ROPE_DIM
    c_ref[...] = jnp.where(valid, jnp.cos(ang), 0.0)
    s_ref[...] = jnp.where(valid, jnp.where(lane < half, -jnp.sin(ang), jnp.sin(ang)), 0.0)


def _rope_tables(n_pos):
    return pl.pallas_call(
        _rope_table_kernel,
        out_shape=[jax.ShapeDtypeStruct((n_pos, LANES), F32)] * 2,
        name="rope_tables",
    )()


def _mla_in_kernel(x_ref, nw_ref, wd_ref, qnw_ref, kvnw_ref, wq_ref, c_ref, s_ref,
                   qcat_ref, ckv_ref, kpe_ref):
    h = _rms(x_ref[...], nw_ref[...]).astype(BF16)
    d = _dot(h, wd_ref[...])
    cq = _rms(d[:, :Q_LORA], qnw_ref[...]).astype(BF16)
    ckv_ref[...] = _rms(d[:, Q_LORA:Q_LORA + KV_LORA], kvnw_ref[...])
    cs = c_ref[...]
    sn = s_ref[...]
    o = Q_LORA + KV_LORA
    kpe_ref[...] = d[:, o:o + LANES] * cs + d[:, o + LANES:o + 2 * LANES] * sn
    hw = MLA_HEADS * LANES
    qn = _dot(cq, wq_ref[:, 0:hw])
    pr = _dot(cq, wq_ref[:, hw:2 * hw])
    ps = _dot(cq, wq_ref[:, 2 * hw:3 * hw])
    for hh in range(MLA_HEADS):
        sl = slice(hh * LANES, (hh + 1) * LANES)
        qcat_ref[hh, :, 0:LANES] = qn[:, sl].astype(BF16)
        qcat_ref[hh, :, LANES:2 * LANES] = (pr[:, sl] * cs + ps[:, sl] * sn).astype(BF16)


def _mla_in(x, nw, wd_bf, qnw, kvnw, wq_bf, cos_t, sin_t, tm):
    T, D = x.shape
    n_tab_blk = cos_t.shape[0] // tm
    return pl.pallas_call(
        _mla_in_kernel,
        grid=(T // tm,),
        in_specs=[
            pl.BlockSpec((tm, D), lambda i: (i, 0)),
            pl.BlockSpec((1, D), lambda i: (0, 0)),
            pl.BlockSpec(wd_bf.shape, lambda i: (0, 0)),
            pl.BlockSpec((1, Q_LORA), lambda i: (0, 0)),
            pl.BlockSpec((1, KV_LORA), lambda i: (0, 0)),
            pl.BlockSpec(wq_bf.shape, lambda i: (0, 0)),
            pl.BlockSpec((tm, LANES), lambda i: (i % n_tab_blk, 0)),
            pl.BlockSpec((tm, LANES), lambda i: (i % n_tab_blk, 0)),
        ],
        out_specs=[
            pl.BlockSpec((MLA_HEADS, tm, QK_PAD), lambda i: (0, i, 0)),
            pl.BlockSpec((tm, KV_LORA), lambda i: (i, 0)),
            pl.BlockSpec((tm, LANES), lambda i: (i, 0)),
        ],
        out_shape=[
            jax.ShapeDtypeStruct((MLA_HEADS, T, QK_PAD), BF16),
            jax.ShapeDtypeStruct((T, KV_LORA), F32),
            jax.ShapeDtypeStruct((T, LANES), F32),
        ],
        compiler_params=_cparams(("arbitrary",)),
        name="mla_in_proj",
    )(x, nw, wd_bf, qnw, kvnw, wq_bf, cos_t, sin_t)


def _kv_up_kernel(ckv_ref, kpe_ref, wkv_ref, kcat_ref, v_ref):
    kv = _dot(ckv_ref[...].astype(BF16), wkv_ref[...])
    kp = kpe_ref[...].astype(BF16)
    hw = NOPE_DIM + V_DIM
    for hh in range(MLA_HEADS):
        kcat_ref[hh, :, 0:LANES] = kv[:, hh * hw:hh * hw + NOPE_DIM].astype(BF16)
        kcat_ref[hh, :, LANES:2 * LANES] = kp
        v_ref[hh] = kv[:, hh * hw + NOPE_DIM:(hh + 1) * hw].astype(BF16)


def _kv_up(ckv, kpe_pad, wkv_bf, tm):
    R = ckv.shape[0]
    return pl.pallas_call(
        _kv_up_kernel,
        grid=(R // tm,),
        in_specs=[
            pl.BlockSpec((tm, KV_LORA), lambda i: (i, 0)),
            pl.BlockSpec((tm, LANES), lambda i: (i, 0)),
            pl.BlockSpec(wkv_bf.shape, lambda i: (0, 0)),
        ],
        out_specs=[
            pl.BlockSpec((MLA_HEADS, tm, QK_PAD), lambda i: (0, i, 0)),
            pl.BlockSpec((MLA_HEADS, tm, V_DIM), lambda i: (0, i, 0)),
        ],
        out_shape=[
            jax.ShapeDtypeStruct((MLA_HEADS, R, QK_PAD), BF16),
            jax.ShapeDtypeStruct((MLA_HEADS, R, V_DIM), BF16),
        ],
        compiler_params=_cparams(("arbitrary",)),
        name="mla_kv_up_proj",
    )(ckv, kpe_pad, wkv_bf)


def _attn_prompt_kernel(*refs, L, tq, scale, side_holds):
    n_side = len(side_holds)
    q_ref, k_ref, v_ref = refs[:3]
    o_ref = refs[3 + n_side]
    _side_cast(refs[3:3 + n_side], refs[4 + n_side:], side_holds)
    row = lax.broadcasted_iota(jnp.int32, (tq, tq), 0)
    col = lax.broadcasted_iota(jnp.int32, (tq, tq), 1)
    sh = int(math.log2(CHUNK))
    dmask = (row >> sh) >= (col >> sh)

    c2 = scale * math.log2(math.e)
    def scores(qi):
        n = qi * tq
        q = q_ref[0, n:n + tq, :]
        s_d = jnp.where(dmask, _dot_nt(q, k_ref[0, n:n + tq, :]), -jnp.inf)
        s_p = _dot_nt(q, k_ref[0, 0:n, :]) if qi > 0 else None
        return s_d, s_p

    nq = L // tq
    pending = [scores(t) for t in range(min(SCORES_AHEAD, nq))]
    for qi in range(nq):
        n = qi * tq
        s_d, s_p = pending.pop(0)
        if qi + SCORES_AHEAD < nq:
            pending.append(scores(qi + SCORES_AHEAD))
        m = jnp.max(s_d, axis=-1, keepdims=True)
        if qi > 0:
            m = jnp.maximum(m, jnp.max(s_p, axis=-1, keepdims=True))
        p_d = jnp.exp2((s_d - m) * c2)
        l = jnp.sum(p_d, axis=-1, keepdims=True)
        acc = _dot(p_d.astype(BF16), v_ref[0, n:n + tq, :])
        if qi > 0:
            p_p = jnp.exp2((s_p - m) * c2)
            l = l + jnp.sum(p_p, axis=-1, keepdims=True)
            acc = acc + _dot(p_p.astype(BF16), v_ref[0, 0:n, :])
        o_ref[n:n + tq, :] = (acc / l).astype(BF16)


def _attn_prompt(qcat, kcat, v, B, L, side=()):
    H = MLA_HEADS
    scale = (NOPE_DIM + ROPE_DIM) ** -0.5
    s_in, s_out, s_shape, holds = _side_cast_specs(side, B * H, lambda b, h: b * H + h)
    outs = pl.pallas_call(
        functools.partial(_attn_prompt_kernel, L=L, tq=ATTN_Q_TILE, scale=scale,
                          side_holds=holds),
        grid=(B, H),
        in_specs=[
            pl.BlockSpec((1, L, QK_PAD), lambda b, h: (h, b, 0)),
            pl.BlockSpec((1, L, QK_PAD), lambda b, h: (h, b, 0)),
            pl.BlockSpec((1, L, V_DIM), lambda b, h: (h, b, 0)),
        ] + s_in,
        out_specs=[pl.BlockSpec((L, V_DIM), lambda b, h: (b, h))] + s_out,
        out_shape=[jax.ShapeDtypeStruct((B * L, H * V_DIM), BF16)] + s_shape,
        compiler_params=_cparams(("arbitrary", "arbitrary")),
        name="attn_prompt",
    )(qcat, kcat, v, *[w for w, _ in side])
    return outs[0], outs[1:]


def _attn_sample_kernel(q_ref, cc_ref, pc_ref, cn_ref, pn_ref, wkv_ref, o_ref, *, past, scale):
    H, Lq, _ = q_ref.shape
    hw = NOPE_DIM + V_DIM
    qlat, qrope = [], []
    for hh in range(H):
        wk_h = wkv_ref[:, hh * hw:hh * hw + NOPE_DIM]
        qlat.append(_dot_nt(q_ref[hh, :, 0:NOPE_DIM], wk_h).astype(BF16))
        qrope.append(q_ref[hh, :, NOPE_DIM:NOPE_DIM + ROPE_DIM])
    qlat = jnp.concatenate(qlat, axis=0)
    qrope = jnp.concatenate(qrope, axis=0)
    cc = cc_ref[0].astype(BF16)
    cn = cn_ref[...].astype(BF16)
    s1 = _dot_nt(qlat, cc) + _dot_nt(qrope, pc_ref[0].astype(BF16))
    s2 = _dot_nt(qlat, cn) + _dot_nt(qrope, pn_ref[:, 0:ROPE_DIM].astype(BF16))
    sh = int(math.log2(CHUNK))

    def q_chunk(n):
        t = lax.broadcasted_iota(jnp.int32, (H, Lq, n), 1).reshape(H * Lq, n)
        return (t + past) >> sh

    kc1 = lax.broadcasted_iota(jnp.int32, s1.shape, 1) >> sh
    kc2 = (lax.broadcasted_iota(jnp.int32, s2.shape, 1) + past) >> sh
    s1 = jnp.where(kc1 <= q_chunk(s1.shape[1]), s1 * scale, -jnp.inf)
    s2 = jnp.where(kc2 <= q_chunk(s2.shape[1]), s2 * scale, -jnp.inf)
    m = jnp.maximum(jnp.max(s1, axis=-1, keepdims=True), jnp.max(s2, axis=-1, keepdims=True))
    p1 = jnp.exp(s1 - m)
    p2 = jnp.exp(s2 - m)
    l = jnp.sum(p1, axis=-1, keepdims=True) + jnp.sum(p2, axis=-1, keepdims=True)
    olat = ((_dot(p1.astype(BF16), cc) + _dot(p2.astype(BF16), cn)) / l).astype(BF16)
    for hh in range(H):
        sl = slice(hh * LANES, (hh + 1) * LANES)
        wv_h = wkv_ref[:, hh * hw + NOPE_DIM:(hh + 1) * hw]
        o_ref[:, sl] = _dot(olat[hh * Lq:(hh + 1) * Lq], wv_h).astype(BF16)


def _attn_sample(qcat, ckv_cache, kpe_cache, ckv_new, kpe_new, wkv_bf, B, L, past):
    H = MLA_HEADS
    scale = (NOPE_DIM + ROPE_DIM) ** -0.5
    return pl.pallas_call(
        functools.partial(_attn_sample_kernel, past=past, scale=scale),
        grid=(B,),
        in_specs=[
            pl.BlockSpec((H, L, QK_PAD), lambda b: (0, b, 0)),
            pl.BlockSpec((1, past, KV_LORA), lambda b: (b, 0, 0)),
            pl.BlockSpec((1, past, ROPE_DIM), lambda b: (b, 0, 0)),
            pl.BlockSpec((L, KV_LORA), lambda b: (b, 0)),
            pl.BlockSpec((L, LANES), lambda b: (b, 0)),
            pl.BlockSpec(wkv_bf.shape, lambda b: (0, 0)),
        ],
        out_specs=pl.BlockSpec((L, H * V_DIM), lambda b: (b, 0)),
        out_shape=jax.ShapeDtypeStruct((B * L, H * V_DIM), BF16),
        compiler_params=_cparams(("arbitrary",)),
        name="attn_sample",
    )(qcat, ckv_cache, kpe_cache, ckv_new, kpe_new, wkv_bf)


def _swap_halves(w):
    half = w.shape[-1] // 2
    return jnp.concatenate([w[..., half:], w[..., :half]], axis=-1)


def _pad_lanes(w):
    return jnp.pad(w, [(0, 0)] * (w.ndim - 1) + [(0, LANES - w.shape[-1])])


def _prep_mla_weights(w_down, w_uq, w_ukv):
    o = Q_LORA + KV_LORA
    wpe = w_down[:, o:]
    wd = jnp.concatenate([w_down[:, :o], _pad_lanes(wpe), _pad_lanes(_swap_halves(wpe))], axis=-1)
    wq3 = w_uq.reshape(Q_LORA, MLA_HEADS, NOPE_DIM + ROPE_DIM)
    wn = wq3[..., :NOPE_DIM].reshape(Q_LORA, -1)
    wr = wq3[..., NOPE_DIM:]
    wrp = _pad_lanes(wr).reshape(Q_LORA, -1)
    wrs = _pad_lanes(_swap_halves(wr)).reshape(Q_LORA, -1)
    wq = jnp.concatenate([wn, wrp, wrs], axis=-1)
    return wd.astype(BF16), wq.astype(BF16), w_ukv.astype(BF16)


def _trunk(x, B, L, hg_state, ckv_cache, kpe_cache, conv_state, W, cos_t, sin_t, bf=None):
    norm_w = W["norm_w"]
    nrow = lambda l, k: norm_w[l, k].reshape(1, -1)
    past = 0 if ckv_cache is None else ckv_cache.shape[1]
    T = B * L
    D = x.shape[-1]
    tm, tm_ffn, tm_mla = min(T, TOKEN_TILE), min(T, FFN_TOKEN_TILE), min(T, MLA_TOKEN_TILE)
    convert = bf is None
    bf = {} if convert else dict(bf)
    f32 = W["f32"]

    p, cast = _hgrn_in(x, nrow(0, 0), W["hgrn_w_in"], W["lb_logits"], 0, tm,
                       [(f32["hgrn_w_out"], 0)] if convert else ())
    if convert:
        bf["hgrn_w_out"], = cast
    z, hg_new, cast = _hgrn_rec(p, W["hgrn_gnorm_w"].reshape(1, -1), hg_state, B, L,
                                [(f32["ffn_w_in"], 0), (f32["ffn_w_out"], 0)] if convert else ())
    if convert:
        bf["ffn0"] = cast
    x = _out_proj(z, bf["hgrn_w_out"], x, nrow(0, 1), tm)
    x, conv0 = _ffn(x, nrow(0, 2), bf["ffn0"][0], W["ffn_conv_w"][0], W["ffn_conv_b"][0],
                    bf["ffn0"][1], nrow(0, 3), None if conv_state is None else conv_state[0],
                    B, L, tm_ffn)

    qcat, ckv, kpe_pad = _mla_in(x, nrow(1, 0), W["mla_wd"], W["mla_q_norm_w"].reshape(1, -1),
                                 W["mla_kv_norm_w"].reshape(1, -1), W["mla_wq"], cos_t, sin_t,
                                 tm_mla)
    if ckv_cache is None:
        kcat_n, v_n = _kv_up(ckv, kpe_pad, W["mla_wkv"], tm_mla)
        o, cast = _attn_prompt(
            qcat, kcat_n, v_n, B, L,
            [(f32["ffn_w_in"], 1), (f32["ffn_w_out"], 1), (f32["mla_w_out"], 0)] if convert
            else ())
        if convert:
            bf["ffn1"], bf["mla_w_out"] = cast[:2], cast[2]
    else:
        o = _attn_sample(qcat, ckv_cache, kpe_cache, ckv, kpe_pad, W["mla_wkv"], B, L, past)
    x = _out_proj(o, bf["mla_w_out"], x, nrow(1, 1), tm)
    x, conv1 = _ffn(x, nrow(1, 2), bf["ffn1"][0], W["ffn_conv_w"][1], W["ffn_conv_b"][1],
                    bf["ffn1"][1], nrow(1, 3), None if conv_state is None else conv_state[1],
                    B, L, tm_ffn)

    return (x.reshape(B, L, D), hg_new[None], ckv.reshape(1, B, L, KV_LORA),
            kpe_pad[:, :ROPE_DIM].reshape(1, B, L, ROPE_DIM), jnp.stack([conv0, conv1]), bf)


def kernel(x_prompt, x_sample, state_hgrn, cache_ckv, cache_kpe, state_conv, norm_w, lb_logits,
           hgrn_w_in, hgrn_gnorm_w, hgrn_w_out, mla_w_down, mla_q_norm_w, mla_kv_norm_w,
           mla_w_uq, mla_w_ukv, mla_w_out, ffn_w_in, ffn_conv_w, ffn_conv_b, ffn_w_out):
    wd, wq, wkv = _prep_mla_weights(mla_w_down[0], mla_w_uq[0], mla_w_ukv[0])
    W = dict(
        norm_w=norm_w, lb_logits=lb_logits,
        hgrn_w_in=hgrn_w_in[0].astype(BF16), hgrn_gnorm_w=hgrn_gnorm_w[0],
        mla_wd=wd, mla_wq=wq, mla_wkv=wkv,
        mla_q_norm_w=mla_q_norm_w[0], mla_kv_norm_w=mla_kv_norm_w[0],
        ffn_conv_w=ffn_conv_w, ffn_conv_b=ffn_conv_b,
        f32=dict(hgrn_w_out=hgrn_w_out, mla_w_out=mla_w_out, ffn_w_in=ffn_w_in,
                 ffn_w_out=ffn_w_out),
    )
    Bp, Lp, D = x_prompt.shape
    Bs, Ls, _ = x_sample.shape
    past = cache_ckv.shape[2]
    cos_t, sin_t = _rope_tables(max(Lp, past + Ls))

    yp, hgp, ckvp, kpep, convp, bf = _trunk(
        x_prompt.reshape(Bp * Lp, D), Bp, Lp, None, None, None, None, W, cos_t[:Lp], sin_t[:Lp])

    cos_s = jnp.tile(cos_t[past:past + Ls], (Bs, 1))
    sin_s = jnp.tile(sin_t[past:past + Ls], (Bs, 1))
    ys, hgs, ckvs, kpes, convs, _ = _trunk(
        x_sample.reshape(Bs * Ls, D), Bs, Ls, state_hgrn[0], cache_ckv[0], cache_kpe[0],
        state_conv, W, cos_s, sin_s, bf)
    return (yp, ys, hgp, hgs, ckvp, ckvs, kpep, kpes, convp, convs)
```

```python
import functools
import math

import jax
import jax.numpy as jnp
from jax import lax
from jax.experimental import pallas as pl
from jax.experimental.pallas import tpu as pltpu

F32 = jnp.float32
BF16 = jnp.bfloat16

EPS = 1e-6
LANES = 128
BF16_ROWS = 16
CHUNK = 64
HG_HEADS = 16
HG_DK = 128
HG_DV = 128
MLA_HEADS = 16
Q_LORA = 512
KV_LORA = 512
NOPE_DIM = 128
ROPE_DIM = 64
V_DIM = 128
ROPE_BASE = 10000.0
D_FF = 5632
FF_TILE = 512
TOKEN_TILE = 512
FFN_TOKEN_TILE = 1024
MLA_TOKEN_TILE = 256
IN_PROJ_COLS = 2048
ATTN_Q_TILE = 256
REC_HEADS = 4
REC_ROWS = 1024
SCORES_AHEAD = 2
QK_PAD = 256
VMEM_LIMIT = 56 * 1024 * 1024


def _cparams(sem):
    return pltpu.CompilerParams(dimension_semantics=sem, vmem_limit_bytes=VMEM_LIMIT)


def _rms(x, w):
    ms = jnp.mean(x * x, axis=-1, keepdims=True)
    return x * lax.rsqrt(ms + EPS) * w


def _sigmoid(x):
    return 0.5 * jnp.tanh(0.5 * x) + 0.5


def _dot(a, b):
    return jnp.dot(a, b, preferred_element_type=F32)


def _dot_nt(a, b):
    return lax.dot_general(a, b, (((1,), (1,)), ((), ())), preferred_element_type=F32)


def _dot_tn(a, b):
    return lax.dot_general(a, b, (((0,), (0,)), ((), ())), preferred_element_type=F32)


def _side_cast_specs(srcs, n_steps, step):
    in_specs, out_specs, out_shapes, holds = [], [], [], []
    for w, layer in srcs:
        _, R, C = w.shape
        hold = 1
        while (R * hold) % (n_steps * BF16_ROWS) != 0:
            hold *= 2
            assert hold <= n_steps, (R, n_steps)
        rows = R * hold // n_steps

        def slab(*g, hold=hold):
            return step(*g) // hold

        in_specs.append(pl.BlockSpec((None, rows, C),
                                     lambda *g, slab=slab, layer=layer: (layer, slab(*g), 0)))
        out_specs.append(pl.BlockSpec((rows, C), lambda *g, slab=slab: (slab(*g), 0)))
        out_shapes.append(jax.ShapeDtypeStruct((R, C), BF16))
        holds.append(hold)
    return in_specs, out_specs, out_shapes, tuple(holds)


def _side_cast(src_refs, dst_refs, holds, grid_rank=2):
    step = pl.program_id(0)
    for d in range(1, grid_rank):
        step = step * pl.num_programs(d) + pl.program_id(d)
    for src, dst, hold in zip(src_refs, dst_refs, holds):
        @pl.when(step % hold == 0)
        def _():
            dst[...] = src[...].astype(BF16)


def _hgrn_in_kernel(*refs, lb_row, blk_per_sec, side_holds, emit_w):
    n_side = len(side_holds)
    x_ref, nw_ref, w_ref, lb_ref = refs[:4]
    o_ref, h_ref = refs[4 + n_side], refs[-1]
    _side_cast(refs[4:4 + n_side], refs[5 + n_side:5 + 2 * n_side], side_holds)
    if emit_w:
        wbf_ref = refs[5 + 2 * n_side]
        wbf_ref[...] = w_ref[...].astype(BF16)
        w_ref = wbf_ref
    j = pl.program_id(1)

    @pl.when(j == 0)
    def _():
        h_ref[...] = _rms(x_ref[...], nw_ref[...]).astype(BF16)

    sec = j // blk_per_sec
    heads = o_ref.shape[0]
    lg = lb_ref[...]
    e = jnp.exp(lg - jnp.max(lg, axis=0, keepdims=True))
    sm = e / jnp.sum(e, axis=0, keepdims=True)
    lb_all = jnp.sum(sm[0:lb_row + 1], axis=0, keepdims=True)
    hp = heads // 2
    for c0 in (0, hp):
        cols = slice(c0 * LANES, (c0 + hp) * LANES)
        r = _dot(h_ref[...], w_ref[:, cols])
        lb = lb_all[:, cols]
        sig = _sigmoid(r)
        val = jnp.where(sec == 2, r, jnp.where(sec == 1, lb + (1.0 - lb) * sig, r * sig))
        for hh in range(hp):
            o_ref[c0 + hh] = val[:, hh * LANES:(hh + 1) * LANES]


def _hgrn_in(x, nw, w_bf, lb_logits, lb_row, tm, side=()):
    T, D = x.shape
    N = w_bf.shape[1]
    emit_w = w_bf.dtype == F32
    assert not emit_w or T == tm
    tn = IN_PROJ_COLS // 2 if emit_w else IN_PROJ_COLS
    sec_w = N // 4
    blk_per_sec = sec_w // tn
    heads = tn // LANES
    nj = N // tn
    s_in, s_out, s_shape, holds = _side_cast_specs(side, (T // tm) * nj,
                                                   lambda i, j: i * nj + j)
    outs = pl.pallas_call(
        functools.partial(_hgrn_in_kernel, lb_row=lb_row, blk_per_sec=blk_per_sec,
                          side_holds=holds, emit_w=emit_w),
        grid=(T // tm, nj),
        in_specs=[
            pl.BlockSpec((tm, D), lambda i, j: (i, 0)),
            pl.BlockSpec((1, D), lambda i, j: (0, 0)),
            pl.BlockSpec((D, tn), lambda i, j: (0, j)),
            pl.BlockSpec((lb_logits.shape[0], tn), lambda i, j: (0, j % blk_per_sec)),
        ] + s_in,
        out_specs=[pl.BlockSpec((heads, tm, LANES), lambda i, j: (j, i, 0))] + s_out
        + ([pl.BlockSpec((D, tn), lambda i, j: (0, j))] if emit_w else []),
        out_shape=[jax.ShapeDtypeStruct((N // LANES, T, LANES), F32)] + s_shape
        + ([jax.ShapeDtypeStruct((D, N), BF16)] if emit_w else []),
        scratch_shapes=[pltpu.VMEM((tm, D), BF16)],
        compiler_params=_cparams(("arbitrary", "arbitrary")),
        name="hgrn_in_proj",
    )(x, nw, w_bf, lb_logits, *[w for w, _ in side])
    return outs[0], outs[1:]


SUB = 8


def _bcast_mid_in_tile(c, s, t8):
    if s == 4:
        return jnp.broadcast_to(c[s - 1:s, :], c.shape)
    dn1 = pltpu.roll(c, 1, 0)
    if s == 1:
        return jnp.where((t8 & 1) == 1, dn1, c)
    ph = t8 & 3
    up1 = pltpu.roll(c, SUB - 1, 0)
    dn2 = pltpu.roll(c, 2, 0)
    return jnp.where(ph == 0, up1, jnp.where(ph == 1, c, jnp.where(ph == 2, dn1, dn2)))


def _gla_masks(C):
    t8 = lax.broadcasted_iota(jnp.int32, (SUB, LANES), 0)
    rr = lax.broadcasted_iota(jnp.int32, (C, C), 0)
    cc = lax.broadcasted_iota(jnp.int32, (C, C), 1)
    pairs = []
    s = 1
    while s < C:
        sh = int(math.log2(2 * s))
        pairs.append(((rr >> sh) == (cc >> sh)) & ((rr & s) != 0) & ((cc & s) == 0))
        s *= 2
    return t8, pairs


def _gla_head(q, fg, v, masks):
    C = q.shape[0]
    nt = C // SUB
    t8, pairs = masks
    k = 1.0 - fg
    tiles = lambda x: [x[i * SUB:(i + 1) * SUB] for i in range(nt)]
    qs, ks = tiles(q), tiles(k)
    cs = [jnp.log2(f) for f in tiles(fg)]
    zero = jnp.zeros((SUB, LANES), F32)
    a = jnp.zeros((C, C), F32)
    s = 1
    for pair in pairs:
        if s < SUB:
            upper = (t8 & s) != 0
            qd, kd = [], []
            for i in range(nt):
                bc = _bcast_mid_in_tile(cs[i], s, t8)
                qd.append(qs[i] * jnp.exp2(cs[i]))
                kd.append(ks[i] * jnp.exp2(jnp.minimum(bc - cs[i], 0.0)))
                cs[i] = cs[i] + jnp.where(upper, bc, 0.0)
        else:
            m = s // SUB
            qd, kd = [zero] * nt, [zero] * nt
            for blk in range(0, nt, 2 * m):
                bc = jnp.broadcast_to(cs[blk + m - 1][SUB - 1:SUB, :], (SUB, LANES))
                for i in range(blk, blk + m):
                    kd[i] = ks[i] * jnp.exp2(bc - cs[i])
                for i in range(blk + m, blk + 2 * m):
                    qd[i] = qs[i] * jnp.exp2(cs[i])
                    cs[i] = cs[i] + bc
        a_s = _dot_nt(jnp.concatenate(qd, axis=0).astype(BF16),
                      jnp.concatenate(kd, axis=0).astype(BF16))
        a = jnp.where(pair, a_s, a)
        s *= 2
    b = jnp.concatenate(cs, axis=0)
    b_last = b[C - 1:C, :]
    diag = jnp.sum(q * k, axis=-1, keepdims=True)
    qb = (q * jnp.exp2(b)).astype(BF16)
    kdec = (k * jnp.exp2(b_last - b)).astype(BF16)
    return a.astype(BF16), qb, kdec, diag * v, jnp.exp2(b_last)


def _gla_tail(a, qb, kdec, dv, e_last, v_bf, st):
    o = _dot(a, v_bf) + dv + _dot_nt(qb, st.astype(BF16))
    st_new = st * e_last + _dot_tn(v_bf, kdec)
    return o, st_new


def _hgrn_rec_kernel(*refs, C, n_chunks, has_state, side_holds):
    n_side = len(side_holds)
    n_in = 6 if has_state else 5
    q_ref, f_ref, v_ref, g_ref, gw_ref = refs[:5]
    if has_state:
        s0_ref = refs[5]
    z_ref, so_ref = refs[n_in + n_side:n_in + n_side + 2]
    st_ref, a_buf, qb_buf, kd_buf, dv_buf, el_buf = refs[n_in + 2 * n_side + 2:]
    _side_cast(refs[n_in:n_in + n_side],
               refs[n_in + n_side + 2:n_in + 2 * n_side + 2], side_holds, grid_rank=3)
    li = pl.program_id(2)
    hb = st_ref.shape[0]

    @pl.when(li == 0)
    def _():
        for hh in range(hb):
            st_ref[hh] = s0_ref[0, hh].T if has_state else jnp.zeros((HG_DV, HG_DK), F32)

    gw = gw_ref[...]
    masks = _gla_masks(C)

    def rows_of(ci):
        return pl.ds(pl.multiple_of(ci * C, C), C)

    def head(ci, slot):
        rows = rows_of(ci)
        for hh in range(hb):
            a, qb, kdec, dv, e_last = _gla_head(q_ref[hh, rows, :], f_ref[hh, rows, :],
                                                v_ref[hh, rows, :], masks)
            a_buf[slot, hh] = a
            qb_buf[slot, hh] = qb
            kd_buf[slot, hh] = kdec
            dv_buf[slot, hh] = dv
            el_buf[slot, hh] = e_last

    def tail(ci, slot):
        rows = rows_of(ci)
        for hh in range(hb):
            o, st_new = _gla_tail(a_buf[slot, hh], qb_buf[slot, hh], kd_buf[slot, hh],
                                  dv_buf[slot, hh], el_buf[slot, hh],
                                  v_ref[hh, rows, :].astype(BF16), st_ref[hh])
            st_ref[hh] = st_new
            z_ref[rows, hh * LANES:(hh + 1) * LANES] = (
                _rms(o, gw) * g_ref[hh, rows, :]).astype(BF16)

    head(0, 0)
    if n_chunks > 1:
        assert n_chunks % 2 == 0

        def body(kk, carry):
            c0 = 2 * kk
            tail(c0, 0)
            head(c0 + 1, 1)
            tail(c0 + 1, 1)
            head(c0 + 2, 0)
            return carry

        lax.fori_loop(0, n_chunks // 2 - 1, body, 0)
        tail(n_chunks - 2, 0)
        head(n_chunks - 1, 1)
        tail(n_chunks - 1, 1)
    else:
        tail(0, 0)

    @pl.when(li == pl.num_programs(2) - 1)
    def _():
        for hh in range(hb):
            so_ref[0, hh] = st_ref[hh].T


def _hgrn_rec(p, gnorm_w, s0, B, L, side=()):
    H = HG_HEADS
    T = B * L
    C = min(CHUNK, L)
    hb = REC_HEADS if L > CHUNK else H
    lb = min(L, REC_ROWS)
    nl = L // lb
    ng = H // hb
    has_state = s0 is not None

    def sec_spec(sec):
        return pl.BlockSpec((hb, lb, LANES), lambda b, g, l: (sec * ng + g, b * nl + l, 0))

    in_specs = [sec_spec(0), sec_spec(1), sec_spec(2), sec_spec(3),
                pl.BlockSpec((1, HG_DV), lambda b, g, l: (0, 0))]
    args = [p, p, p, p, gnorm_w]
    if has_state:
        in_specs.append(pl.BlockSpec((1, hb, HG_DK, HG_DV), lambda b, g, l: (b, g, 0, 0)))
        args.append(s0)
    s_in, s_out, s_shape, holds = _side_cast_specs(
        side, B * ng * nl, lambda b, g, l: (b * ng + g) * nl + l)
    outs = pl.pallas_call(
        functools.partial(_hgrn_rec_kernel, C=C, n_chunks=lb // C, has_state=has_state,
                          side_holds=holds),
        grid=(B, ng, nl),
        in_specs=in_specs + s_in,
        out_specs=[
            pl.BlockSpec((lb, hb * LANES), lambda b, g, l: (b * nl + l, g)),
            pl.BlockSpec((1, hb, HG_DK, HG_DV), lambda b, g, l: (b, g, 0, 0)),
        ] + s_out,
        out_shape=[
            jax.ShapeDtypeStruct((T, H * HG_DV), BF16),
            jax.ShapeDtypeStruct((B, H, HG_DK, HG_DV), F32),
        ] + s_shape,
        scratch_shapes=[
            pltpu.VMEM((hb, HG_DV, HG_DK), F32),
            pltpu.VMEM((2, hb, C, C), BF16),
            pltpu.VMEM((2, hb, C, HG_DK), BF16),
            pltpu.VMEM((2, hb, C, HG_DK), BF16),
            pltpu.VMEM((2, hb, C, HG_DV), F32),
            pltpu.VMEM((2, hb, 1, HG_DK), F32),
        ],
        compiler_params=_cparams(("arbitrary", "arbitrary", "arbitrary")),
        name="hgrn_recurrence",
    )(*args, *[w for w, _ in side])
    return outs[0], outs[1], outs[2:]


def _out_proj_kernel(a_ref, w_ref, x_ref, nw_ref, o_ref):
    y = _dot(a_ref[...], w_ref[...])
    o_ref[...] = x_ref[...] + _rms(y, nw_ref[...])


def _out_proj(a_bf, w_bf, x, nw, tm):
    T, K = a_bf.shape
    N = w_bf.shape[1]
    return pl.pallas_call(
        _out_proj_kernel,
        grid=(T // tm,),
        in_specs=[
            pl.BlockSpec((tm, K), lambda i: (i, 0)),
            pl.BlockSpec((K, N), lambda i: (0, 0)),
            pl.BlockSpec((tm, N), lambda i: (i, 0)),
            pl.BlockSpec((1, N), lambda i: (0, 0)),
        ],
        out_specs=pl.BlockSpec((tm, N), lambda i: (i, 0)),
        out_shape=jax.ShapeDtypeStruct((T, N), F32),
        compiler_params=_cparams(("arbitrary",)),
        name="mixer_out_proj",
    )(a_bf, w_bf, x, nw)


def _ffn_kernel(*refs, ns, ls, tps, nj, has_state):
    if has_state:
        (x_ref, nw_ref, wg_ref, wu_ref, cwg_ref, cwu_ref, cbg_ref, cbu_ref, wo_ref, nw2_ref,
         sg_ref, su_ref, o_ref, csg_ref, csu_ref, h_ref, work_ref, carry_ref, act_ref) = refs
    else:
        (x_ref, nw_ref, wg_ref, wu_ref, cwg_ref, cwu_ref, cbg_ref, cbu_ref, wo_ref, nw2_ref,
         o_ref, csg_ref, csu_ref, h_ref, work_ref, carry_ref, act_ref) = refs
        sg_ref = su_ref = None
    i = pl.program_id(0)
    j = pl.program_id(1)
    tf = wg_ref.shape[1]

    def conv(half, w_ref, cw_ref, cb_ref, s_ref, cs_ref):
        u3 = _dot(h_ref[...], w_ref[...]).reshape(ns, ls, tf)
        work_ref[half, :, 8:8 + ls, :] = u3
        prev = jnp.zeros((ns, 2, tf), F32) if s_ref is None else s_ref[...]
        if tps > 1:
            prev = jnp.where((i % tps) == 0, prev, carry_ref[j, half])
        work_ref[half, :, 6:8, :] = prev
        x1 = work_ref[half, :, 7:7 + ls, :]
        x2 = work_ref[half, :, 6:6 + ls, :]
        cw = cw_ref[...]
        c = cb_ref[...] + cw[0:1] * x2 + cw[1:2] * x1 + cw[2:3] * u3
        tail = u3[:, ls - 2:ls, :]
        cs_ref[...] = tail
        if tps > 1:
            carry_ref[j, half] = tail
        return c.reshape(ns * ls, tf)

    def up_conv_act(slot):
        cg = conv(0, wg_ref, cwg_ref, cbg_ref, sg_ref, csg_ref)
        cu = conv(1, wu_ref, cwu_ref, cbu_ref, su_ref, csu_ref)
        act_ref[slot] = (cg * _sigmoid(cg) * cu).astype(BF16)

    @pl.when(j == 0)
    def _():
        h_ref[...] = _rms(x_ref[...], nw_ref[...]).astype(BF16)
        o_ref[...] = jnp.zeros_like(o_ref)
        up_conv_act(0)

    for par in range(2):
        @pl.when((j > 0) & (j < nj) & (j % 2 == par))
        def _():
            o_ref[...] += _dot(act_ref[1 - par], wo_ref[...])
            up_conv_act(par)

    @pl.when(j == nj)
    def _():
        y = o_ref[...] + _dot(act_ref[(nj - 1) % 2], wo_ref[...])
        o_ref[...] = x_ref[...] + _rms(y, nw2_ref[...])


def _ffn(x, nw, w_in_bf, conv_w, conv_b, w_out_bf, nw2, state, n_streams, L, tm):
    T, D = x.shape
    tf = FF_TILE
    nj = D_FF // tf
    if tm >= L:
        assert tm % L == 0
        ns, ls, tps = tm // L, L, 1
    else:
        assert L % tm == 0
        ns, ls, tps = 1, tm, L // tm
    has_state = state is not None
    cb2 = conv_b.reshape(1, 2 * D_FF)

    def stream_blk(i):
        return (i * tm) // L // ns if ns > 1 else (i * tm) // L

    up = lambda j: jnp.minimum(j, nj - 1)
    dn = lambda j: jnp.maximum(j - 1, 0)
    in_specs = [
        pl.BlockSpec((tm, D), lambda i, j: (i, 0), pipeline_mode=pl.Buffered(1)),
        pl.BlockSpec((1, D), lambda i, j: (0, 0)),
        pl.BlockSpec((D, tf), lambda i, j: (0, up(j))),
        pl.BlockSpec((D, tf), lambda i, j: (0, nj + up(j))),
        pl.BlockSpec((3, tf), lambda i, j: (0, up(j))),
        pl.BlockSpec((3, tf), lambda i, j: (0, nj + up(j))),
        pl.BlockSpec((1, tf), lambda i, j: (0, up(j))),
        pl.BlockSpec((1, tf), lambda i, j: (0, nj + up(j))),
        pl.BlockSpec((tf, D), lambda i, j: (dn(j), 0)),
        pl.BlockSpec((1, D), lambda i, j: (0, 0)),
    ]
    args = [x, nw, w_in_bf, w_in_bf, conv_w, conv_w, cb2, cb2, w_out_bf, nw2]
    if has_state:
        in_specs += [
            pl.BlockSpec((ns, 2, tf), lambda i, j: (stream_blk(i), 0, up(j))),
            pl.BlockSpec((ns, 2, tf), lambda i, j: (stream_blk(i), 0, nj + up(j))),
        ]
        args += [state, state]
    out, csg, csu = pl.pallas_call(
        functools.partial(_ffn_kernel, ns=ns, ls=ls, tps=tps, nj=nj, has_state=has_state),
        grid=(T // tm, nj + 1),
        in_specs=in_specs,
        out_specs=[
            pl.BlockSpec((tm, D), lambda i, j: (i, 0)),
            pl.BlockSpec((ns, 2, tf), lambda i, j: (i, 0, up(j))),
            pl.BlockSpec((ns, 2, tf), lambda i, j: (i, 0, up(j))),
        ],
        out_shape=[
            jax.ShapeDtypeStruct((T, D), F32),
            jax.ShapeDtypeStruct((T // ls, 2, D_FF), F32),
            jax.ShapeDtypeStruct((T // ls, 2, D_FF), F32),
        ],
        scratch_shapes=[
            pltpu.VMEM((tm, D), BF16),
            pltpu.VMEM((2, ns, 8 + ls, tf), F32),
            pltpu.VMEM((nj, 2, ns, 2, tf), F32),
            pltpu.VMEM((2, tm, tf), BF16),
        ],
        compiler_params=_cparams(("arbitrary", "arbitrary")),
        name="conv_ffn",
    )(*args)
    tails = jnp.concatenate([csg, csu], axis=-1).reshape(n_streams, tps, 2, 2 * D_FF)
    return out, tails[:, tps - 1]


def _rope_table_kernel(c_ref, s_ref):
    shape = c_ref.shape
    half = ROPE_DIM // 2
    pos = lax.broadcasted_iota(jnp.int32, shape, 0).astype(F32)
    lane = lax.broadcasted_iota(jnp.int32, shape, 1)
    fi = (lane & (half - 1)).astype(F32)
    inv = jnp.exp(fi * (-math.log(ROPE_BASE) / half))
    ang = pos * inv
    valid = lane < ROPE_DIM
    c_ref[...] = jnp.where(valid, jnp.cos(ang), 0.0)
    s_ref[...] = jnp.where(valid, jnp.where(lane < half, -jnp.sin(ang), jnp.sin(ang)), 0.0)


def _rope_tables(n_pos):
    return pl.pallas_call(
        _rope_table_kernel,
        out_shape=[jax.ShapeDtypeStruct((n_pos, LANES), F32)] * 2,
        name="rope_tables",
    )()


def _mla_in_kernel(x_ref, nw_ref, wd_ref, qnw_ref, kvnw_ref, wq_ref, c_ref, s_ref,
                   qcat_ref, ckv_ref, kpe_ref):
    h = _rms(x_ref[...], nw_ref[...]).astype(BF16)
    d = _dot(h, wd_ref[...])
    cq = _rms(d[:, :Q_LORA], qnw_ref[...]).astype(BF16)
    ckv_ref[...] = _rms(d[:, Q_LORA:Q_LORA + KV_LORA], kvnw_ref[...])
    cs = c_ref[...]
    sn = s_ref[...]
    o = Q_LORA + KV_LORA
    kpe_ref[...] = d[:, o:o + LANES] * cs + d[:, o + LANES:o + 2 * LANES] * sn
    hw = MLA_HEADS * LANES
    qn = _dot(cq, wq_ref[:, 0:hw])
    pr = _dot(cq, wq_ref[:, hw:2 * hw])
    ps = _dot(cq, wq_ref[:, 2 * hw:3 * hw])
    for hh in range(MLA_HEADS):
        sl = slice(hh * LANES, (hh + 1) * LANES)
        qcat_ref[hh, :, 0:LANES] = qn[:, sl].astype(BF16)
        qcat_ref[hh, :, LANES:2 * LANES] = (pr[:, sl] * cs + ps[:, sl] * sn).astype(BF16)


def _mla_in(x, nw, wd_bf, qnw, kvnw, wq_bf, cos_t, sin_t, tm):
    T, D = x.shape
    n_tab_blk = cos_t.shape[0] // tm
    return pl.pallas_call(
        _mla_in_kernel,
        grid=(T // tm,),
        in_specs=[
            pl.BlockSpec((tm, D), lambda i: (i, 0)),
            pl.BlockSpec((1, D), lambda i: (0, 0)),
            pl.BlockSpec(wd_bf.shape, lambda i: (0, 0)),
            pl.BlockSpec((1, Q_LORA), lambda i: (0, 0)),
            pl.BlockSpec((1, KV_LORA), lambda i: (0, 0)),
            pl.BlockSpec(wq_bf.shape, lambda i: (0, 0)),
            pl.BlockSpec((tm, LANES), lambda i: (i % n_tab_blk, 0)),
            pl.BlockSpec((tm, LANES), lambda i: (i % n_tab_blk, 0)),
        ],
        out_specs=[
            pl.BlockSpec((MLA_HEADS, tm, QK_PAD), lambda i: (0, i, 0)),
            pl.BlockSpec((tm, KV_LORA), lambda i: (i, 0)),
            pl.BlockSpec((tm, LANES), lambda i: (i, 0)),
        ],
        out_shape=[
            jax.ShapeDtypeStruct((MLA_HEADS, T, QK_PAD), BF16),
            jax.ShapeDtypeStruct((T, KV_LORA), F32),
            jax.ShapeDtypeStruct((T, LANES), F32),
        ],
        compiler_params=_cparams(("arbitrary",)),
        name="mla_in_proj",
    )(x, nw, wd_bf, qnw, kvnw, wq_bf, cos_t, sin_t)


def _kv_up_kernel(ckv_ref, kpe_ref, wkv_ref, kcat_ref, v_ref):
    kv = _dot(ckv_ref[...].astype(BF16), wkv_ref[...])
    kp = kpe_ref[...].astype(BF16)
    hw = NOPE_DIM + V_DIM
    for hh in range(MLA_HEADS):
        kcat_ref[hh, :, 0:LANES] = kv[:, hh * hw:hh * hw + NOPE_DIM].astype(BF16)
        kcat_ref[hh, :, LANES:2 * LANES] = kp
        v_ref[hh] = kv[:, hh * hw + NOPE_DIM:(hh + 1) * hw].astype(BF16)


def _kv_up(ckv, kpe_pad, wkv_bf, tm):
    R = ckv.shape[0]
    return pl.pallas_call(
        _kv_up_kernel,
        grid=(R // tm,),
        in_specs=[
            pl.BlockSpec((tm, KV_LORA), lambda i: (i, 0)),
            pl.BlockSpec((tm, LANES), lambda i: (i, 0)),
            pl.BlockSpec(wkv_bf.shape, lambda i: (0, 0)),
        ],
        out_specs=[
            pl.BlockSpec((MLA_HEADS, tm, QK_PAD), lambda i: (0, i, 0)),
            pl.BlockSpec((MLA_HEADS, tm, V_DIM), lambda i: (0, i, 0)),
        ],
        out_shape=[
            jax.ShapeDtypeStruct((MLA_HEADS, R, QK_PAD), BF16),
            jax.ShapeDtypeStruct((MLA_HEADS, R, V_DIM), BF16),
        ],
        compiler_params=_cparams(("arbitrary",)),
        name="mla_kv_up_proj",
    )(ckv, kpe_pad, wkv_bf)


def _attn_prompt_kernel(*refs, L, tq, scale, side_holds):
    n_side = len(side_holds)
    q_ref, k_ref, v_ref = refs[:3]
    o_ref = refs[3 + n_side]
    _side_cast(refs[3:3 + n_side], refs[4 + n_side:], side_holds)
    row = lax.broadcasted_iota(jnp.int32, (tq, tq), 0)
    col = lax.broadcasted_iota(jnp.int32, (tq, tq), 1)
    sh = int(math.log2(CHUNK))
    dmask = (row >> sh) >= (col >> sh)

    c2 = scale * math.log2(math.e)
    def scores(qi):
        n = qi * tq
        q = q_ref[0, n:n + tq, :]
        s_d = jnp.where(dmask, _dot_nt(q, k_ref[0, n:n + tq, :]), -jnp.inf)
        s_p = _dot_nt(q, k_ref[0, 0:n, :]) if qi > 0 else None
        return s_d, s_p

    nq = L // tq
    pending = [scores(t) for t in range(min(SCORES_AHEAD, nq))]
    for qi in range(nq):
        n = qi * tq
        s_d, s_p = pending.pop(0)
        if qi + SCORES_AHEAD < nq:
            pending.append(scores(qi + SCORES_AHEAD))
        m = jnp.max(s_d, axis=-1, keepdims=True)
        if qi > 0:
            m = jnp.maximum(m, jnp.max(s_p, axis=-1, keepdims=True))
        p_d = jnp.exp2((s_d - m) * c2)
        l = jnp.sum(p_d, axis=-1, keepdims=True)
        acc = _dot(p_d.astype(BF16), v_ref[0, n:n + tq, :])
        if qi > 0:
            p_p = jnp.exp2((s_p - m) * c2)
            l = l + jnp.sum(p_p, axis=-1, keepdims=True)
            acc = acc + _dot(p_p.astype(BF16), v_ref[0, 0:n, :])
        o_ref[n:n + tq, :] = (acc / l).astype(BF16)


def _attn_prompt(qcat, kcat, v, B, L, side=()):
    H = MLA_HEADS
    scale = (NOPE_DIM + ROPE_DIM) ** -0.5
    s_in, s_out, s_shape, holds = _side_cast_specs(side, B * H, lambda b, h: b * H + h)
    outs = pl.pallas_call(
        functools.partial(_attn_prompt_kernel, L=L, tq=ATTN_Q_TILE, scale=scale,
                          side_holds=holds),
        grid=(B, H),
        in_specs=[
            pl.BlockSpec((1, L, QK_PAD), lambda b, h: (h, b, 0)),
            pl.BlockSpec((1, L, QK_PAD), lambda b, h: (h, b, 0)),
            pl.BlockSpec((1, L, V_DIM), lambda b, h: (h, b, 0)),
        ] + s_in,
        out_specs=[pl.BlockSpec((L, V_DIM), lambda b, h: (b, h))] + s_out,
        out_shape=[jax.ShapeDtypeStruct((B * L, H * V_DIM), BF16)] + s_shape,
        compiler_params=_cparams(("arbitrary", "arbitrary")),
        name="attn_prompt",
    )(qcat, kcat, v, *[w for w, _ in side])
    return outs[0], outs[1:]


def _attn_sample_kernel(q_ref, cc_ref, pc_ref, cn_ref, pn_ref, wkv_ref, o_ref, *, past, scale):
    H, Lq, _ = q_ref.shape
    hw = NOPE_DIM + V_DIM
    qlat, qrope = [], []
    for hh in range(H):
        wk_h = wkv_ref[:, hh * hw:hh * hw + NOPE_DIM]
        qlat.append(_dot_nt(q_ref[hh, :, 0:NOPE_DIM], wk_h).astype(BF16))
        qrope.append(q_ref[hh, :, NOPE_DIM:NOPE_DIM + ROPE_DIM])
    qlat = jnp.concatenate(qlat, axis=0)
    qrope = jnp.concatenate(qrope, axis=0)
    cc = cc_ref[0].astype(BF16)
    cn = cn_ref[...].astype(BF16)
    s1 = _dot_nt(qlat, cc) + _dot_nt(qrope, pc_ref[0].astype(BF16))
    s2 = _dot_nt(qlat, cn) + _dot_nt(qrope, pn_ref[:, 0:ROPE_DIM].astype(BF16))
    sh = int(math.log2(CHUNK))

    def q_chunk(n):
        t = lax.broadcasted_iota(jnp.int32, (H, Lq, n), 1).reshape(H * Lq, n)
        return (t + past) >> sh

    kc1 = lax.broadcasted_iota(jnp.int32, s1.shape, 1) >> sh
    kc2 = (lax.broadcasted_iota(jnp.int32, s2.shape, 1) + past) >> sh
    s1 = jnp.where(kc1 <= q_chunk(s1.shape[1]), s1 * scale, -jnp.inf)
    s2 = jnp.where(kc2 <= q_chunk(s2.shape[1]), s2 * scale, -jnp.inf)
    m = jnp.maximum(jnp.max(s1, axis=-1, keepdims=True), jnp.max(s2, axis=-1, keepdims=True))
    p1 = jnp.exp(s1 - m)
    p2 = jnp.exp(s2 - m)
    l = jnp.sum(p1, axis=-1, keepdims=True) + jnp.sum(p2, axis=-1, keepdims=True)
    olat = ((_dot(p1.astype(BF16), cc) + _dot(p2.astype(BF16), cn)) / l).astype(BF16)
    for hh in range(H):
        sl = slice(hh * LANES, (hh + 1) * LANES)
        wv_h = wkv_ref[:, hh * hw + NOPE_DIM:(hh + 1) * hw]
        o_ref[:, sl] = _dot(olat[hh * Lq:(hh + 1) * Lq], wv_h).astype(BF16)


def _attn_sample(qcat, ckv_cache, kpe_cache, ckv_new, kpe_new, wkv_bf, B, L, past):
    H = MLA_HEADS
    scale = (NOPE_DIM + ROPE_DIM) ** -0.5
    return pl.pallas_call(
        functools.partial(_attn_sample_kernel, past=past, scale=scale),
        grid=(B,),
        in_specs=[
            pl.BlockSpec((H, L, QK_PAD), lambda b: (0, b, 0)),
            pl.BlockSpec((1, past, KV_LORA), lambda b: (b, 0, 0)),
            pl.BlockSpec((1, past, ROPE_DIM), lambda b: (b, 0, 0)),
            pl.BlockSpec((L, KV_LORA), lambda b: (b, 0)),
            pl.BlockSpec((L, LANES), lambda b: (b, 0)),
            pl.BlockSpec(wkv_bf.shape, lambda b: (0, 0)),
        ],
        out_specs=pl.BlockSpec((L, H * V_DIM), lambda b: (b, 0)),
        out_shape=jax.ShapeDtypeStruct((B * L, H * V_DIM), BF16),
        compiler_params=_cparams(("arbitrary",)),
        name="attn_sample",
    )(qcat, ckv_cache, kpe_cache, ckv_new, kpe_new, wkv_bf)


def _swap_halves(w):
    half = w.shape[-1] // 2
    return jnp.concatenate([w[..., half:], w[..., :half]], axis=-1)


def _pad_lanes(w):
    return jnp.pad(w, [(0, 0)] * (w.ndim - 1) + [(0, LANES - w.shape[-1])])


def _prep_mla_weights(w_down, w_uq, w_ukv):
    o = Q_LORA + KV_LORA
    wpe = w_down[:, o:]
    wd = jnp.concatenate([w_down[:, :o], _pad_lanes(wpe), _pad_lanes(_swap_halves(wpe))], axis=-1)
    wq3 = w_uq.reshape(Q_LORA, MLA_HEADS, NOPE_DIM + ROPE_DIM)
    wn = wq3[..., :NOPE_DIM].reshape(Q_LORA, -1)
    wr = wq3[..., NOPE_DIM:]
    wrp = _pad_lanes(wr).reshape(Q_LORA, -1)
    wrs = _pad_lanes(_swap_halves(wr)).reshape(Q_LORA, -1)
    wq = jnp.concatenate([wn, wrp, wrs], axis=-1)
    return wd.astype(BF16), wq.astype(BF16), w_ukv.astype(BF16)


def _trunk(x, B, L, hg_state, ckv_cache, kpe_cache, conv_state, W, cos_t, sin_t, bf=None,
           p=None):
    norm_w = W["norm_w"]
    nrow = lambda l, k: norm_w[l, k].reshape(1, -1)
    past = 0 if ckv_cache is None else ckv_cache.shape[1]
    T = B * L
    D = x.shape[-1]
    tm, tm_ffn, tm_mla = min(T, TOKEN_TILE), min(T, FFN_TOKEN_TILE), min(T, MLA_TOKEN_TILE)
    convert = bf is None
    bf = {} if convert else dict(bf)
    f32 = W["f32"]

    if p is None:
        p, cast = _hgrn_in(x, nrow(0, 0), W["hgrn_w_in"], W["lb_logits"], 0, tm,
                           [(f32["hgrn_w_out"], 0)] if convert else ())
        if convert:
            bf["hgrn_w_out"], = cast
    z, hg_new, cast = _hgrn_rec(p, W["hgrn_gnorm_w"].reshape(1, -1), hg_state, B, L,
                                [(f32["ffn_w_in"], 0), (f32["ffn_w_out"], 0)] if convert else ())
    if convert:
        bf["ffn0"] = cast
    x = _out_proj(z, bf["hgrn_w_out"], x, nrow(0, 1), tm)
    x, conv0 = _ffn(x, nrow(0, 2), bf["ffn0"][0], W["ffn_conv_w"][0], W["ffn_conv_b"][0],
                    bf["ffn0"][1], nrow(0, 3), None if conv_state is None else conv_state[0],
                    B, L, tm_ffn)

    qcat, ckv, kpe_pad = _mla_in(x, nrow(1, 0), W["mla_wd"], W["mla_q_norm_w"].reshape(1, -1),
                                 W["mla_kv_norm_w"].reshape(1, -1), W["mla_wq"], cos_t, sin_t,
                                 tm_mla)
    if ckv_cache is None:
        kcat_n, v_n = _kv_up(ckv, kpe_pad, W["mla_wkv"], tm_mla)
        o, cast = _attn_prompt(
            qcat, kcat_n, v_n, B, L,
            [(f32["ffn_w_in"], 1), (f32["ffn_w_out"], 1), (f32["mla_w_out"], 0)] if convert
            else ())
        if convert:
            bf["ffn1"], bf["mla_w_out"] = cast[:2], cast[2]
    else:
        o = _attn_sample(qcat, ckv_cache, kpe_cache, ckv, kpe_pad, W["mla_wkv"], B, L, past)
    x = _out_proj(o, bf["mla_w_out"], x, nrow(1, 1), tm)
    x, conv1 = _ffn(x, nrow(1, 2), bf["ffn1"][0], W["ffn_conv_w"][1], W["ffn_conv_b"][1],
                    bf["ffn1"][1], nrow(1, 3), None if conv_state is None else conv_state[1],
                    B, L, tm_ffn)

    return (x.reshape(B, L, D), hg_new[None], ckv.reshape(1, B, L, KV_LORA),
            kpe_pad[:, :ROPE_DIM].reshape(1, B, L, ROPE_DIM), jnp.stack([conv0, conv1]), bf)


def kernel(x_prompt, x_sample, state_hgrn, cache_ckv, cache_kpe, state_conv, norm_w, lb_logits,
           hgrn_w_in, hgrn_gnorm_w, hgrn_w_out, mla_w_down, mla_q_norm_w, mla_kv_norm_w,
           mla_w_uq, mla_w_ukv, mla_w_out, ffn_w_in, ffn_conv_w, ffn_conv_b, ffn_w_out):
    wd, wq, wkv = _prep_mla_weights(mla_w_down[0], mla_w_uq[0], mla_w_ukv[0])
    W = dict(
        norm_w=norm_w, lb_logits=lb_logits,
        hgrn_gnorm_w=hgrn_gnorm_w[0],
        mla_wd=wd, mla_wq=wq, mla_wkv=wkv,
        mla_q_norm_w=mla_q_norm_w[0], mla_kv_norm_w=mla_kv_norm_w[0],
        ffn_conv_w=ffn_conv_w, ffn_conv_b=ffn_conv_b,
        f32=dict(hgrn_w_out=hgrn_w_out, mla_w_out=mla_w_out, ffn_w_in=ffn_w_in,
                 ffn_w_out=ffn_w_out),
    )
    Bp, Lp, D = x_prompt.shape
    Bs, Ls, _ = x_sample.shape
    past = cache_ckv.shape[2]
    cos_t, sin_t = _rope_tables(max(Lp, past + Ls))

    x_s = x_sample.reshape(Bs * Ls, D)
    p_s, (w_in_bf,) = _hgrn_in(x_s, norm_w[0, 0].reshape(1, -1), hgrn_w_in[0], lb_logits, 0,
                               Bs * Ls)
    W["hgrn_w_in"] = w_in_bf

    yp, hgp, ckvp, kpep, convp, bf = _trunk(
        x_prompt.reshape(Bp * Lp, D), Bp, Lp, None, None, None, None, W, cos_t[:Lp], sin_t[:Lp])

    cos_s = jnp.tile(cos_t[past:past + Ls], (Bs, 1))
    sin_s = jnp.tile(sin_t[past:past + Ls], (Bs, 1))
    ys, hgs, ckvs, kpes, convs, _ = _trunk(
        x_s, Bs, Ls, state_hgrn[0], cache_ckv[0], cache_kpe[0], state_conv, W, cos_s, sin_s, bf,
        p_s)
    return (yp, ys, hgp, hgs, ckvp, ckvs, kpep, kpes, convp, convs)
```
